```python
import math
import jax, jax.numpy as jnp
from jax import lax
import numpy as np

D_MODEL = 2048
BATCH = 4
SEQ = 4096
DEPTH = 4

MEM_LEN = 256
FOX_HEAD_DIM = 128
FOX_HEADS = (3 * D_MODEL // 8) // FOX_HEAD_DIM
FOX_W = FOX_HEADS * FOX_HEAD_DIM
DIFF_QK_DIM = 64
DIFF_V_DIM = 2 * DIFF_QK_DIM
DIFF_HEADS = (3 * D_MODEL // 8) // DIFF_V_DIM
DIFF_QK_W = DIFF_HEADS * 2 * DIFF_QK_DIM
DIFF_V_W = DIFF_HEADS * DIFF_V_DIM
CONV_CH = D_MODEL - FOX_W - DIFF_V_W
CONV_WIDTH = 3
MIX_WIDTH = FOX_W + DIFF_V_W + CONV_CH
IN_SIZES = (FOX_W, FOX_W, FOX_W, FOX_HEADS,
            DIFF_QK_W, DIFF_QK_W, DIFF_V_W,
            CONV_CH, CONV_CH, CONV_CH)
IN_WIDTH = sum(IN_SIZES)
IN_SPLITS = [int(v) for v in np.cumsum(IN_SIZES)[:-1]]
ROPE_THETA = 500000.0
ROT_DIM = DIFF_QK_DIM // 4
CROSS_HEADS = 4
CROSS_HEAD_DIM = 128
CROSS_W = CROSS_HEADS * CROSS_HEAD_DIM
D_FF = 5632
Q_BLOCK = 128
EPS = 1e-6
NEG_INF = -1e30

kernel_name = "hymba_fox_diff_conv_macaron"


def rmsnorm(x, g):
    x32 = x.astype(jnp.float32)
    y = x32 * lax.rsqrt(jnp.mean(x32 * x32, axis=-1, keepdims=True) + EPS)
    return (y * g.astype(jnp.float32)).astype(x.dtype)


def swiglu(x, w_gate, w_up, w_down):
    return (jax.nn.silu(x @ w_gate) * (x @ w_up)) @ w_down


def partial_rope(x, cos, sin):
    half = ROT_DIM // 2
    xr = x[..., :ROT_DIM].astype(jnp.float32)
    x1, x2 = xr[..., :half], xr[..., half:]
    rot = jnp.concatenate([x1 * cos - x2 * sin, x2 * cos + x1 * sin], axis=-1).astype(x.dtype)
    return jnp.concatenate([rot, x[..., ROT_DIM:]], axis=-1)


def fox_attention(q, k, v, log_f):
    B, S, H, Dh = q.shape
    nb = S // Q_BLOCK
    c = jnp.cumsum(log_f, axis=1)
    c_keys = c.transpose(0, 2, 1)
    q_blocks = q.reshape(B, nb, Q_BLOCK, H, Dh).swapaxes(0, 1)
    c_blocks = c.reshape(B, nb, Q_BLOCK, H).swapaxes(0, 1)
    key_pos = jnp.arange(S)
    scale = Dh ** -0.5

    def block(args):
        qi, ci, i = args
        s = jnp.einsum('bqhd,bkhd->bhqk', qi, k).astype(jnp.float32) * scale
        s = s + ci.transpose(0, 2, 1)[..., None] - c_keys[:, :, None, :]
        mask = (i * Q_BLOCK + jnp.arange(Q_BLOCK))[:, None] >= key_pos[None, :]
        p = jax.nn.softmax(jnp.where(mask, s, NEG_INF), axis=-1)
        return jnp.einsum('bhqk,bkhd->bqhd', p.astype(v.dtype), v)

    out = lax.map(block, (q_blocks, c_blocks, jnp.arange(nb)))
    return out.swapaxes(0, 1).reshape(B, S, H, Dh)


def diff_attention(q, k, v, lam):
    B, S, H, _, dk = q.shape
    nb = S // Q_BLOCK
    q_blocks = q.reshape(B, nb, Q_BLOCK, H, 2, dk).swapaxes(0, 1)
    key_pos = jnp.arange(S)
    scale = dk ** -0.5

    def block(args):
        qi, i = args
        s = jnp.einsum('bqhcd,bkhcd->bhcqk', qi, k).astype(jnp.float32) * scale
        mask = (i * Q_BLOCK + jnp.arange(Q_BLOCK))[:, None] >= key_pos[None, :]
        p = jax.nn.softmax(jnp.where(mask, s, NEG_INF), axis=-1)
        a = p[:, :, 0] - lam * p[:, :, 1]
        return jnp.einsum('bhqk,bkhd->bqhd', a.astype(v.dtype), v)

    out = lax.map(block, (q_blocks, jnp.arange(nb)))
    return out.swapaxes(0, 1).reshape(B, S, H, v.shape[-1])


def causal_depthwise_conv(z, w, b):
    S = z.shape[1]
    zp = jnp.pad(z, ((0, 0), (CONV_WIDTH - 1, 0), (0, 0)))
    out = zp[:, 0:S] * w[0]
    for j in range(1, CONV_WIDTH):
        out = out + zp[:, j:j + S] * w[j]
    return out + b


def cross_attention(xn, memn, w_q, w_kv, w_o):
    B, S, _ = xn.shape
    M = memn.shape[1]
    q = (xn @ w_q).reshape(B, S, CROSS_HEADS, CROSS_HEAD_DIM)
    kv = (memn @ w_kv).reshape(B, M, 2, CROSS_HEADS, CROSS_HEAD_DIM)
    k, v = kv[:, :, 0], kv[:, :, 1]
    s = jnp.einsum('bqhd,bmhd->bhqm', q, k).astype(jnp.float32) * (CROSS_HEAD_DIM ** -0.5)
    p = jax.nn.softmax(s, axis=-1)
    o = jnp.einsum('bhqm,bmhd->bqhd', p.astype(v.dtype), v).reshape(B, S, CROSS_W)
    return o @ w_o


def setup_inputs(seed: int = 0) -> dict:
    key = jax.random.key(seed)
    ks = jax.random.split(key, 32)

    def w(k, shape, fan_in):
        return jax.random.normal(k, shape, jnp.float32) * (fan_in ** -0.5)

    def gain(k, shape):
        return 1.0 + 0.02 * jax.random.normal(k, shape, jnp.float32)

    L, D = DEPTH, D_MODEL
    x = jax.random.normal(ks[0], (BATCH, SEQ, D), jnp.float32)
    mem = jax.random.normal(ks[1], (BATCH, MEM_LEN, D), jnp.float32)
    offsets = jax.random.randint(ks[2], (BATCH, 1), 0, 1024, dtype=jnp.int32)
    positions = (offsets + jnp.arange(SEQ, dtype=jnp.int32)[None, :]).astype(jnp.int32)
    return {
        "x": x,
        "mem": mem,
        "positions": positions,
        "ffn1_norm": gain(ks[3], (L, D)),
        "ffn1_w_gate": w(ks[4], (L, D, D_FF), D),
        "ffn1_w_up": w(ks[5], (L, D, D_FF), D),
        "ffn1_w_down": w(ks[6], (L, D_FF, D), D_FF),
        "mix_norm": gain(ks[7], (L, D)),
        "mix_w_in": w(ks[8], (L, D, IN_WIDTH), D),
        "forget_bias": jax.random.uniform(ks[9], (L, FOX_HEADS), jnp.float32, 1.0, 4.0),
        "conv_w": w(ks[10], (L, CONV_WIDTH, CONV_CH), CONV_WIDTH),
        "conv_b": 0.01 * jax.random.normal(ks[11], (L, CONV_CH), jnp.float32),
        "lambda_q1": 0.1 * jax.random.normal(ks[12], (L, DIFF_QK_DIM), jnp.float32),
        "lambda_k1": 0.1 * jax.random.normal(ks[13], (L, DIFF_QK_DIM), jnp.float32),
        "lambda_q2": 0.1 * jax.random.normal(ks[14], (L, DIFF_QK_DIM), jnp.float32),
        "lambda_k2": 0.1 * jax.random.normal(ks[15], (L, DIFF_QK_DIM), jnp.float32),
        "diff_subln": gain(ks[16], (L, DIFF_V_DIM)),
        "mix_w_out": w(ks[17], (L, MIX_WIDTH, D), MIX_WIDTH),
        "cross_norm": gain(ks[18], (L, D)),
        "mem_norm": gain(ks[19], (L, D)),
        "cross_w_q": w(ks[20], (L, D, CROSS_W), D),
        "cross_w_kv": w(ks[21], (L, D, 2 * CROSS_W), D),
        "cross_w_o": w(ks[22], (L, CROSS_W, D), CROSS_W),
        "ffn2_norm": gain(ks[23], (L, D)),
        "ffn2_w_gate": w(ks[24], (L, D, D_FF), D),
        "ffn2_w_up": w(ks[25], (L, D, D_FF), D),
        "ffn2_w_down": w(ks[26], (L, D_FF, D), D_FF),
        "final_norm": gain(ks[27], (D,)),
    }


def reference(x, mem, positions, ffn1_norm, ffn1_w_gate, ffn1_w_up, ffn1_w_down,
              mix_norm, mix_w_in, forget_bias, conv_w, conv_b,
              lambda_q1, lambda_k1, lambda_q2, lambda_k2, diff_subln, mix_w_out,
              cross_norm, mem_norm, cross_w_q, cross_w_kv, cross_w_o,
              ffn2_norm, ffn2_w_gate, ffn2_w_up, ffn2_w_down, final_norm):
    B, S, _ = x.shape
    inv_freq = ROPE_THETA ** (-jnp.arange(0, ROT_DIM, 2, dtype=jnp.float32) / ROT_DIM)
    ang = positions.astype(jnp.float32)[..., None] * inv_freq
    cos = jnp.cos(ang)[:, :, None, None, :]
    sin = jnp.sin(ang)[:, :, None, None, :]

    h = x
    for l in range(DEPTH):
        h = h + 0.5 * swiglu(rmsnorm(h, ffn1_norm[l]), ffn1_w_gate[l], ffn1_w_up[l], ffn1_w_down[l])

        n = rmsnorm(h, mix_norm[l])
        proj = n @ mix_w_in[l]
        fq, fk, fv, ff, dq, dk, dv, gb, gc, hc = jnp.split(proj, IN_SPLITS, axis=-1)

        log_f = jax.nn.log_sigmoid(ff.astype(jnp.float32) + forget_bias[l].astype(jnp.float32))
        fox = fox_attention(fq.reshape(B, S, FOX_HEADS, FOX_HEAD_DIM),
                            fk.reshape(B, S, FOX_HEADS, FOX_HEAD_DIM),
                            fv.reshape(B, S, FOX_HEADS, FOX_HEAD_DIM), log_f)
        fox = fox.reshape(B, S, FOX_W)

        lam_init = 0.8 - 0.6 * math.exp(-0.3 * l)
        lam = (jnp.exp(jnp.sum(lambda_q1[l].astype(jnp.float32) * lambda_k1[l].astype(jnp.float32)))
               - jnp.exp(jnp.sum(lambda_q2[l].astype(jnp.float32) * lambda_k2[l].astype(jnp.float32)))
               + lam_init)
        dq = partial_rope(dq.reshape(B, S, DIFF_HEADS, 2, DIFF_QK_DIM), cos, sin)
        dk = partial_rope(dk.reshape(B, S, DIFF_HEADS, 2, DIFF_QK_DIM), cos, sin)
        diff = diff_attention(dq, dk, dv.reshape(B, S, DIFF_HEADS, DIFF_V_DIM), lam)
        diff = (rmsnorm(diff, diff_subln[l]) * (1.0 - lam_init)).reshape(B, S, DIFF_V_W)

        conv = gb * causal_depthwise_conv(gc * hc, conv_w[l], conv_b[l])

        mixed = jnp.concatenate([fox, diff, conv], axis=-1) @ mix_w_out[l]
        h = h + mixed

        h = h + cross_attention(rmsnorm(h, cross_norm[l]), rmsnorm(mem, mem_norm[l]),
                                cross_w_q[l], cross_w_kv[l], cross_w_o[l])

        h = h + 0.5 * swiglu(rmsnorm(h, ffn2_norm[l]), ffn2_w_gate[l], ffn2_w_up[l], ffn2_w_down[l])

    return rmsnorm(h, final_norm)
```

```python
import functools
import math

import jax
import jax.numpy as jnp
from jax import lax
from jax.experimental import pallas as pl
from jax.experimental.pallas import tpu as pltpu

F32 = jnp.float32
BF16 = jnp.bfloat16

FOX_HEADS = 6
FOX_HEAD_DIM = 128
DIFF_HEADS = 6
DIFF_QK_DIM = 64
DIFF_V_DIM = 128
CONV_CH = 512
CONV_WIDTH = 3
ROT_DIM = 16
ROPE_THETA = 500000.0
CROSS_HEADS = 4
CROSS_HEAD_DIM = 128
EPS = 1e-6
NEG_INF = -1e30

LANES = 128
SUBLANES = 8

TOKEN_TILE = 512
FF_TILE = 512
IN_TILE = 768
CONV_TILE = 256
ATT_TILE = 512
VMEM_LIMIT = 56 * 1024 * 1024

_ARB = "arbitrary"


def _params(n_axes):
    return pltpu.CompilerParams(dimension_semantics=(_ARB,) * n_axes,
                                vmem_limit_bytes=VMEM_LIMIT)


def _rms(x, g):
    return x * lax.rsqrt(jnp.mean(x * x, axis=-1, keepdims=True) + EPS) * g


def _dot(a, b):
    return jnp.dot(a, b, preferred_element_type=F32)


def _dot_t(a, b):
    return lax.dot_general(a, b, (((1,), (1,)), ((), ())), preferred_element_type=F32)


def _ffn_body(x_ref, g_ref, wg_ref, wu_ref, wd_ref, fg_ref, o_ref, n_ref, acc_ref, *, final):
    j = pl.program_id(1)

    @pl.when(j == 0)
    def _():
        n_ref[...] = _rms(x_ref[...], g_ref[...]).astype(BF16)
        acc_ref[...] = jnp.zeros_like(acc_ref)

    n = n_ref[...]
    gate = _dot(n, wg_ref[...])
    up = _dot(n, wu_ref[...])
    act = (gate * jax.nn.sigmoid(gate) * up).astype(BF16)
    acc_ref[...] += _dot(act, wd_ref[...])

    @pl.when(j == pl.num_programs(1) - 1)
    def _():
        y = x_ref[...] + 0.5 * acc_ref[...]
        if final:
            y = _rms(y, fg_ref[...])
        o_ref[...] = y


def _ffn(h, g, wg, wu, wd, final_g, *, final):
    n_tok, d = h.shape
    d_ff = wg.shape[1]
    tm, tf = TOKEN_TILE, FF_TILE
    return pl.pallas_call(
        functools.partial(_ffn_body, final=final),
        grid=(n_tok // tm, d_ff // tf),
        in_specs=[
            pl.BlockSpec((tm, d), lambda i, j: (i, 0)),
            pl.BlockSpec((1, d), lambda i, j: (0, 0)),
            pl.BlockSpec((d, tf), lambda i, j: (0, j)),
            pl.BlockSpec((d, tf), lambda i, j: (0, j)),
            pl.BlockSpec((tf, d), lambda i, j: (j, 0)),
            pl.BlockSpec((1, d), lambda i, j: (0, 0)),
        ],
        out_specs=pl.BlockSpec((tm, d), lambda i, j: (i, 0)),
        out_shape=jax.ShapeDtypeStruct((n_tok, d), F32),
        scratch_shapes=[pltpu.VMEM((tm, d), BF16), pltpu.VMEM((tm, d), F32)],
        compiler_params=_params(2),
        name="ffn_final" if final else "ffn",
    )(h, g, wg, wu, wd, final_g)


def _inproj_body(x_ref, g_ref, w_ref, wff_ref, fb_ref, ct_ref, st_ref, cw_ref, cb_ref,
                 qkv_ref, conv_ref, c_ref, n_ref, zbuf_ref, carry_ref, *, tiles_per_seq):
    i = pl.program_id(0)
    j = pl.program_id(1)
    tm = x_ref.shape[0]
    seq_start = (i % tiles_per_seq) == 0

    @pl.when(j == 0)
    def _():
        n_ref[...] = _rms(x_ref[...], g_ref[...]).astype(BF16)

    n = n_ref[...]
    y = _dot(n, w_ref[...])

    @pl.when(j == 0)
    def _():
        logf = jax.nn.log_sigmoid(_dot(n, wff_ref[...]) + fb_ref[...])
        row = lax.broadcasted_iota(jnp.int32, (tm, tm), 0)
        col = lax.broadcasted_iota(jnp.int32, (tm, tm), 1)
        tri = jnp.where(row >= col, 1.0, 0.0).astype(BF16)
        hi = logf.astype(BF16)
        r1 = logf - hi.astype(F32)
        mid = r1.astype(BF16)
        lo = (r1 - mid.astype(F32)).astype(BF16)
        prev = jnp.where(seq_start, 0.0, carry_ref[...])
        c = _dot(tri, hi) + _dot(tri, mid) + _dot(tri, lo) + prev
        c_ref[...] = c
        carry_ref[...] = c[tm - 1:tm, :]

    @pl.when((j <= 2) | (j == 5))
    def _():
        qkv_ref[...] = y.astype(BF16)

    @pl.when((j == 3) | (j == 4))
    def _():
        lane = lax.broadcasted_iota(jnp.int32, (tm, LANES), 1)
        first_half = (lane % DIFF_QK_DIM) < (ROT_DIM // 2)
        ct = ct_ref[...]
        st = st_ref[...]
        for c0 in range(0, IN_TILE, LANES):
            yc = y[:, c0:c0 + LANES]
            partner = jnp.where(first_half,
                                pltpu.roll(yc, LANES - ROT_DIM // 2, axis=1),
                                pltpu.roll(yc, ROT_DIM // 2, axis=1))
            qkv_ref[:, c0:c0 + LANES] = (yc * ct + partner * st).astype(BF16)

    for t in range(CONV_CH // CONV_TILE):
        @pl.when(j == 6 + t)
        def _(t=t):
            gb = y[:, 0:CONV_TILE]
            z = y[:, CONV_TILE:2 * CONV_TILE] * y[:, 2 * CONV_TILE:3 * CONV_TILE]
            zb = zbuf_ref.at[t]

            @pl.when(seq_start)
            def _():
                zb[0:SUBLANES, :] = jnp.zeros((SUBLANES, CONV_TILE), F32)

            zb[SUBLANES:SUBLANES + tm, :] = z
            z1 = zb[SUBLANES - 1:SUBLANES - 1 + tm, :]
            z2 = zb[SUBLANES - 2:SUBLANES - 2 + tm, :]
            cw = cw_ref[...]
            conv = z2 * cw[0:1, :] + z1 * cw[1:2, :] + z * cw[2:3, :] + cb_ref[...]
            conv_ref[...] = (gb * conv).astype(BF16)
            zb[0:SUBLANES, :] = z[tm - SUBLANES:tm, :]


def _inproj(h, g, w, wff, fbias, ctab, stab, conv_w, conv_b, *, seq_len):
    n_tok, d = h.shape
    tm = TOKEN_TILE
    n_col = w.shape[1] // IN_TILE
    n_plain = n_col - CONV_CH // CONV_TILE
    conv_idx = lambda i, j: (0, jnp.clip(j - n_plain, 0, CONV_CH // CONV_TILE - 1))
    return pl.pallas_call(
        functools.partial(_inproj_body, tiles_per_seq=seq_len // tm),
        grid=(n_tok // tm, n_col),
        in_specs=[
            pl.BlockSpec((tm, d), lambda i, j: (i, 0)),
            pl.BlockSpec((1, d), lambda i, j: (0, 0)),
            pl.BlockSpec((d, IN_TILE), lambda i, j: (0, j)),
            pl.BlockSpec((d, LANES), lambda i, j: (0, 0)),
            pl.BlockSpec((1, LANES), lambda i, j: (0, 0)),
            pl.BlockSpec((tm, LANES), lambda i, j: (i, 0)),
            pl.BlockSpec((tm, LANES), lambda i, j: (i, 0)),
            pl.BlockSpec((CONV_WIDTH, CONV_TILE), conv_idx),
            pl.BlockSpec((1, CONV_TILE), conv_idx),
        ],
        out_specs=[
            pl.BlockSpec((tm, IN_TILE), lambda i, j: (i, jnp.minimum(j, n_plain - 1))),
            pl.BlockSpec((tm, CONV_TILE),
                         lambda i, j: (i, jnp.clip(j - n_plain, 0, CONV_CH // CONV_TILE - 1))),
            pl.BlockSpec((tm, LANES), lambda i, j: (i, 0)),
        ],
        out_shape=[
            jax.ShapeDtypeStruct((n_tok, n_plain * IN_TILE), BF16),
            jax.ShapeDtypeStruct((n_tok, CONV_CH), BF16),
            jax.ShapeDtypeStruct((n_tok, LANES), F32),
        ],
        scratch_shapes=[
            pltpu.VMEM((tm, d), BF16),
            pltpu.VMEM((CONV_CH // CONV_TILE, tm + SUBLANES, CONV_TILE), F32),
            pltpu.VMEM((1, LANES), F32),
        ],
        compiler_params=_params(2),
        name="inproj",
    )(h, g, w, wff, fbias, ctab, stab, conv_w, conv_b)


def _causal_mask(t):
    row = lax.broadcasted_iota(jnp.int32, (t, t), 0)
    col = lax.broadcasted_iota(jnp.int32, (t, t), 1)
    return row >= col


def _online_update(s, v, m_ref, l_ref, acc_ref):
    m_old = m_ref[...]
    m_new = jnp.maximum(m_old, jnp.max(s, axis=-1, keepdims=True))
    alpha = jnp.exp(m_old - m_new)
    p = jnp.exp(s - m_new)
    l_ref[...] = alpha * l_ref[...] + jnp.sum(p, axis=-1, keepdims=True)
    acc_ref[...] = alpha * acc_ref[...] + _dot(p.astype(BF16), v)
    m_ref[...] = m_new


def _fox_body(q_ref, k_ref, v_ref, cq_ref, ck_ref, o_ref, m_ref, l_ref, acc_ref):
    h = pl.program_id(1)
    qi = pl.program_id(2)
    t = q_ref.shape[0]
    scale = FOX_HEAD_DIM ** -0.5
    q = q_ref[...]
    lane = lax.broadcasted_iota(jnp.int32, (t, LANES), 1)
    cq = jnp.sum(jnp.where(lane == h, cq_ref[...], 0.0), axis=-1, keepdims=True)

    m_ref[...] = jnp.full_like(m_ref, NEG_INF)
    l_ref[...] = jnp.zeros_like(l_ref)
    acc_ref[...] = jnp.zeros_like(acc_ref)

    def step(kb, masked):
        rows = pl.ds(pl.multiple_of(kb * t, t), t)
        s = _dot_t(q, k_ref[rows, :]) * scale + cq - ck_ref[0, kb]
        if masked:
            s = jnp.where(_causal_mask(t), s, NEG_INF)
        _online_update(s, v_ref[rows, :], m_ref, l_ref, acc_ref)

    def loop_body(kb, carry):
        step(kb, False)
        return carry

    lax.fori_loop(0, qi, loop_body, 0)
    step(qi, True)
    o_ref[...] = (acc_ref[...] / l_ref[...]).astype(BF16)


def _fox_attention(qkv, c, c_keys, *, batch, seq_len):
    t = ATT_TILE
    nq = seq_len // t
    hd = FOX_HEAD_DIM
    return pl.pallas_call(
        _fox_body,
        grid=(batch, FOX_HEADS, nq),
        in_specs=[
            pl.BlockSpec((t, hd), lambda b, h, qi: (b * nq + qi, h)),
            pl.BlockSpec((seq_len, hd), lambda b, h, qi: (b, FOX_HEADS + h)),
            pl.BlockSpec((seq_len, hd), lambda b, h, qi: (b, 2 * FOX_HEADS + h)),
            pl.BlockSpec((t, LANES), lambda b, h, qi: (b * nq + qi, 0)),
            pl.BlockSpec((1, nq, 1, t), lambda b, h, qi: (b * FOX_HEADS + h, 0, 0, 0)),
        ],
        out_specs=pl.BlockSpec((t, hd), lambda b, h, qi: (b * nq + qi, h)),
        out_shape=jax.ShapeDtypeStruct((batch * seq_len, FOX_HEADS * hd), BF16),
        scratch_shapes=[pltpu.VMEM((t, 1), F32), pltpu.VMEM((t, 1), F32),
                        pltpu.VMEM((t, hd), F32)],
        compiler_params=_params(3),
        name="fox_attention",
    )(qkv, qkv, qkv, c, c_keys)


def _diff_body(q_ref, k_ref, v_ref, lam_ref, sg_ref, o_ref,
               m1_ref, l1_ref, a1_ref, m2_ref, l2_ref, a2_ref, *, lam_init):
    qi = pl.program_id(2)
    t = q_ref.shape[0]
    scale = DIFF_QK_DIM ** -0.5
    q = q_ref[...]
    lane = lax.broadcasted_iota(jnp.int32, q.shape, 1)
    zero = jnp.zeros_like(q)
    q1 = jnp.where(lane < DIFF_QK_DIM, q, zero)
    q2 = jnp.where(lane >= DIFF_QK_DIM, q, zero)

    for m_ref, l_ref, a_ref in ((m1_ref, l1_ref, a1_ref), (m2_ref, l2_ref, a2_ref)):
        m_ref[...] = jnp.full_like(m_ref, NEG_INF)
        l_ref[...] = jnp.zeros_like(l_ref)
        a_ref[...] = jnp.zeros_like(a_ref)

    def step(kb, masked):
        rows = pl.ds(pl.multiple_of(kb * t, t), t)
        k = k_ref[rows, :]
        v = v_ref[rows, :]
        for qc, m_ref, l_ref, a_ref in ((q1, m1_ref, l1_ref, a1_ref), (q2, m2_ref, l2_ref, a2_ref)):
            s = _dot_t(qc, k) * scale
            if masked:
                s = jnp.where(_causal_mask(t), s, NEG_INF)
            _online_update(s, v, m_ref, l_ref, a_ref)

    def loop_body(kb, carry):
        step(kb, False)
        return carry

    lax.fori_loop(0, qi, loop_body, 0)
    step(qi, True)

    lv = lam_ref[...]
    lam = (jnp.exp(jnp.sum(lv[0:1, :] * lv[1:2, :], axis=-1, keepdims=True))
           - jnp.exp(jnp.sum(lv[2:3, :] * lv[3:4, :], axis=-1, keepdims=True))
           + lam_init)
    o = a1_ref[...] / l1_ref[...] - lam * (a2_ref[...] / l2_ref[...])
    o_ref[...] = (_rms(o, sg_ref[...]) * (1.0 - lam_init)).astype(BF16)


def _diff_attention(qkv, lam_vecs, subln, *, batch, seq_len, lam_init):
    t = ATT_TILE
    nq = seq_len // t
    hd = DIFF_V_DIM
    base = 3 * FOX_HEADS
    stat = pltpu.VMEM((t, 1), F32)
    acc = pltpu.VMEM((t, hd), F32)
    return pl.pallas_call(
        functools.partial(_diff_body, lam_init=lam_init),
        grid=(batch, DIFF_HEADS, nq),
        in_specs=[
            pl.BlockSpec((t, hd), lambda b, h, qi: (b * nq + qi, base + h)),
            pl.BlockSpec((seq_len, hd), lambda b, h, qi: (b, base + DIFF_HEADS + h)),
            pl.BlockSpec((seq_len, hd), lambda b, h, qi: (b, base + 2 * DIFF_HEADS + h)),
            pl.BlockSpec((4, DIFF_QK_DIM), lambda b, h, qi: (0, 0)),
            pl.BlockSpec((1, hd), lambda b, h, qi: (0, 0)),
        ],
        out_specs=pl.BlockSpec((t, hd), lambda b, h, qi: (b * nq + qi, h)),
        out_shape=jax.ShapeDtypeStruct((batch * seq_len, DIFF_HEADS * hd), BF16),
        scratch_shapes=[stat, stat, acc, stat, stat, acc],
        compiler_params=_params(3),
        name="diff_attention",
    )(qkv, qkv, qkv, lam_vecs, subln)


def _mixout_body(h_ref, fox_ref, diff_ref, conv_ref, wf_ref, wd_ref, wc_ref, o_ref):
    o_ref[...] = (h_ref[...] + _dot(fox_ref[...], wf_ref[...])
                  + _dot(diff_ref[...], wd_ref[...]) + _dot(conv_ref[...], wc_ref[...]))


def _mixout(h, fox, diff, conv, w_fox, w_diff, w_conv):
    n_tok, d = h.shape
    tm = TOKEN_TILE
    row = lambda i: (i, 0)
    const = lambda i: (0, 0)
    resident = functools.partial(pl.BlockSpec, index_map=const, pipeline_mode=pl.Buffered(1))
    return pl.pallas_call(
        _mixout_body,
        grid=(n_tok // tm,),
        in_specs=[
            pl.BlockSpec((tm, d), row),
            pl.BlockSpec((tm, fox.shape[1]), row),
            pl.BlockSpec((tm, diff.shape[1]), row),
            pl.BlockSpec((tm, conv.shape[1]), row),
            resident(w_fox.shape),
            resident(w_diff.shape),
            resident(w_conv.shape),
        ],
        out_specs=pl.BlockSpec((tm, d), row),
        out_shape=jax.ShapeDtypeStruct((n_tok, d), F32),
        compiler_params=_params(1),
        name="mixout",
    )(h, fox, diff, conv, w_fox, w_diff, w_conv)


def _memkv_body(m_ref, g_ref, w_ref, o_ref):
    o_ref[...] = _dot(_rms(m_ref[...], g_ref[...]).astype(BF16), w_ref[...]).astype(BF16)


def _memkv(mem, g, w):
    n_mem, d = mem.shape
    tm = min(n_mem, TOKEN_TILE)
    return pl.pallas_call(
        _memkv_body,
        grid=(n_mem // tm,),
        in_specs=[
            pl.BlockSpec((tm, d), lambda i: (i, 0)),
            pl.BlockSpec((1, d), lambda i: (0, 0)),
            pl.BlockSpec(w.shape, lambda i: (0, 0), pipeline_mode=pl.Buffered(1)),
        ],
        out_specs=pl.BlockSpec((tm, w.shape[1]), lambda i: (i, 0)),
        out_shape=jax.ShapeDtypeStruct((n_mem, w.shape[1]), BF16),
        compiler_params=_params(1),
        name="memkv",
    )(mem, g, w)


def _cross_body(h_ref, g_ref, wq_ref, kv_ref, wo_ref, o_ref):
    hd = CROSS_HEAD_DIM
    width = CROSS_HEADS * hd
    scale = hd ** -0.5
    x = h_ref[...]
    q = _dot(_rms(x, g_ref[...]).astype(BF16), wq_ref[...]).astype(BF16)
    heads = []
    for hh in range(CROSS_HEADS):
        k = kv_ref[:, hh * hd:(hh + 1) * hd]
        v = kv_ref[:, width + hh * hd:width + (hh + 1) * hd]
        s = _dot_t(q[:, hh * hd:(hh + 1) * hd], k) * scale
        e = jnp.exp(s - jnp.max(s, axis=-1, keepdims=True))
        p = e / jnp.sum(e, axis=-1, keepdims=True)
        heads.append(_dot(p.astype(BF16), v))
    o = jnp.concatenate(heads, axis=-1).astype(BF16)
    o_ref[...] = x + _dot(o, wo_ref[...])


def _cross(h, g, wq, kv, wo, *, seq_len, mem_len):
    n_tok, d = h.shape
    tm = TOKEN_TILE
    tiles_per_seq = seq_len // tm
    const = lambda i: (0, 0)
    return pl.pallas_call(
        _cross_body,
        grid=(n_tok // tm,),
        in_specs=[
            pl.BlockSpec((tm, d), lambda i: (i, 0)),
            pl.BlockSpec((1, d), const),
            pl.BlockSpec(wq.shape, const, pipeline_mode=pl.Buffered(1)),
            pl.BlockSpec((mem_len, kv.shape[1]), lambda i: (i // tiles_per_seq, 0)),
            pl.BlockSpec(wo.shape, const, pipeline_mode=pl.Buffered(1)),
        ],
        out_specs=pl.BlockSpec((tm, d), lambda i: (i, 0)),
        out_shape=jax.ShapeDtypeStruct((n_tok, d), F32),
        compiler_params=_params(1),
        name="cross_attention",
    )(h, g, wq, kv, wo)


def _rope_tables(positions):
    half = ROT_DIM // 2
    inv_freq = ROPE_THETA ** (-jnp.arange(0, ROT_DIM, 2, dtype=F32) / ROT_DIM)
    ang = positions.astype(F32)[..., None] * inv_freq
    cos = jnp.cos(ang).reshape(-1, half)
    sin = jnp.sin(ang).reshape(-1, half)
    n = cos.shape[0]
    rest = DIFF_QK_DIM - ROT_DIM
    ct = jnp.concatenate([cos, cos, jnp.ones((n, rest), F32)], axis=-1)
    st = jnp.concatenate([-sin, sin, jnp.zeros((n, rest), F32)], axis=-1)
    reps = LANES // DIFF_QK_DIM
    return jnp.tile(ct, (1, reps)), jnp.tile(st, (1, reps))


def _arrange_in_weights(w_in):
    fw, dw, cc = FOX_HEADS * FOX_HEAD_DIM, DIFF_HEADS * DIFF_V_DIM, CONV_CH
    o = 0
    fq, fk, fv = (w_in[..., o + k * fw:o + (k + 1) * fw] for k in range(3))
    o += 3 * fw
    ff = w_in[..., o:o + FOX_HEADS]
    o += FOX_HEADS
    dq, dk, dv = (w_in[..., o + k * dw:o + (k + 1) * dw] for k in range(3))
    o += 3 * dw
    gb, gc, hc = (w_in[..., o + k * cc:o + (k + 1) * cc] for k in range(3))
    parts = [fq, fk, fv, dq, dk, dv]
    for t in range(cc // CONV_TILE):
        sl = slice(t * CONV_TILE, (t + 1) * CONV_TILE)
        parts += [gb[..., sl], gc[..., sl], hc[..., sl]]
    w = jnp.concatenate(parts, axis=-1).astype(BF16)
    wff = jnp.pad(ff, ((0, 0), (0, 0), (0, LANES - FOX_HEADS))).astype(BF16)
    return w, wff


def kernel(x, mem, positions, ffn1_norm, ffn1_w_gate, ffn1_w_up, ffn1_w_down, mix_norm, mix_w_in, forget_bias, conv_w, conv_b, lambda_q1, lambda_k1, lambda_q2, lambda_k2, diff_subln, mix_w_out, cross_norm, mem_norm, cross_w_q, cross_w_kv, cross_w_o, ffn2_norm, ffn2_w_gate, ffn2_w_up, ffn2_w_down, final_norm):
    batch, seq_len, d = x.shape
    mem_len = mem.shape[1]
    depth = ffn1_norm.shape[0]
    n_tok = batch * seq_len
    assert seq_len % TOKEN_TILE == 0 and seq_len % ATT_TILE == 0
    assert ffn1_w_gate.shape[2] % FF_TILE == 0

    bf = lambda a: a.astype(BF16)
    w1g, w1u, w1d = bf(ffn1_w_gate), bf(ffn1_w_up), bf(ffn1_w_down)
    w2g, w2u, w2d = bf(ffn2_w_gate), bf(ffn2_w_up), bf(ffn2_w_down)
    w_in, w_ff = _arrange_in_weights(mix_w_in)
    w_out = bf(mix_w_out)
    fw, dw = FOX_HEADS * FOX_HEAD_DIM, DIFF_HEADS * DIFF_V_DIM
    wq, wkv, wo = bf(cross_w_q), bf(cross_w_kv), bf(cross_w_o)
    fbias = jnp.pad(forget_bias.astype(F32), ((0, 0), (0, LANES - FOX_HEADS)))
    lam_vecs = jnp.stack([lambda_q1, lambda_k1, lambda_q2, lambda_k2], axis=1).astype(F32)
    ctab, stab = _rope_tables(positions)
    row = lambda a, l: a[l][None, :]

    h = x.reshape(n_tok, d)
    mem2 = mem.reshape(batch * mem_len, d)
    nq = seq_len // ATT_TILE
    for l in range(depth):
        h = _ffn(h, row(ffn1_norm, l), w1g[l], w1u[l], w1d[l], row(ffn1_norm, l), final=False)

        qkv, conv, c = _inproj(h, row(mix_norm, l), w_in[l], w_ff[l], fbias[l][None, :], ctab, stab,
                               conv_w[l], row(conv_b, l), seq_len=seq_len)
        c_keys = (c.reshape(batch, seq_len, LANES)[:, :, :FOX_HEADS]
                  .transpose(0, 2, 1).reshape(batch * FOX_HEADS, nq, 1, ATT_TILE))
        fox = _fox_attention(qkv, c, c_keys, batch=batch, seq_len=seq_len)
        lam_init = 0.8 - 0.6 * math.exp(-0.3 * l)
        diff = _diff_attention(qkv, lam_vecs[l], row(diff_subln, l), batch=batch, seq_len=seq_len,
                               lam_init=lam_init)
        h = _mixout(h, fox, diff, conv, w_out[l, :fw], w_out[l, fw:fw + dw], w_out[l, fw + dw:])

        kv = _memkv(mem2, row(mem_norm, l), wkv[l])
        h = _cross(h, row(cross_norm, l), wq[l], kv, wo[l], seq_len=seq_len, mem_len=mem_len)

        last = l == depth - 1
        h = _ffn(h, row(ffn2_norm, l), w2g[l], w2u[l], w2d[l], final_norm[None, :], final=last)
    return h.reshape(batch, seq_len, d)
```

```python
import functools
import math

import jax
import jax.numpy as jnp
from jax import lax
from jax.experimental import pallas as pl
from jax.experimental.pallas import tpu as pltpu

F32 = jnp.float32
BF16 = jnp.bfloat16

FOX_HEADS = 6
FOX_HEAD_DIM = 128
DIFF_HEADS = 6
DIFF_QK_DIM = 64
DIFF_V_DIM = 128
CONV_CH = 512
CONV_WIDTH = 3
ROT_DIM = 16
ROPE_THETA = 500000.0
CROSS_HEADS = 4
CROSS_HEAD_DIM = 128
EPS = 1e-6
NEG_INF = -1e30
LOG2E = math.log2(math.e)
FOX_Q_SCALE = FOX_HEAD_DIM ** -0.5 * LOG2E
DIFF_Q_SCALE = DIFF_QK_DIM ** -0.5 * LOG2E

LANES = 128
SUBLANES = 8
BF16_ROWS = 16

TOKEN_TILE = 512
FF_TILE = 512
IN_TILE = 768
CONV_TILE = 256
ATT_TILE = 512
ONES_ROWS = BF16_ROWS
VMEM_LIMIT = 56 * 1024 * 1024

_ARB = "arbitrary"


def _params(n_axes):
    return pltpu.CompilerParams(dimension_semantics=(_ARB,) * n_axes,
                                vmem_limit_bytes=VMEM_LIMIT)


def _rms(x, g):
    return x * lax.rsqrt(jnp.mean(x * x, axis=-1, keepdims=True) + EPS) * g


def _dot(a, b):
    return jnp.dot(a, b, preferred_element_type=F32)


def _dot_t(a, b):
    return lax.dot_general(a, b, (((1,), (1,)), ((), ())), preferred_element_type=F32)


def _ffn_body(x_ref, g_ref, wg_ref, wu_ref, wd_ref, fg_ref, o_ref, n_ref, acc_ref, *, final):
    j = pl.program_id(1)

    @pl.when(j == 0)
    def _():
        n_ref[...] = _rms(x_ref[...], g_ref[...]).astype(BF16)
        acc_ref[...] = jnp.zeros_like(acc_ref)

    n = n_ref[...]
    gate = _dot(n, wg_ref[...])
    up = _dot(n, wu_ref[...])
    act = (gate * jax.nn.sigmoid(gate) * up).astype(BF16)
    acc_ref[...] += _dot(act, wd_ref[...])

    @pl.when(j == pl.num_programs(1) - 1)
    def _():
        y = x_ref[...] + 0.5 * acc_ref[...]
        if final:
            y = _rms(y, fg_ref[...])
        o_ref[...] = y


def _ffn(h, g, wg, wu, wd, final_g, *, final):
    n_tok, d = h.shape
    d_ff = wg.shape[1]
    tm, tf = TOKEN_TILE, FF_TILE
    return pl.pallas_call(
        functools.partial(_ffn_body, final=final),
        grid=(n_tok // tm, d_ff // tf),
        in_specs=[
            pl.BlockSpec((tm, d), lambda i, j: (i, 0)),
            pl.BlockSpec((1, d), lambda i, j: (0, 0)),
            pl.BlockSpec((d, tf), lambda i, j: (0, j)),
            pl.BlockSpec((d, tf), lambda i, j: (0, j)),
            pl.BlockSpec((tf, d), lambda i, j: (j, 0)),
            pl.BlockSpec((1, d), lambda i, j: (0, 0)),
        ],
        out_specs=pl.BlockSpec((tm, d), lambda i, j: (i, 0)),
        out_shape=jax.ShapeDtypeStruct((n_tok, d), F32),
        scratch_shapes=[pltpu.VMEM((tm, d), BF16), pltpu.VMEM((tm, d), F32)],
        compiler_params=_params(2),
        name="ffn_final" if final else "ffn",
    )(h, g, wg, wu, wd, final_g)


def _inproj_body(x_ref, g_ref, w_ref, wff_ref, fb_ref, ct_ref, st_ref, cw_ref, cb_ref,
                 qkv_ref, conv_ref, c_ref, n_ref, zbuf_ref, carry_ref, *, tiles_per_seq):
    i = pl.program_id(0)
    j = pl.program_id(1)
    tm = x_ref.shape[0]
    seq_start = (i % tiles_per_seq) == 0

    @pl.when(j == 0)
    def _():
        n_ref[...] = _rms(x_ref[...], g_ref[...]).astype(BF16)

    n = n_ref[...]
    y = _dot(n, w_ref[...])

    @pl.when(j == 0)
    def _():
        logf = jax.nn.log_sigmoid(_dot(n, wff_ref[...]) + fb_ref[...])
        row = lax.broadcasted_iota(jnp.int32, (tm, tm), 0)
        col = lax.broadcasted_iota(jnp.int32, (tm, tm), 1)
        tri = jnp.where(row >= col, 1.0, 0.0).astype(BF16)
        hi = logf.astype(BF16)
        r1 = logf - hi.astype(F32)
        mid = r1.astype(BF16)
        lo = (r1 - mid.astype(F32)).astype(BF16)
        prev = jnp.where(seq_start, 0.0, carry_ref[...])
        c = _dot(tri, hi) + _dot(tri, mid) + _dot(tri, lo) + prev
        c_ref[...] = c
        carry_ref[...] = c[tm - 1:tm, :]

    @pl.when((j <= 2) | (j == 5))
    def _():
        qkv_ref[...] = (y * jnp.where(j == 0, FOX_Q_SCALE, 1.0)).astype(BF16)

    @pl.when((j == 3) | (j == 4))
    def _():
        lane = lax.broadcasted_iota(jnp.int32, (tm, LANES), 1)
        first_half = (lane % DIFF_QK_DIM) < (ROT_DIM // 2)
        q_scale = jnp.where(j == 3, DIFF_Q_SCALE, 1.0)
        ct = ct_ref[...]
        st = st_ref[...]
        for c0 in range(0, IN_TILE, LANES):
            yc = y[:, c0:c0 + LANES]
            partner = jnp.where(first_half,
                                pltpu.roll(yc, LANES - ROT_DIM // 2, axis=1),
                                pltpu.roll(yc, ROT_DIM // 2, axis=1))
            qkv_ref[:, c0:c0 + LANES] = ((yc * ct + partner * st) * q_scale).astype(BF16)

    for t in range(CONV_CH // CONV_TILE):
        @pl.when(j == 6 + t)
        def _(t=t):
            gb = y[:, 0:CONV_TILE]
            z = y[:, CONV_TILE:2 * CONV_TILE] * y[:, 2 * CONV_TILE:3 * CONV_TILE]
            zb = zbuf_ref.at[t]

            @pl.when(seq_start)
            def _():
                zb[0:SUBLANES, :] = jnp.zeros((SUBLANES, CONV_TILE), F32)

            zb[SUBLANES:SUBLANES + tm, :] = z
            z1 = zb[SUBLANES - 1:SUBLANES - 1 + tm, :]
            z2 = zb[SUBLANES - 2:SUBLANES - 2 + tm, :]
            cw = cw_ref[...]
            conv = z2 * cw[0:1, :] + z1 * cw[1:2, :] + z * cw[2:3, :] + cb_ref[...]
            conv_ref[...] = (gb * conv).astype(BF16)
            zb[0:SUBLANES, :] = z[tm - SUBLANES:tm, :]


def _inproj(h, g, w, wff, fbias, ctab, stab, conv_w, conv_b, *, seq_len):
    n_tok, d = h.shape
    tm = TOKEN_TILE
    n_col = w.shape[1] // IN_TILE
    n_plain = n_col - CONV_CH // CONV_TILE
    conv_idx = lambda i, j: (0, jnp.clip(j - n_plain, 0, CONV_CH // CONV_TILE - 1))
    return pl.pallas_call(
        functools.partial(_inproj_body, tiles_per_seq=seq_len // tm),
        grid=(n_tok // tm, n_col),
        in_specs=[
            pl.BlockSpec((tm, d), lambda i, j: (i, 0)),
            pl.BlockSpec((1, d), lambda i, j: (0, 0)),
            pl.BlockSpec((d, IN_TILE), lambda i, j: (0, j)),
            pl.BlockSpec((d, LANES), lambda i, j: (0, 0)),
            pl.BlockSpec((1, LANES), lambda i, j: (0, 0)),
            pl.BlockSpec((tm, LANES), lambda i, j: (i, 0)),
            pl.BlockSpec((tm, LANES), lambda i, j: (i, 0)),
            pl.BlockSpec((CONV_WIDTH, CONV_TILE), conv_idx),
            pl.BlockSpec((1, CONV_TILE), conv_idx),
        ],
        out_specs=[
            pl.BlockSpec((tm, IN_TILE), lambda i, j: (i, jnp.minimum(j, n_plain - 1))),
            pl.BlockSpec((tm, CONV_TILE),
                         lambda i, j: (i, jnp.clip(j - n_plain, 0, CONV_CH // CONV_TILE - 1))),
            pl.BlockSpec((tm, LANES), lambda i, j: (i, 0)),
        ],
        out_shape=[
            jax.ShapeDtypeStruct((n_tok, n_plain * IN_TILE), BF16),
            jax.ShapeDtypeStruct((n_tok, CONV_CH), BF16),
            jax.ShapeDtypeStruct((n_tok, LANES), F32),
        ],
        scratch_shapes=[
            pltpu.VMEM((tm, d), BF16),
            pltpu.VMEM((CONV_CH // CONV_TILE, tm + SUBLANES, CONV_TILE), F32),
            pltpu.VMEM((1, LANES), F32),
        ],
        compiler_params=_params(2),
        name="inproj",
    )(h, g, w, wff, fbias, ctab, stab, conv_w, conv_b)


def _keep_mask(t):
    key = lax.broadcasted_iota(jnp.int32, (t, t), 0)
    qry = lax.broadcasted_iota(jnp.int32, (t, t), 1)
    return key <= qry


def _online_update_t(u, shift, vt, m_ref, acc_ref):
    m_old = m_ref[...]
    m_new = jnp.maximum(m_old, jnp.max(u, axis=0, keepdims=True) + shift)
    alpha = jnp.exp2(m_old - m_new)
    p = jnp.exp2(u - (m_new - shift))
    acc_ref[...] = alpha * acc_ref[...] + _dot(vt, p.astype(BF16))
    m_ref[...] = m_new


def _fox_body(q_ref, k_ref, vt_ref, c_ref, crow_ref, o_ref, ckb_ref, m_ref, acc_ref):
    h = pl.program_id(1)
    qi = pl.program_id(2)
    t = q_ref.shape[0]
    hd = FOX_HEAD_DIM

    @pl.when(qi == 0)
    def _():
        lane = lax.broadcasted_iota(jnp.int32, c_ref.shape, 1)
        ck = jnp.sum(jnp.where(lane == h, c_ref[...], 0.0), axis=-1, keepdims=True)
        ckb_ref[...] = jnp.broadcast_to(ck * LOG2E, ckb_ref.shape)

    q = q_ref[...]
    cq = crow_ref[0, qi] * LOG2E
    m_ref[...] = jnp.full_like(m_ref, NEG_INF)
    acc_ref[...] = jnp.zeros_like(acc_ref)

    def step(kb, masked):
        rows = pl.ds(pl.multiple_of(kb * t, t), t)
        u = _dot_t(k_ref[rows, :], q) - jnp.tile(ckb_ref[rows, :], (1, t // LANES))
        if masked:
            u = jnp.where(_keep_mask(t), u, NEG_INF)
        _online_update_t(u, cq, vt_ref[0, kb], m_ref, acc_ref)

    def loop_body(kb, carry):
        step(kb, False)
        return carry

    lax.fori_loop(0, qi, loop_body, 0)
    step(qi, True)
    acc = acc_ref[...]
    o_ref[...] = (acc[:hd, :] / acc[hd:hd + 1, :]).T.astype(BF16)


def _value_rows(qkv, col0, heads, *, batch, seq_len):
    hd = FOX_HEAD_DIM
    v = qkv[:, col0:col0 + heads * hd].reshape(batch, seq_len, heads, hd)
    vt = jnp.concatenate([v.transpose(0, 2, 3, 1),
                          jnp.ones((batch, heads, ONES_ROWS, seq_len), BF16)], axis=2)
    nkb = seq_len // ATT_TILE
    return vt.reshape(batch * heads, hd + ONES_ROWS, nkb, ATT_TILE).transpose(0, 2, 1, 3)


def _fox_attention(qkv, vt, c, c_rows, *, batch, seq_len):
    t = ATT_TILE
    nq = seq_len // t
    hd = FOX_HEAD_DIM
    rows = hd + ONES_ROWS
    return pl.pallas_call(
        _fox_body,
        grid=(batch, FOX_HEADS, nq),
        in_specs=[
            pl.BlockSpec((t, hd), lambda b, h, qi: (b * nq + qi, h)),
            pl.BlockSpec((seq_len, hd), lambda b, h, qi: (b, FOX_HEADS + h)),
            pl.BlockSpec((1, nq, rows, t), lambda b, h, qi: (b * FOX_HEADS + h, 0, 0, 0)),
            pl.BlockSpec((seq_len, LANES), lambda b, h, qi: (b, 0)),
            pl.BlockSpec((1, nq, 1, t), lambda b, h, qi: (b * FOX_HEADS + h, 0, 0, 0)),
        ],
        out_specs=pl.BlockSpec((t, hd), lambda b, h, qi: (b * nq + qi, h)),
        out_shape=jax.ShapeDtypeStruct((batch * seq_len, FOX_HEADS * hd), BF16),
        scratch_shapes=[pltpu.VMEM((seq_len, LANES), F32), pltpu.VMEM((1, t), F32),
                        pltpu.VMEM((rows, t), F32)],
        compiler_params=_params(3),
        name="fox_attention",
    )(qkv, qkv, vt, c, c_rows)


def _diff_body(q_ref, k_ref, vt_ref, lam_ref, sg_ref, o_ref,
               m1_ref, a1_ref, m2_ref, a2_ref, *, lam_init):
    qi = pl.program_id(2)
    t = q_ref.shape[0]
    hd = DIFF_V_DIM
    q = q_ref[...]
    lane = lax.broadcasted_iota(jnp.int32, q.shape, 1)
    zero = jnp.zeros_like(q)
    q1 = jnp.where(lane < DIFF_QK_DIM, q, zero)
    q2 = jnp.where(lane >= DIFF_QK_DIM, q, zero)
    no_shift = jnp.zeros((1, t), F32)

    for m_ref, a_ref in ((m1_ref, a1_ref), (m2_ref, a2_ref)):
        m_ref[...] = jnp.full_like(m_ref, NEG_INF)
        a_ref[...] = jnp.zeros_like(a_ref)

    def step(kb, masked):
        rows = pl.ds(pl.multiple_of(kb * t, t), t)
        k = k_ref[rows, :]
        vt = vt_ref[0, kb]
        for qc, m_ref, a_ref in ((q1, m1_ref, a1_ref), (q2, m2_ref, a2_ref)):
            u = _dot_t(k, qc)
            if masked:
                u = jnp.where(_keep_mask(t), u, NEG_INF)
            _online_update_t(u, no_shift, vt, m_ref, a_ref)

    def loop_body(kb, carry):
        step(kb, False)
        return carry

    lax.fori_loop(0, qi, loop_body, 0)
    step(qi, True)

    lv = lam_ref[...]
    lam = (jnp.exp(jnp.sum(lv[0:1, :] * lv[1:2, :], axis=-1, keepdims=True))
           - jnp.exp(jnp.sum(lv[2:3, :] * lv[3:4, :], axis=-1, keepdims=True))
           + lam_init)
    a1 = a1_ref[...]
    a2 = a2_ref[...]
    o = a1[:hd, :] / a1[hd:hd + 1, :] - lam * (a2[:hd, :] / a2[hd:hd + 1, :])
    o = o * lax.rsqrt(jnp.mean(o * o, axis=0, keepdims=True) + EPS)
    o_ref[...] = (o.T * sg_ref[...] * (1.0 - lam_init)).astype(BF16)


def _diff_attention(qkv, vt, lam_vecs, subln, *, batch, seq_len, lam_init):
    t = ATT_TILE
    nq = seq_len // t
    hd = DIFF_V_DIM
    rows = hd + ONES_ROWS
    base = 3 * FOX_HEADS
    stat = pltpu.VMEM((1, t), F32)
    acc = pltpu.VMEM((rows, t), F32)
    return pl.pallas_call(
        functools.partial(_diff_body, lam_init=lam_init),
        grid=(batch, DIFF_HEADS, nq),
        in_specs=[
            pl.BlockSpec((t, hd), lambda b, h, qi: (b * nq + qi, base + h)),
            pl.BlockSpec((seq_len, hd), lambda b, h, qi: (b, base + DIFF_HEADS + h)),
            pl.BlockSpec((1, nq, rows, t), lambda b, h, qi: (b * DIFF_HEADS + h, 0, 0, 0)),
            pl.BlockSpec((4, DIFF_QK_DIM), lambda b, h, qi: (0, 0)),
            pl.BlockSpec((1, hd), lambda b, h, qi: (0, 0)),
        ],
        out_specs=pl.BlockSpec((t, hd), lambda b, h, qi: (b * nq + qi, h)),
        out_shape=jax.ShapeDtypeStruct((batch * seq_len, DIFF_HEADS * hd), BF16),
        scratch_shapes=[stat, acc, stat, acc],
        compiler_params=_params(3),
        name="diff_attention",
    )(qkv, qkv, vt, lam_vecs, subln)


def _mixout_body(h_ref, fox_ref, diff_ref, conv_ref, wf_ref, wd_ref, wc_ref, o_ref):
    o_ref[...] = (h_ref[...] + _dot(fox_ref[...], wf_ref[...])
                  + _dot(diff_ref[...], wd_ref[...]) + _dot(conv_ref[...], wc_ref[...]))


def _mixout(h, fox, diff, conv, w_fox, w_diff, w_conv):
    n_tok, d = h.shape
    tm = TOKEN_TILE
    row = lambda i: (i, 0)
    const = lambda i: (0, 0)
    resident = functools.partial(pl.BlockSpec, index_map=const, pipeline_mode=pl.Buffered(1))
    return pl.pallas_call(
        _mixout_body,
        grid=(n_tok // tm,),
        in_specs=[
            pl.BlockSpec((tm, d), row),
            pl.BlockSpec((tm, fox.shape[1]), row),
            pl.BlockSpec((tm, diff.shape[1]), row),
            pl.BlockSpec((tm, conv.shape[1]), row),
            resident(w_fox.shape),
            resident(w_diff.shape),
            resident(w_conv.shape),
        ],
        out_specs=pl.BlockSpec((tm, d), row),
        out_shape=jax.ShapeDtypeStruct((n_tok, d), F32),
        compiler_params=_params(1),
        name="mixout",
    )(h, fox, diff, conv, w_fox, w_diff, w_conv)


def _memkv_body(m_ref, g_ref, w_ref, o_ref):
    o_ref[...] = _dot(_rms(m_ref[...], g_ref[...]).astype(BF16), w_ref[...]).astype(BF16)


def _memkv(mem, g, w):
    n_mem, d = mem.shape
    tm = min(n_mem, TOKEN_TILE)
    return pl.pallas_call(
        _memkv_body,
        grid=(n_mem // tm,),
        in_specs=[
            pl.BlockSpec((tm, d), lambda i: (i, 0)),
            pl.BlockSpec((1, d), lambda i: (0, 0)),
            pl.BlockSpec(w.shape, lambda i: (0, 0), pipeline_mode=pl.Buffered(1)),
        ],
        out_specs=pl.BlockSpec((tm, w.shape[1]), lambda i: (i, 0)),
        out_shape=jax.ShapeDtypeStruct((n_mem, w.shape[1]), BF16),
        compiler_params=_params(1),
        name="memkv",
    )(mem, g, w)


def _cross_body(h_ref, g_ref, wq_ref, kv_ref, wo_ref, o_ref):
    hd = CROSS_HEAD_DIM
    width = CROSS_HEADS * hd
    scale = hd ** -0.5
    x = h_ref[...]
    q = _dot(_rms(x, g_ref[...]).astype(BF16), wq_ref[...]).astype(BF16)
    heads = []
    for hh in range(CROSS_HEADS):
        k = kv_ref[:, hh * hd:(hh + 1) * hd]
        v = kv_ref[:, width + hh * hd:width + (hh + 1) * hd]
        s = _dot_t(q[:, hh * hd:(hh + 1) * hd], k) * scale
        e = jnp.exp(s - jnp.max(s, axis=-1, keepdims=True))
        p = e / jnp.sum(e, axis=-1, keepdims=True)
        heads.append(_dot(p.astype(BF16), v))
    o = jnp.concatenate(heads, axis=-1).astype(BF16)
    o_ref[...] = x + _dot(o, wo_ref[...])


def _cross(h, g, wq, kv, wo, *, seq_len, mem_len):
    n_tok, d = h.shape
    tm = TOKEN_TILE
    tiles_per_seq = seq_len // tm
    const = lambda i: (0, 0)
    return pl.pallas_call(
        _cross_body,
        grid=(n_tok // tm,),
        in_specs=[
            pl.BlockSpec((tm, d), lambda i: (i, 0)),
            pl.BlockSpec((1, d), const),
            pl.BlockSpec(wq.shape, const, pipeline_mode=pl.Buffered(1)),
            pl.BlockSpec((mem_len, kv.shape[1]), lambda i: (i // tiles_per_seq, 0)),
            pl.BlockSpec(wo.shape, const, pipeline_mode=pl.Buffered(1)),
        ],
        out_specs=pl.BlockSpec((tm, d), lambda i: (i, 0)),
        out_shape=jax.ShapeDtypeStruct((n_tok, d), F32),
        compiler_params=_params(1),
        name="cross_attention",
    )(h, g, wq, kv, wo)


def _rope_tables(positions):
    half = ROT_DIM // 2
    inv_freq = ROPE_THETA ** (-jnp.arange(0, ROT_DIM, 2, dtype=F32) / ROT_DIM)
    ang = positions.astype(F32)[..., None] * inv_freq
    cos = jnp.cos(ang).reshape(-1, half)
    sin = jnp.sin(ang).reshape(-1, half)
    n = cos.shape[0]
    rest = DIFF_QK_DIM - ROT_DIM
    ct = jnp.concatenate([cos, cos, jnp.ones((n, rest), F32)], axis=-1)
    st = jnp.concatenate([-sin, sin, jnp.zeros((n, rest), F32)], axis=-1)
    reps = LANES // DIFF_QK_DIM
    return jnp.tile(ct, (1, reps)), jnp.tile(st, (1, reps))


def _arrange_in_weights(w_in):
    fw, dw, cc = FOX_HEADS * FOX_HEAD_DIM, DIFF_HEADS * DIFF_V_DIM, CONV_CH
    o = 0
    fq, fk, fv = (w_in[..., o + k * fw:o + (k + 1) * fw] for k in range(3))
    o += 3 * fw
    ff = w_in[..., o:o + FOX_HEADS]
    o += FOX_HEADS
    dq, dk, dv = (w_in[..., o + k * dw:o + (k + 1) * dw] for k in range(3))
    o += 3 * dw
    gb, gc, hc = (w_in[..., o + k * cc:o + (k + 1) * cc] for k in range(3))
    parts = [fq, fk, fv, dq, dk, dv]
    for t in range(cc // CONV_TILE):
        sl = slice(t * CONV_TILE, (t + 1) * CONV_TILE)
        parts += [gb[..., sl], gc[..., sl], hc[..., sl]]
    w = jnp.concatenate(parts, axis=-1).astype(BF16)
    wff = jnp.pad(ff, ((0, 0), (0, 0), (0, LANES - FOX_HEADS))).astype(BF16)
    return w, wff


def kernel(x, mem, positions, ffn1_norm, ffn1_w_gate, ffn1_w_up, ffn1_w_down, mix_norm, mix_w_in, forget_bias, conv_w, conv_b, lambda_q1, lambda_k1, lambda_q2, lambda_k2, diff_subln, mix_w_out, cross_norm, mem_norm, cross_w_q, cross_w_kv, cross_w_o, ffn2_norm, ffn2_w_gate, ffn2_w_up, ffn2_w_down, final_norm):
    batch, seq_len, d = x.shape
    mem_len = mem.shape[1]
    depth = ffn1_norm.shape[0]
    n_tok = batch * seq_len
    assert seq_len % TOKEN_TILE == 0 and seq_len % ATT_TILE == 0
    assert ffn1_w_gate.shape[2] % FF_TILE == 0

    bf = lambda a: a.astype(BF16)
    w1g, w1u, w1d = bf(ffn1_w_gate), bf(ffn1_w_up), bf(ffn1_w_down)
    w2g, w2u, w2d = bf(ffn2_w_gate), bf(ffn2_w_up), bf(ffn2_w_down)
    w_in, w_ff = _arrange_in_weights(mix_w_in)
    w_out = bf(mix_w_out)
    fw, dw = FOX_HEADS * FOX_HEAD_DIM, DIFF_HEADS * DIFF_V_DIM
    wq, wkv, wo = bf(cross_w_q), bf(cross_w_kv), bf(cross_w_o)
    fbias = jnp.pad(forget_bias.astype(F32), ((0, 0), (0, LANES - FOX_HEADS)))
    lam_vecs = jnp.stack([lambda_q1, lambda_k1, lambda_q2, lambda_k2], axis=1).astype(F32)
    ctab, stab = _rope_tables(positions)
    row = lambda a, l: a[l][None, :]

    h = x.reshape(n_tok, d)
    mem2 = mem.reshape(batch * mem_len, d)
    nq = seq_len // ATT_TILE
    for l in range(depth):
        h = _ffn(h, row(ffn1_norm, l), w1g[l], w1u[l], w1d[l], row(ffn1_norm, l), final=False)

        qkv, conv, c = _inproj(h, row(mix_norm, l), w_in[l], w_ff[l], fbias[l][None, :], ctab, stab,
                               conv_w[l], row(conv_b, l), seq_len=seq_len)
        c_rows = (c.reshape(batch, seq_len, LANES)[:, :, :FOX_HEADS]
                  .transpose(0, 2, 1).reshape(batch * FOX_HEADS, nq, 1, ATT_TILE))
        fox_vt = _value_rows(qkv, 2 * fw, FOX_HEADS, batch=batch, seq_len=seq_len)
        fox = _fox_attention(qkv, fox_vt, c, c_rows, batch=batch, seq_len=seq_len)
        lam_init = 0.8 - 0.6 * math.exp(-0.3 * l)
        diff_vt = _value_rows(qkv, 3 * fw + 2 * dw, DIFF_HEADS, batch=batch, seq_len=seq_len)
        diff = _diff_attention(qkv, diff_vt, lam_vecs[l], row(diff_subln, l), batch=batch,
                               seq_len=seq_len, lam_init=lam_init)
        h = _mixout(h, fox, diff, conv, w_out[l, :fw], w_out[l, fw:fw + dw], w_out[l, fw + dw:])

        kv = _memkv(mem2, row(mem_norm, l), wkv[l])
        h = _cross(h, row(cross_norm, l), wq[l], kv, wo[l], seq_len=seq_len, mem_len=mem_len)

        last = l == depth - 1
        h = _ffn(h, row(ffn2_norm, l), w2g[l], w2u[l], w2d[l], final_norm[None, :], final=last)
    return h.reshape(batch, seq_len, d)
```

```python
import functools
import math

import numpy as np
import jax
import jax.numpy as jnp
from jax import lax
from jax.experimental import pallas as pl
from jax.experimental.pallas import tpu as pltpu

F32 = jnp.float32
BF16 = jnp.bfloat16

FOX_HEADS = 6
FOX_HEAD_DIM = 128
DIFF_HEADS = 6
DIFF_QK_DIM = 64
DIFF_V_DIM = 128
CONV_CH = 512
CONV_WIDTH = 3
ROT_DIM = 16
ROPE_THETA = 500000.0
CROSS_HEADS = 4
CROSS_HEAD_DIM = 128
EPS = 1e-6
NEG_INF = -1e30
LOG2E = math.log2(math.e)
FOX_Q_SCALE = FOX_HEAD_DIM ** -0.5 * LOG2E
DIFF_Q_SCALE = DIFF_QK_DIM ** -0.5 * LOG2E

LANES = 128
SUBLANES = 8
BF16_ROWS = 16

TOKEN_TILE = 512
FF_TILE = 512
IN_TILE = 768
CONV_TILE = 256
ATT_TILE = 512
ONES_ROWS = BF16_ROWS
VMEM_LIMIT = 56 * 1024 * 1024

_ARB = "arbitrary"


def _params(n_axes):
    return pltpu.CompilerParams(dimension_semantics=(_ARB,) * n_axes,
                                vmem_limit_bytes=VMEM_LIMIT)


def _rms(x, g):
    return x * lax.rsqrt(jnp.mean(x * x, axis=-1, keepdims=True) + EPS) * g


def _dot(a, b):
    return jnp.dot(a, b, preferred_element_type=F32)


def _dot_t(a, b):
    return lax.dot_general(a, b, (((1,), (1,)), ((), ())), preferred_element_type=F32)


def _layer_spec(shape, index_map):
    return pl.BlockSpec((None,) + tuple(shape), index_map)


def _ffn_body(x_ref, g_ref, wg_ref, wu_ref, wd_ref, fg_ref, o_ref, n_ref, acc_ref, *, final):
    j = pl.program_id(1)

    @pl.when(j == 0)
    def _():
        n_ref[...] = _rms(x_ref[...], g_ref[...]).astype(BF16)
        acc_ref[...] = jnp.zeros_like(acc_ref)

    n = n_ref[...]
    gate = _dot(n, wg_ref[...])
    up = _dot(n, wu_ref[...])
    act = (gate * jax.nn.sigmoid(gate) * up).astype(BF16)
    acc_ref[...] += _dot(act, wd_ref[...])

    @pl.when(j == pl.num_programs(1) - 1)
    def _():
        y = x_ref[...] + 0.5 * acc_ref[...]
        if final:
            y = _rms(y, fg_ref[...])
        o_ref[...] = y


def _ffn(h, g, wg, wu, wd, final_g, l, *, final):
    n_tok, d = h.shape
    d_ff = wg.shape[2]
    tm, tf = TOKEN_TILE, FF_TILE
    return pl.pallas_call(
        functools.partial(_ffn_body, final=final),
        grid=(n_tok // tm, d_ff // tf),
        in_specs=[
            pl.BlockSpec((tm, d), lambda i, j: (i, 0)),
            _layer_spec((1, d), lambda i, j: (l, 0, 0)),
            _layer_spec((d, tf), lambda i, j: (l, 0, j)),
            _layer_spec((d, tf), lambda i, j: (l, 0, j)),
            _layer_spec((tf, d), lambda i, j: (l, j, 0)),
            _layer_spec((1, d), lambda i, j: (0, 0, 0)),
        ],
        out_specs=pl.BlockSpec((tm, d), lambda i, j: (i, 0)),
        out_shape=jax.ShapeDtypeStruct((n_tok, d), F32),
        scratch_shapes=[pltpu.VMEM((tm, d), BF16), pltpu.VMEM((tm, d), F32)],
        compiler_params=_params(2),
        name="ffn_final" if final else "ffn",
    )(h, g, wg, wu, wd, final_g)


_J_FOX_Q, _J_FOX_K, _J_FOX_V, _J_DIFF_Q, _J_DIFF_K, _J_DIFF_V, _J_CONV = 0, 1, 2, 3, 4, 5, 6
_N_CONV_TILES = CONV_CH // CONV_TILE
_N_IN_TILES = _J_CONV + _N_CONV_TILES


def _inproj_body(x_ref, g_ref, w_ref, wff_ref, fb_ref, ct_ref, st_ref, cw_ref, cb_ref,
                 qk_ref, conv_ref, c_ref, fvt_ref, dvt_ref, n_ref, zbuf_ref, carry_ref,
                 *, tiles_per_seq):
    i = pl.program_id(0)
    j = pl.program_id(1)
    tm = x_ref.shape[0]
    seq_start = (i % tiles_per_seq) == 0

    @pl.when(j == 0)
    def _():
        n_ref[...] = _rms(x_ref[...], g_ref[...]).astype(BF16)

    n = n_ref[...]
    y = _dot(n, w_ref[...])

    @pl.when(j == 0)
    def _():
        logf = jax.nn.log_sigmoid(_dot(n, wff_ref[...]) + fb_ref[...])
        row = lax.broadcasted_iota(jnp.int32, (tm, tm), 0)
        col = lax.broadcasted_iota(jnp.int32, (tm, tm), 1)
        tri = jnp.where(row >= col, 1.0, 0.0).astype(BF16)
        hi = logf.astype(BF16)
        r1 = logf - hi.astype(F32)
        mid = r1.astype(BF16)
        lo = (r1 - mid.astype(F32)).astype(BF16)
        prev = jnp.where(seq_start, 0.0, carry_ref[...])
        c = _dot(tri, hi) + _dot(tri, mid) + _dot(tri, lo) + prev
        c_ref[...] = c
        carry_ref[...] = c[tm - 1:tm, :]

    @pl.when(j <= _J_FOX_K)
    def _():
        qk_ref[...] = (y * jnp.where(j == _J_FOX_Q, FOX_Q_SCALE, 1.0)).astype(BF16)

    @pl.when(j == _J_FOX_V)
    def _():
        fvt_ref[...] = y.T.astype(BF16)

    @pl.when(j == _J_DIFF_V)
    def _():
        dvt_ref[...] = y.T.astype(BF16)

    @pl.when((j == _J_DIFF_Q) | (j == _J_DIFF_K))
    def _():
        lane = lax.broadcasted_iota(jnp.int32, (tm, LANES), 1)
        first_half = (lane % DIFF_QK_DIM) < (ROT_DIM // 2)
        q_scale = jnp.where(j == _J_DIFF_Q, DIFF_Q_SCALE, 1.0)
        ct = ct_ref[...]
        st = st_ref[...]
        for c0 in range(0, IN_TILE, LANES):
            yc = y[:, c0:c0 + LANES]
            partner = jnp.where(first_half,
                                pltpu.roll(yc, LANES - ROT_DIM // 2, axis=1),
                                pltpu.roll(yc, ROT_DIM // 2, axis=1))
            qk_ref[:, c0:c0 + LANES] = ((yc * ct + partner * st) * q_scale).astype(BF16)

    for t in range(_N_CONV_TILES):
        @pl.when(j == _J_CONV + t)
        def _(t=t):
            gb = y[:, 0:CONV_TILE]
            z = y[:, CONV_TILE:2 * CONV_TILE] * y[:, 2 * CONV_TILE:3 * CONV_TILE]
            zb = zbuf_ref.at[t]

            @pl.when(seq_start)
            def _():
                zb[0:SUBLANES, :] = jnp.zeros((SUBLANES, CONV_TILE), F32)

            zb[SUBLANES:SUBLANES + tm, :] = z
            z1 = zb[SUBLANES - 1:SUBLANES - 1 + tm, :]
            z2 = zb[SUBLANES - 2:SUBLANES - 2 + tm, :]
            cw = cw_ref[...]
            conv = z2 * cw[0:1, :] + z1 * cw[1:2, :] + z * cw[2:3, :] + cb_ref[...]
            conv_ref[...] = (gb * conv).astype(BF16)
            zb[0:SUBLANES, :] = z[tm - SUBLANES:tm, :]


def _inproj(h, g, w, wff, fbias, ctab, stab, conv_w, conv_b, l, *, seq_len):
    n_tok, d = h.shape
    tm = TOKEN_TILE
    conv_col = lambda j: jnp.clip(j - _J_CONV, 0, _N_CONV_TILES - 1)
    qk_col = lambda j: jnp.minimum(j - (j >= _J_FOX_V).astype(jnp.int32)
                                   - (j >= _J_DIFF_V).astype(jnp.int32), 3)
    return pl.pallas_call(
        functools.partial(_inproj_body, tiles_per_seq=seq_len // tm),
        grid=(n_tok // tm, _N_IN_TILES),
        in_specs=[
            pl.BlockSpec((tm, d), lambda i, j: (i, 0)),
            _layer_spec((1, d), lambda i, j: (l, 0, 0)),
            _layer_spec((d, IN_TILE), lambda i, j: (l, 0, j)),
            _layer_spec((d, LANES), lambda i, j: (l, 0, 0)),
            _layer_spec((1, LANES), lambda i, j: (l, 0, 0)),
            pl.BlockSpec((tm, LANES), lambda i, j: (i, 0)),
            pl.BlockSpec((tm, LANES), lambda i, j: (i, 0)),
            _layer_spec((CONV_WIDTH, CONV_TILE), lambda i, j: (l, 0, conv_col(j))),
            _layer_spec((1, CONV_TILE), lambda i, j: (l, 0, conv_col(j))),
        ],
        out_specs=[
            pl.BlockSpec((tm, IN_TILE), lambda i, j: (i, qk_col(j))),
            pl.BlockSpec((tm, CONV_TILE), lambda i, j: (i, conv_col(j))),
            pl.BlockSpec((tm, LANES), lambda i, j: (i, 0)),
            pl.BlockSpec((IN_TILE, tm), lambda i, j: (0, i)),
            pl.BlockSpec((IN_TILE, tm), lambda i, j: (0, i)),
        ],
        out_shape=[
            jax.ShapeDtypeStruct((n_tok, 4 * IN_TILE), BF16),
            jax.ShapeDtypeStruct((n_tok, CONV_CH), BF16),
            jax.ShapeDtypeStruct((n_tok, LANES), F32),
            jax.ShapeDtypeStruct((IN_TILE, n_tok), BF16),
            jax.ShapeDtypeStruct((IN_TILE, n_tok), BF16),
        ],
        scratch_shapes=[
            pltpu.VMEM((tm, d), BF16),
            pltpu.VMEM((_N_CONV_TILES, tm + SUBLANES, CONV_TILE), F32),
            pltpu.VMEM((1, LANES), F32),
        ],
        compiler_params=_params(2),
        name="inproj",
    )(h, g, w, wff, fbias, ctab, stab, conv_w, conv_b)


_T_QI, _T_KB, _T_FIRST, _T_DIAG, _T_ACC = 0, 1, 2, 3, 4


def _item_table(nq):
    items = [(qi, kb) for qi in range(nq) for kb in range(qi + 1)]
    n_pairs = (len(items) + 2) // 2
    n_pos = 2 * n_pairs + 2
    tab = np.zeros((5, n_pos), np.int32)
    for pos in range(n_pos):
        tau = pos - 1
        qi, kb = items[min(max(tau, 0), len(items) - 1)]
        real = 0 <= tau < len(items)
        tab[:, pos] = (qi, kb, int(real and kb == 0), int(real and kb == qi), qi if real else nq)
    return tab, n_pairs


def _fill_bias(bias_ref, t):
    key = lax.broadcasted_iota(jnp.int32, (t, t), 0)
    qry = lax.broadcasted_iota(jnp.int32, (t, t), 1)
    bias_ref[0] = jnp.zeros((t, t), F32)
    bias_ref[1] = jnp.where(key <= qry, 0.0, NEG_INF)


def _stage_values(vt_ref, vte_ref, t):
    hd = vt_ref.shape[0]
    for kb in range(vte_ref.shape[0]):
        vte_ref[kb, 0:hd, :] = vt_ref[:, kb * t:(kb + 1) * t]
        vte_ref[kb, hd:, :] = jnp.ones((ONES_ROWS, t), BF16)


def _store_scores(u, u_ref, mx_ref):
    u_ref[...] = u
    mx_ref[...] = jnp.max(u, axis=0, keepdims=True)


def _softmax_stage(u_ref, mx_ref, shift, first, m_ref, p_ref, al_ref):
    m_old = jnp.where(first, NEG_INF, m_ref[...])
    m_new = jnp.maximum(m_old, mx_ref[...] + shift)
    al_ref[...] = jnp.exp2(m_old - m_new)
    p_ref[...] = jnp.exp2(u_ref[...] - (m_new - shift)).astype(BF16)
    m_ref[...] = m_new


def _value_stage(vt, p_ref, al_ref, acc_ref):
    acc_ref[...] = al_ref[...] * acc_ref[...] + _dot(vt, p_ref[...])


def _fox_body(tab_ref, q_ref, k_ref, vt_ref, c_ref, crow_ref, o_ref,
              ckb_ref, vte_ref, bias_ref, u_ref, mx_ref, p_ref, al_ref, m_ref, acc_ref, *, n_pairs):
    b = pl.program_id(0)
    h = pl.program_id(1)
    t = ATT_TILE
    hd = FOX_HEAD_DIM

    @pl.when((b == 0) & (h == 0))
    def _():
        _fill_bias(bias_ref, t)

    lane = lax.broadcasted_iota(jnp.int32, c_ref.shape, 1)
    ck = jnp.sum(jnp.where(lane == h, c_ref[...], 0.0), axis=-1, keepdims=True)
    ckb_ref[...] = jnp.broadcast_to(ck * LOG2E, ckb_ref.shape)
    _stage_values(vt_ref, vte_ref, t)
    p_ref[1] = jnp.zeros((t, t), BF16)
    al_ref[1] = jnp.zeros((1, t), F32)
    acc_ref[...] = jnp.zeros_like(acc_ref)
    m_ref[...] = jnp.full_like(m_ref, NEG_INF)

    def rows(idx):
        return pl.ds(pl.multiple_of(idx * t, t), t)

    def score_stage(pos, slot):
        kb = tab_ref[_T_KB, pos]
        u = (_dot_t(k_ref[rows(kb), :], q_ref[rows(tab_ref[_T_QI, pos]), :])
             - jnp.tile(ckb_ref[rows(kb), :], (1, t // LANES)) + bias_ref[tab_ref[_T_DIAG, pos]])
        _store_scores(u, u_ref.at[slot], mx_ref.at[slot])

    def step(tau, slot):
        other = 1 - slot
        score_stage(tau + 2, other)
        pos = tau + 1
        cq = crow_ref[0, tab_ref[_T_QI, pos]] * LOG2E
        _softmax_stage(u_ref.at[slot], mx_ref.at[slot], cq, tab_ref[_T_FIRST, pos] == 1,
                       m_ref, p_ref.at[slot], al_ref.at[slot])
        _value_stage(vte_ref[tab_ref[_T_KB, tau]], p_ref.at[other], al_ref.at[other],
                     acc_ref.at[tab_ref[_T_ACC, tau]])

    score_stage(1, 0)

    def pair(it, carry):
        step(2 * it, 0)
        step(2 * it + 1, 1)
        return carry

    lax.fori_loop(0, n_pairs, pair, 0)

    for qi in range(acc_ref.shape[0] - 1):
        acc = acc_ref[qi]
        o_ref[qi * t:(qi + 1) * t, :] = (acc[:hd, :] / acc[hd:hd + 1, :]).T.astype(BF16)


def _attention_scratch(t, nq, rows, n_streams):
    shape = lambda *s: ((n_streams,) if n_streams > 1 else ()) + s
    return [
        pltpu.VMEM((2,) + shape(t, t), F32),
        pltpu.VMEM((2,) + shape(1, t), F32),
        pltpu.VMEM((2,) + shape(t, t), BF16),
        pltpu.VMEM((2,) + shape(1, t), F32),
        pltpu.VMEM(shape(1, t), F32),
        pltpu.VMEM(shape(nq + 1, rows, t), F32),
    ]


def _fox_attention(qk, vt, c, c_rows, *, batch, seq_len):
    t = ATT_TILE
    nq = seq_len // t
    hd = FOX_HEAD_DIM
    rows = hd + ONES_ROWS
    tab, n_pairs = _item_table(nq)
    grid_spec = pltpu.PrefetchScalarGridSpec(
        num_scalar_prefetch=1,
        grid=(batch, FOX_HEADS),
        in_specs=[
            pl.BlockSpec((seq_len, hd), lambda b, h, tab: (b, h)),
            pl.BlockSpec((seq_len, hd), lambda b, h, tab: (b, FOX_HEADS + h)),
            pl.BlockSpec((hd, seq_len), lambda b, h, tab: (h, b)),
            pl.BlockSpec((seq_len, LANES), lambda b, h, tab: (b, 0)),
            pl.BlockSpec((1, nq, 1, t), lambda b, h, tab: (b * FOX_HEADS + h, 0, 0, 0)),
        ],
        out_specs=pl.BlockSpec((seq_len, hd), lambda b, h, tab: (b, h)),
        scratch_shapes=[pltpu.VMEM((seq_len, LANES), F32),
                        pltpu.VMEM((nq, rows, t), BF16),
                        pltpu.VMEM((2, t, t), F32)] + _attention_scratch(t, nq, rows, 1),
    )
    return pl.pallas_call(
        functools.partial(_fox_body, n_pairs=n_pairs),
        grid_spec=grid_spec,
        out_shape=jax.ShapeDtypeStruct((batch * seq_len, FOX_HEADS * hd), BF16),
        compiler_params=_params(2),
        name="fox_attention",
    )(jnp.asarray(tab), qk, qk, vt, c, c_rows)


def _diff_body(tab_ref, q_ref, k_ref, vt_ref, lam_ref, sg_ref, o_ref,
               vte_ref, bias_ref, u_ref, mx_ref, p_ref, al_ref, m_ref, acc_ref, *, n_pairs, lam_init):
    b = pl.program_id(0)
    h = pl.program_id(1)
    t = ATT_TILE
    hd = DIFF_V_DIM

    @pl.when((b == 0) & (h == 0))
    def _():
        _fill_bias(bias_ref, t)

    _stage_values(vt_ref, vte_ref, t)
    p_ref[1] = jnp.zeros((2, t, t), BF16)
    al_ref[1] = jnp.zeros((2, 1, t), F32)
    acc_ref[...] = jnp.zeros_like(acc_ref)
    m_ref[...] = jnp.full_like(m_ref, NEG_INF)
    no_shift = jnp.zeros((1, t), F32)

    def rows(idx):
        return pl.ds(pl.multiple_of(idx * t, t), t)

    def score_stage(pos, slot):
        q = q_ref[rows(tab_ref[_T_QI, pos]), :]
        k = k_ref[rows(tab_ref[_T_KB, pos]), :]
        bias = bias_ref[tab_ref[_T_DIAG, pos]]
        lane = lax.broadcasted_iota(jnp.int32, q.shape, 1)
        zero = jnp.zeros_like(q)
        for s, keep in enumerate((lane < DIFF_QK_DIM, lane >= DIFF_QK_DIM)):
            _store_scores(_dot_t(k, jnp.where(keep, q, zero)) + bias,
                          u_ref.at[slot, s], mx_ref.at[slot, s])

    def step(tau, slot):
        other = 1 - slot
        score_stage(tau + 2, other)
        pos = tau + 1
        first = tab_ref[_T_FIRST, pos] == 1
        vt = vte_ref[tab_ref[_T_KB, tau]]
        for s in range(2):
            _softmax_stage(u_ref.at[slot, s], mx_ref.at[slot, s], no_shift, first,
                           m_ref.at[s], p_ref.at[slot, s], al_ref.at[slot, s])
            _value_stage(vt, p_ref.at[other, s], al_ref.at[other, s],
                         acc_ref.at[s, tab_ref[_T_ACC, tau]])

    score_stage(1, 0)

    def pair(it, carry):
        step(2 * it, 0)
        step(2 * it + 1, 1)
        return carry

    lax.fori_loop(0, n_pairs, pair, 0)

    lv = lam_ref[...]
    lam = (jnp.exp(jnp.sum(lv[0:1, :] * lv[1:2, :], axis=-1, keepdims=True))
           - jnp.exp(jnp.sum(lv[2:3, :] * lv[3:4, :], axis=-1, keepdims=True))
           + lam_init)
    for qi in range(acc_ref.shape[1] - 1):
        a1 = acc_ref[0, qi]
        a2 = acc_ref[1, qi]
        o = a1[:hd, :] / a1[hd:hd + 1, :] - lam * (a2[:hd, :] / a2[hd:hd + 1, :])
        o = o * lax.rsqrt(jnp.mean(o * o, axis=0, keepdims=True) + EPS)
        o_ref[qi * t:(qi + 1) * t, :] = (o.T * sg_ref[...] * (1.0 - lam_init)).astype(BF16)


def _diff_attention(qk, vt, lam_vecs, subln, l, *, batch, seq_len, lam_init):
    t = ATT_TILE
    nq = seq_len // t
    hd = DIFF_V_DIM
    rows = hd + ONES_ROWS
    base = 2 * FOX_HEADS
    tab, n_pairs = _item_table(nq)
    grid_spec = pltpu.PrefetchScalarGridSpec(
        num_scalar_prefetch=1,
        grid=(batch, DIFF_HEADS),
        in_specs=[
            pl.BlockSpec((seq_len, hd), lambda b, h, tab: (b, base + h)),
            pl.BlockSpec((seq_len, hd), lambda b, h, tab: (b, base + DIFF_HEADS + h)),
            pl.BlockSpec((hd, seq_len), lambda b, h, tab: (h, b)),
            _layer_spec((4, DIFF_QK_DIM), lambda b, h, tab: (l, 0, 0)),
            _layer_spec((1, hd), lambda b, h, tab: (l, 0, 0)),
        ],
        out_specs=pl.BlockSpec((seq_len, hd), lambda b, h, tab: (b, h)),
        scratch_shapes=[pltpu.VMEM((nq, rows, t), BF16),
                        pltpu.VMEM((2, t, t), F32)] + _attention_scratch(t, nq, rows, 2),
    )
    return pl.pallas_call(
        functools.partial(_diff_body, n_pairs=n_pairs, lam_init=lam_init),
        grid_spec=grid_spec,
        out_shape=jax.ShapeDtypeStruct((batch * seq_len, DIFF_HEADS * hd), BF16),
        compiler_params=_params(2),
        name="diff_attention",
    )(jnp.asarray(tab), qk, qk, vt, lam_vecs, subln)


def _mixout_body(h_ref, fox_ref, diff_ref, conv_ref, wf_ref, wd_ref, wc_ref, o_ref):
    o_ref[...] = (h_ref[...] + _dot(fox_ref[...], wf_ref[...])
                  + _dot(diff_ref[...], wd_ref[...]) + _dot(conv_ref[...], wc_ref[...]))


def _mixout(h, fox, diff, conv, w_out, l):
    n_tok, d = h.shape
    tm = TOKEN_TILE
    fw, dw, cw = fox.shape[1], diff.shape[1], conv.shape[1]
    assert fw == dw and (fw + dw) % cw == 0
    row = lambda i: (i, 0)
    resident = lambda rows, blk: pl.BlockSpec((None, rows, d), lambda i: (l, blk, 0),
                                              pipeline_mode=pl.Buffered(1))
    return pl.pallas_call(
        _mixout_body,
        grid=(n_tok // tm,),
        in_specs=[
            pl.BlockSpec((tm, d), row),
            pl.BlockSpec((tm, fw), row),
            pl.BlockSpec((tm, dw), row),
            pl.BlockSpec((tm, cw), row),
            resident(fw, 0),
            resident(dw, 1),
            resident(cw, (fw + dw) // cw),
        ],
        out_specs=pl.BlockSpec((tm, d), row),
        out_shape=jax.ShapeDtypeStruct((n_tok, d), F32),
        compiler_params=_params(1),
        name="mixout",
    )(h, fox, diff, conv, w_out, w_out, w_out)


def _memkv_body(m_ref, g_ref, w_ref, o_ref):
    o_ref[...] = _dot(_rms(m_ref[...], g_ref[...]).astype(BF16), w_ref[...]).astype(BF16)


def _memkv(mem, g, w, l):
    n_mem, d = mem.shape
    tm = min(n_mem, TOKEN_TILE)
    width = w.shape[2]
    return pl.pallas_call(
        _memkv_body,
        grid=(n_mem // tm,),
        in_specs=[
            pl.BlockSpec((tm, d), lambda i: (i, 0)),
            _layer_spec((1, d), lambda i: (l, 0, 0)),
            pl.BlockSpec((None, d, width), lambda i: (l, 0, 0), pipeline_mode=pl.Buffered(1)),
        ],
        out_specs=pl.BlockSpec((tm, width), lambda i: (i, 0)),
        out_shape=jax.ShapeDtypeStruct((n_mem, width), BF16),
        compiler_params=_params(1),
        name="memkv",
    )(mem, g, w)


def _cross_body(h_ref, g_ref, wq_ref, kv_ref, wo_ref, o_ref):
    hd = CROSS_HEAD_DIM
    width = CROSS_HEADS * hd
    scale = hd ** -0.5
    x = h_ref[...]
    q = _dot(_rms(x, g_ref[...]).astype(BF16), wq_ref[...]).astype(BF16)
    heads = []
    for hh in range(CROSS_HEADS):
        k = kv_ref[:, hh * hd:(hh + 1) * hd]
        v = kv_ref[:, width + hh * hd:width + (hh + 1) * hd]
        s = _dot_t(q[:, hh * hd:(hh + 1) * hd], k) * scale
        e = jnp.exp(s - jnp.max(s, axis=-1, keepdims=True))
        p = e / jnp.sum(e, axis=-1, keepdims=True)
        heads.append(_dot(p.astype(BF16), v))
    o = jnp.concatenate(heads, axis=-1).astype(BF16)
    o_ref[...] = x + _dot(o, wo_ref[...])


def _cross(h, g, wq, kv, wo, l, *, seq_len, mem_len):
    n_tok, d = h.shape
    tm = TOKEN_TILE
    tiles_per_seq = seq_len // tm
    resident = lambda a: pl.BlockSpec((None,) + a.shape[1:], lambda i: (l, 0, 0),
                                      pipeline_mode=pl.Buffered(1))
    return pl.pallas_call(
        _cross_body,
        grid=(n_tok // tm,),
        in_specs=[
            pl.BlockSpec((tm, d), lambda i: (i, 0)),
            _layer_spec((1, d), lambda i: (l, 0, 0)),
            resident(wq),
            pl.BlockSpec((mem_len, kv.shape[1]), lambda i: (i // tiles_per_seq, 0)),
            resident(wo),
        ],
        out_specs=pl.BlockSpec((tm, d), lambda i: (i, 0)),
        out_shape=jax.ShapeDtypeStruct((n_tok, d), F32),
        compiler_params=_params(1),
        name="cross_attention",
    )(h, g, wq, kv, wo)


def _rope_tables(positions):
    half = ROT_DIM // 2
    inv_freq = ROPE_THETA ** (-jnp.arange(0, ROT_DIM, 2, dtype=F32) / ROT_DIM)
    ang = positions.astype(F32)[..., None] * inv_freq
    cos = jnp.cos(ang).reshape(-1, half)
    sin = jnp.sin(ang).reshape(-1, half)
    n = cos.shape[0]
    rest = DIFF_QK_DIM - ROT_DIM
    ct = jnp.concatenate([cos, cos, jnp.ones((n, rest), F32)], axis=-1)
    st = jnp.concatenate([-sin, sin, jnp.zeros((n, rest), F32)], axis=-1)
    reps = LANES // DIFF_QK_DIM
    return jnp.tile(ct, (1, reps)), jnp.tile(st, (1, reps))


def _arrange_in_weights(w_in):
    fw, dw, cc = FOX_HEADS * FOX_HEAD_DIM, DIFF_HEADS * DIFF_V_DIM, CONV_CH
    o = 0
    fq, fk, fv = (w_in[..., o + k * fw:o + (k + 1) * fw] for k in range(3))
    o += 3 * fw
    ff = w_in[..., o:o + FOX_HEADS]
    o += FOX_HEADS
    dq, dk, dv = (w_in[..., o + k * dw:o + (k + 1) * dw] for k in range(3))
    o += 3 * dw
    gb, gc, hc = (w_in[..., o + k * cc:o + (k + 1) * cc] for k in range(3))
    parts = [fq, fk, fv, dq, dk, dv]
    for t in range(_N_CONV_TILES):
        sl = slice(t * CONV_TILE, (t + 1) * CONV_TILE)
        parts += [gb[..., sl], gc[..., sl], hc[..., sl]]
    w = jnp.concatenate(parts, axis=-1).astype(BF16)
    wff = jnp.pad(ff, ((0, 0), (0, 0), (0, LANES - FOX_HEADS))).astype(BF16)
    return w, wff


def kernel(x, mem, positions, ffn1_norm, ffn1_w_gate, ffn1_w_up, ffn1_w_down, mix_norm, mix_w_in, forget_bias, conv_w, conv_b, lambda_q1, lambda_k1, lambda_q2, lambda_k2, diff_subln, mix_w_out, cross_norm, mem_norm, cross_w_q, cross_w_kv, cross_w_o, ffn2_norm, ffn2_w_gate, ffn2_w_up, ffn2_w_down, final_norm):
    batch, seq_len, d = x.shape
    mem_len = mem.shape[1]
    depth = ffn1_norm.shape[0]
    n_tok = batch * seq_len
    assert seq_len % TOKEN_TILE == 0 and seq_len % ATT_TILE == 0
    assert ffn1_w_gate.shape[2] % FF_TILE == 0

    bf = lambda a: a.astype(BF16)
    rows3 = lambda a: a.astype(F32).reshape(a.shape[0], 1, a.shape[1])
    w1g, w1u, w1d = bf(ffn1_w_gate), bf(ffn1_w_up), bf(ffn1_w_down)
    w2g, w2u, w2d = bf(ffn2_w_gate), bf(ffn2_w_up), bf(ffn2_w_down)
    w_in, w_ff = _arrange_in_weights(mix_w_in)
    w_out = bf(mix_w_out)
    wq, wkv, wo = bf(cross_w_q), bf(cross_w_kv), bf(cross_w_o)
    g_ffn1, g_mix, g_cross, g_mem, g_ffn2 = (rows3(a) for a in (ffn1_norm, mix_norm, cross_norm,
                                                                 mem_norm, ffn2_norm))
    g_final = final_norm.astype(F32).reshape(1, 1, d)
    fbias = rows3(jnp.pad(forget_bias, ((0, 0), (0, LANES - FOX_HEADS))))
    cbias = rows3(conv_b)
    subln = rows3(diff_subln)
    lam_vecs = jnp.stack([lambda_q1, lambda_k1, lambda_q2, lambda_k2], axis=1).astype(F32)
    ctab, stab = _rope_tables(positions)

    h = x.reshape(n_tok, d)
    mem2 = mem.reshape(batch * mem_len, d)
    nq = seq_len // ATT_TILE
    for l in range(depth):
        h = _ffn(h, g_ffn1, w1g, w1u, w1d, g_final, l, final=False)

        qk, conv, c, fox_vt, diff_vt = _inproj(h, g_mix, w_in, w_ff, fbias, ctab, stab,
                                               conv_w.astype(F32), cbias, l, seq_len=seq_len)
        c_rows = (c.reshape(batch, seq_len, LANES)[:, :, :FOX_HEADS]
                  .transpose(0, 2, 1).reshape(batch * FOX_HEADS, nq, 1, ATT_TILE))
        fox = _fox_attention(qk, fox_vt, c, c_rows, batch=batch, seq_len=seq_len)
        lam_init = 0.8 - 0.6 * math.exp(-0.3 * l)
        diff = _diff_attention(qk, diff_vt, lam_vecs, subln, l, batch=batch, seq_len=seq_len,
                               lam_init=lam_init)
        h = _mixout(h, fox, diff, conv, w_out, l)

        kv = _memkv(mem2, g_mem, wkv, l)
        h = _cross(h, g_cross, wq, kv, wo, l, seq_len=seq_len, mem_len=mem_len)

        h = _ffn(h, g_ffn2, w2g, w2u, w2d, g_final, l, final=(l == depth - 1))
    return h.reshape(batch, seq_len, d)
```

```python
import functools
import math

import numpy as np
import jax
import jax.numpy as jnp
from jax import lax
from jax.experimental import pallas as pl
from jax.experimental.pallas import tpu as pltpu

F32 = jnp.float32
BF16 = jnp.bfloat16

FOX_HEADS = 6
FOX_HEAD_DIM = 128
DIFF_HEADS = 6
DIFF_QK_DIM = 64
DIFF_V_DIM = 128
CONV_CH = 512
CONV_WIDTH = 3
ROT_DIM = 16
ROPE_THETA = 500000.0
CROSS_HEADS = 4
CROSS_HEAD_DIM = 128
EPS = 1e-6
NEG_INF = -1e30
LOG2E = math.log2(math.e)
FOX_Q_SCALE = FOX_HEAD_DIM ** -0.5 * LOG2E
DIFF_Q_SCALE = DIFF_QK_DIM ** -0.5 * LOG2E

LANES = 128
SUBLANES = 8
BF16_ROWS = 16

TOKEN_TILE = 512
FF_TILE = 512
IN_TILE = 768
CONV_TILE = 256
ATT_TILE = 512
ONES_ROWS = BF16_ROWS
VMEM_LIMIT = 56 * 1024 * 1024

_ARB = "arbitrary"


def _params(n_axes):
    return pltpu.CompilerParams(dimension_semantics=(_ARB,) * n_axes,
                                vmem_limit_bytes=VMEM_LIMIT)


def _rms(x, g):
    return x * lax.rsqrt(jnp.mean(x * x, axis=-1, keepdims=True) + EPS) * g


def _dot(a, b):
    return jnp.dot(a, b, preferred_element_type=F32)


def _dot_t(a, b):
    return lax.dot_general(a, b, (((1,), (1,)), ((), ())), preferred_element_type=F32)


def _layer_spec(shape, index_map):
    return pl.BlockSpec((None,) + tuple(shape), index_map)


def _ffn_body(x_ref, g_ref, wg_ref, wu_ref, wd_ref, fg_ref, o_ref, n_ref, acc_ref, *, final):
    j = pl.program_id(1)

    @pl.when(j == 0)
    def _():
        n_ref[...] = _rms(x_ref[...], g_ref[...]).astype(BF16)
        acc_ref[...] = jnp.zeros_like(acc_ref)

    n = n_ref[...]
    gate = _dot(n, wg_ref[...])
    up = _dot(n, wu_ref[...])
    act = (gate * jax.nn.sigmoid(gate) * up).astype(BF16)
    acc_ref[...] += _dot(act, wd_ref[...])

    @pl.when(j == pl.num_programs(1) - 1)
    def _():
        y = x_ref[...] + 0.5 * acc_ref[...]
        if final:
            y = _rms(y, fg_ref[...])
        o_ref[...] = y


def _ffn(h, g, wg, wu, wd, final_g, l, *, final):
    n_tok, d = h.shape
    d_ff = wg.shape[2]
    tm, tf = TOKEN_TILE, FF_TILE
    return pl.pallas_call(
        functools.partial(_ffn_body, final=final),
        grid=(n_tok // tm, d_ff // tf),
        in_specs=[
            pl.BlockSpec((tm, d), lambda i, j: (i, 0)),
            _layer_spec((1, d), lambda i, j: (l, 0, 0)),
            _layer_spec((d, tf), lambda i, j: (l, 0, j)),
            _layer_spec((d, tf), lambda i, j: (l, 0, j)),
            _layer_spec((tf, d), lambda i, j: (l, j, 0)),
            _layer_spec((1, d), lambda i, j: (0, 0, 0)),
        ],
        out_specs=pl.BlockSpec((tm, d), lambda i, j: (i, 0)),
        out_shape=jax.ShapeDtypeStruct((n_tok, d), F32),
        scratch_shapes=[pltpu.VMEM((tm, d), BF16), pltpu.VMEM((tm, d), F32)],
        compiler_params=_params(2),
        name="ffn_final" if final else "ffn",
    )(h, g, wg, wu, wd, final_g)


_J_FOX_Q, _J_FOX_K, _J_FOX_V, _J_DIFF_Q, _J_DIFF_K, _J_DIFF_V, _J_CONV = 0, 1, 2, 3, 4, 5, 6
_N_CONV_TILES = CONV_CH // CONV_TILE
_N_IN_TILES = _J_CONV + _N_CONV_TILES


def _inproj_body(x_ref, g_ref, w_ref, wff_ref, fb_ref, ct_ref, st_ref, cw_ref, cb_ref,
                 qk_ref, conv_ref, c_ref, fvt_ref, dvt_ref, n_ref, zbuf_ref, carry_ref,
                 *, tiles_per_seq):
    i = pl.program_id(0)
    j = pl.program_id(1)
    tm = x_ref.shape[0]
    seq_start = (i % tiles_per_seq) == 0

    @pl.when(j == 0)
    def _():
        n_ref[...] = _rms(x_ref[...], g_ref[...]).astype(BF16)

        @pl.when(seq_start)
        def _():
            zbuf_ref[:, 0:SUBLANES, :] = jnp.zeros((_N_CONV_TILES, SUBLANES, CONV_TILE), F32)

    def project():
        return _dot(n_ref[...], w_ref[...])

    @pl.when(j == _J_FOX_Q)
    def _():
        qk_ref[...] = (project() * FOX_Q_SCALE).astype(BF16)
        logf = jax.nn.log_sigmoid(_dot(n_ref[...], wff_ref[...]) + fb_ref[...])
        lt = logf.T[0:BF16_ROWS, :]
        hi = lt.astype(BF16)
        r1 = lt - hi.astype(F32)
        mid = r1.astype(BF16)
        lo = (r1 - mid.astype(F32)).astype(BF16)
        src = lax.broadcasted_iota(jnp.int32, (tm, tm), 0)
        dst = lax.broadcasted_iota(jnp.int32, (tm, tm), 1)
        tri = jnp.where(src <= dst, 1.0, 0.0).astype(BF16)
        parts = _dot(jnp.concatenate([hi, mid, lo], axis=0), tri)
        prev = jnp.where(seq_start, 0.0, carry_ref[...])
        ct = (parts[0:BF16_ROWS] + parts[BF16_ROWS:2 * BF16_ROWS] + parts[2 * BF16_ROWS:]) + prev
        carry_ref[...] = ct[:, tm - 1:tm]
        c_ref[...] = jnp.concatenate([ct, jnp.zeros((LANES - BF16_ROWS, tm), F32)], axis=0).T

    @pl.when(j == _J_FOX_K)
    def _():
        qk_ref[...] = project().astype(BF16)

    @pl.when(j == _J_FOX_V)
    def _():
        fvt_ref[...] = project().T.astype(BF16)

    @pl.when(j == _J_DIFF_V)
    def _():
        dvt_ref[...] = project().T.astype(BF16)

    for jj, q_scale in ((_J_DIFF_Q, DIFF_Q_SCALE), (_J_DIFF_K, None)):
        @pl.when(j == jj)
        def _(q_scale=q_scale):
            y = project()
            ct = ct_ref[...]
            st = st_ref[...]
            for c0 in range(0, IN_TILE, LANES):
                yc = y[:, c0:c0 + LANES]
                out = yc * ct + pltpu.roll(yc, LANES // 2, axis=1) * st
                if q_scale is not None:
                    out = out * q_scale
                qk_ref[:, c0:c0 + LANES] = out.astype(BF16)

    for t in range(_N_CONV_TILES):
        @pl.when(j == _J_CONV + t)
        def _(t=t):
            y = project()
            gb = y[:, 0:CONV_TILE]
            z = y[:, CONV_TILE:2 * CONV_TILE] * y[:, 2 * CONV_TILE:3 * CONV_TILE]
            zb = zbuf_ref.at[t]
            zb[SUBLANES:SUBLANES + tm, :] = z
            z1 = zb[SUBLANES - 1:SUBLANES - 1 + tm, :]
            z2 = zb[SUBLANES - 2:SUBLANES - 2 + tm, :]
            cw = cw_ref[...]
            conv = z2 * cw[0:1, :] + z1 * cw[1:2, :] + z * cw[2:3, :] + cb_ref[...]
            conv_ref[...] = (gb * conv).astype(BF16)
            zb[0:SUBLANES, :] = z[tm - SUBLANES:tm, :]


def _inproj(h, g, w, wff, fbias, ctab, stab, conv_w, conv_b, l, *, seq_len):
    n_tok, d = h.shape
    tm = TOKEN_TILE
    conv_col = lambda j: jnp.clip(j - _J_CONV, 0, _N_CONV_TILES - 1)
    qk_col = lambda j: jnp.minimum(j - (j >= _J_FOX_V).astype(jnp.int32)
                                   - (j >= _J_DIFF_V).astype(jnp.int32), 3)
    return pl.pallas_call(
        functools.partial(_inproj_body, tiles_per_seq=seq_len // tm),
        grid=(n_tok // tm, _N_IN_TILES),
        in_specs=[
            pl.BlockSpec((tm, d), lambda i, j: (i, 0)),
            _layer_spec((1, d), lambda i, j: (l, 0, 0)),
            _layer_spec((d, IN_TILE), lambda i, j: (l, 0, j)),
            _layer_spec((d, LANES), lambda i, j: (l, 0, 0)),
            _layer_spec((1, LANES), lambda i, j: (l, 0, 0)),
            pl.BlockSpec((tm, LANES), lambda i, j: (i, 0)),
            pl.BlockSpec((tm, LANES), lambda i, j: (i, 0)),
            _layer_spec((CONV_WIDTH, CONV_TILE), lambda i, j: (l, 0, conv_col(j))),
            _layer_spec((1, CONV_TILE), lambda i, j: (l, 0, conv_col(j))),
        ],
        out_specs=[
            pl.BlockSpec((tm, IN_TILE), lambda i, j: (i, qk_col(j))),
            pl.BlockSpec((tm, CONV_TILE), lambda i, j: (i, conv_col(j))),
            pl.BlockSpec((tm, LANES), lambda i, j: (i, 0)),
            pl.BlockSpec((IN_TILE, tm), lambda i, j: (0, i)),
            pl.BlockSpec((IN_TILE, tm), lambda i, j: (0, i)),
        ],
        out_shape=[
            jax.ShapeDtypeStruct((n_tok, 4 * IN_TILE), BF16),
            jax.ShapeDtypeStruct((n_tok, CONV_CH), BF16),
            jax.ShapeDtypeStruct((n_tok, LANES), F32),
            jax.ShapeDtypeStruct((IN_TILE, n_tok), BF16),
            jax.ShapeDtypeStruct((IN_TILE, n_tok), BF16),
        ],
        scratch_shapes=[
            pltpu.VMEM((tm, d), BF16),
            pltpu.VMEM((_N_CONV_TILES, tm + SUBLANES, CONV_TILE), F32),
            pltpu.VMEM((BF16_ROWS, 1), F32),
        ],
        compiler_params=_params(2),
        name="inproj",
    )(h, g, w, wff, fbias, ctab, stab, conv_w, conv_b)


_T_QI, _T_KB, _T_DIAG, _T_STATE = 0, 1, 2, 3
PIPE_LAG = 2
PIPE_SLOTS = 2 * PIPE_LAG


def _item_table(nq):
    items = [(qi, kb) for qi in range(nq) for kb in range(qi + 1)]
    n_loops = -(-(len(items) + PIPE_LAG) // PIPE_SLOTS)
    n_pos = PIPE_SLOTS * n_loops + 2 * PIPE_LAG
    tab = np.zeros((4, n_pos), np.int32)
    for pos in range(n_pos):
        item = pos - PIPE_LAG
        qi, kb = items[min(max(item, 0), len(items) - 1)]
        real = 0 <= item < len(items)
        tab[:, pos] = (qi, kb, int(real and kb == qi), qi if real else nq)
    return tab, n_loops


def _fill_bias(bias_ref, t):
    key = lax.broadcasted_iota(jnp.int32, (t, t), 0)
    qry = lax.broadcasted_iota(jnp.int32, (t, t), 1)
    bias_ref[0] = jnp.zeros((t, t), F32)
    bias_ref[1] = jnp.where(key <= qry, 0.0, NEG_INF)


def _stage_values(vt_ref, vte_ref, t):
    hd = vt_ref.shape[0]
    for kb in range(vte_ref.shape[0]):
        vte_ref[kb, 0:hd, :] = vt_ref[:, kb * t:(kb + 1) * t]
        vte_ref[kb, hd:, :] = jnp.ones((ONES_ROWS, t), BF16)


def _store_scores(u, u_ref, mx_ref):
    u_ref[...] = u
    mx_ref[...] = jnp.max(u, axis=0, keepdims=True)


def _softmax_stage(u_ref, mx_ref, shift, m_ref, p_ref, al_ref):
    m_old = m_ref[...]
    m_new = jnp.maximum(m_old, mx_ref[...] + shift)
    al_ref[...] = jnp.exp2(m_old - m_new)
    p_ref[...] = jnp.exp2(u_ref[...] - (m_new - shift)).astype(BF16)
    m_ref[...] = m_new


def _value_stage(vt, p_ref, al_ref, acc_ref):
    acc_ref[...] = al_ref[...] * acc_ref[...] + _dot(vt, p_ref[...])


def _fox_body(tab_ref, q_ref, k_ref, vt_ref, c_ref, crow_ref, o_ref,
              ckb_ref, vte_ref, bias_ref, u_ref, mx_ref, p_ref, al_ref, m_ref, acc_ref, *, n_loops):
    b = pl.program_id(0)
    h = pl.program_id(1)
    t = ATT_TILE
    hd = FOX_HEAD_DIM

    @pl.when((b == 0) & (h == 0))
    def _():
        _fill_bias(bias_ref, t)

    lane = lax.broadcasted_iota(jnp.int32, c_ref.shape, 1)
    ck = jnp.sum(jnp.where(lane == h, c_ref[...], 0.0), axis=-1, keepdims=True)
    ckb_ref[...] = jnp.broadcast_to(ck * LOG2E, ckb_ref.shape)
    _stage_values(vt_ref, vte_ref, t)
    p_ref[PIPE_LAG:] = jnp.zeros((PIPE_LAG, t, t), BF16)
    al_ref[PIPE_LAG:] = jnp.zeros((PIPE_LAG, 1, t), F32)
    acc_ref[...] = jnp.zeros_like(acc_ref)
    m_ref[...] = jnp.full_like(m_ref, NEG_INF)

    def rows(idx):
        return pl.ds(pl.multiple_of(idx * t, t), t)

    def score_stage(pos, slot):
        kb = tab_ref[_T_KB, pos]
        u = (_dot_t(k_ref[rows(kb), :], q_ref[rows(tab_ref[_T_QI, pos]), :])
             - jnp.tile(ckb_ref[rows(kb), :], (1, t // LANES)) + bias_ref[tab_ref[_T_DIAG, pos]])
        _store_scores(u, u_ref.at[slot], mx_ref.at[slot])

    def step(tau, slot):
        lagged = (slot + PIPE_LAG) % PIPE_SLOTS
        score_stage(tau + 2 * PIPE_LAG, lagged)
        pos = tau + PIPE_LAG
        cq = crow_ref[0, tab_ref[_T_QI, pos]] * LOG2E
        _softmax_stage(u_ref.at[slot], mx_ref.at[slot], cq, m_ref.at[tab_ref[_T_STATE, pos]],
                       p_ref.at[slot], al_ref.at[slot])
        _value_stage(vte_ref[tab_ref[_T_KB, tau]], p_ref.at[lagged], al_ref.at[lagged],
                     acc_ref.at[tab_ref[_T_STATE, tau]])

    for item in range(PIPE_LAG):
        score_stage(item + PIPE_LAG, item)

    def unrolled(it, carry):
        for slot in range(PIPE_SLOTS):
            step(PIPE_SLOTS * it + slot, slot)
        return carry

    lax.fori_loop(0, n_loops, unrolled, 0)

    for qi in range(acc_ref.shape[0] - 1):
        acc = acc_ref[qi]
        o_ref[qi * t:(qi + 1) * t, :] = (acc[:hd, :] / acc[hd:hd + 1, :]).T.astype(BF16)


def _attention_scratch(t, nq, rows, n_streams):
    shape = lambda *s: ((n_streams,) if n_streams > 1 else ()) + s
    return [
        pltpu.VMEM((PIPE_SLOTS,) + shape(t, t), F32),
        pltpu.VMEM((PIPE_SLOTS,) + shape(1, t), F32),
        pltpu.VMEM((PIPE_SLOTS,) + shape(t, t), BF16),
        pltpu.VMEM((PIPE_SLOTS,) + shape(1, t), F32),
        pltpu.VMEM(shape(nq + 1, 1, t), F32),
        pltpu.VMEM(shape(nq + 1, rows, t), F32),
    ]


def _fox_attention(qk, vt, c, c_rows, *, batch, seq_len):
    t = ATT_TILE
    nq = seq_len // t
    hd = FOX_HEAD_DIM
    rows = hd + ONES_ROWS
    tab, n_loops = _item_table(nq)
    grid_spec = pltpu.PrefetchScalarGridSpec(
        num_scalar_prefetch=1,
        grid=(batch, FOX_HEADS),
        in_specs=[
            pl.BlockSpec((seq_len, hd), lambda b, h, tab: (b, h)),
            pl.BlockSpec((seq_len, hd), lambda b, h, tab: (b, FOX_HEADS + h)),
            pl.BlockSpec((hd, seq_len), lambda b, h, tab: (h, b)),
            pl.BlockSpec((seq_len, LANES), lambda b, h, tab: (b, 0)),
            pl.BlockSpec((1, nq, 1, t), lambda b, h, tab: (b * FOX_HEADS + h, 0, 0, 0)),
        ],
        out_specs=pl.BlockSpec((seq_len, hd), lambda b, h, tab: (b, h)),
        scratch_shapes=[pltpu.VMEM((seq_len, LANES), F32),
                        pltpu.VMEM((nq, rows, t), BF16),
                        pltpu.VMEM((2, t, t), F32)] + _attention_scratch(t, nq, rows, 1),
    )
    return pl.pallas_call(
        functools.partial(_fox_body, n_loops=n_loops),
        grid_spec=grid_spec,
        out_shape=jax.ShapeDtypeStruct((batch * seq_len, FOX_HEADS * hd), BF16),
        compiler_params=_params(2),
        name="fox_attention",
    )(jnp.asarray(tab), qk, qk, vt, c, c_rows)


def _diff_body(tab_ref, q_ref, k_ref, vt_ref, lam_ref, sg_ref, o_ref,
               vte_ref, bias_ref, u_ref, mx_ref, p_ref, al_ref, m_ref, acc_ref, *, n_loops, lam_init):
    b = pl.program_id(0)
    h = pl.program_id(1)
    t = ATT_TILE
    hd = DIFF_V_DIM

    @pl.when((b == 0) & (h == 0))
    def _():
        _fill_bias(bias_ref, t)

    _stage_values(vt_ref, vte_ref, t)
    p_ref[PIPE_LAG:] = jnp.zeros((PIPE_LAG, 2, t, t), BF16)
    al_ref[PIPE_LAG:] = jnp.zeros((PIPE_LAG, 2, 1, t), F32)
    acc_ref[...] = jnp.zeros_like(acc_ref)
    m_ref[...] = jnp.full_like(m_ref, NEG_INF)
    no_shift = jnp.zeros((1, t), F32)

    def rows(idx):
        return pl.ds(pl.multiple_of(idx * t, t), t)

    def score_stage(pos, slot):
        q = q_ref[rows(tab_ref[_T_QI, pos]), :]
        k = k_ref[rows(tab_ref[_T_KB, pos]), :]
        bias = bias_ref[tab_ref[_T_DIAG, pos]]
        lane = lax.broadcasted_iota(jnp.int32, q.shape, 1)
        comp0 = (lane < ROT_DIM // 2) | ((lane >= ROT_DIM) & (lane < LANES // 2 + ROT_DIM // 2))
        zero = jnp.zeros_like(q)
        for s, keep in enumerate((comp0, jnp.logical_not(comp0))):
            _store_scores(_dot_t(k, jnp.where(keep, q, zero)) + bias,
                          u_ref.at[slot, s], mx_ref.at[slot, s])

    def step(tau, slot):
        lagged = (slot + PIPE_LAG) % PIPE_SLOTS
        score_stage(tau + 2 * PIPE_LAG, lagged)
        state = tab_ref[_T_STATE, tau + PIPE_LAG]
        vt = vte_ref[tab_ref[_T_KB, tau]]
        for s in range(2):
            _softmax_stage(u_ref.at[slot, s], mx_ref.at[slot, s], no_shift, m_ref.at[s, state],
                           p_ref.at[slot, s], al_ref.at[slot, s])
            _value_stage(vt, p_ref.at[lagged, s], al_ref.at[lagged, s],
                         acc_ref.at[s, tab_ref[_T_STATE, tau]])

    for item in range(PIPE_LAG):
        score_stage(item + PIPE_LAG, item)

    def unrolled(it, carry):
        for slot in range(PIPE_SLOTS):
            step(PIPE_SLOTS * it + slot, slot)
        return carry

    lax.fori_loop(0, n_loops, unrolled, 0)

    lv = lam_ref[...]
    lam = (jnp.exp(jnp.sum(lv[0:1, :] * lv[1:2, :], axis=-1, keepdims=True))
           - jnp.exp(jnp.sum(lv[2:3, :] * lv[3:4, :], axis=-1, keepdims=True))
           + lam_init)
    for qi in range(acc_ref.shape[1] - 1):
        a1 = acc_ref[0, qi]
        a2 = acc_ref[1, qi]
        o = a1[:hd, :] / a1[hd:hd + 1, :] - lam * (a2[:hd, :] / a2[hd:hd + 1, :])
        o = o * lax.rsqrt(jnp.mean(o * o, axis=0, keepdims=True) + EPS)
        o_ref[qi * t:(qi + 1) * t, :] = (o.T * sg_ref[...] * (1.0 - lam_init)).astype(BF16)


def _diff_attention(qk, vt, lam_vecs, subln, l, *, batch, seq_len, lam_init):
    t = ATT_TILE
    nq = seq_len // t
    hd = DIFF_V_DIM
    rows = hd + ONES_ROWS
    base = 2 * FOX_HEADS
    tab, n_loops = _item_table(nq)
    grid_spec = pltpu.PrefetchScalarGridSpec(
        num_scalar_prefetch=1,
        grid=(batch, DIFF_HEADS),
        in_specs=[
            pl.BlockSpec((seq_len, hd), lambda b, h, tab: (b, base + h)),
            pl.BlockSpec((seq_len, hd), lambda b, h, tab: (b, base + DIFF_HEADS + h)),
            pl.BlockSpec((hd, seq_len), lambda b, h, tab: (h, b)),
            _layer_spec((4, DIFF_QK_DIM), lambda b, h, tab: (l, 0, 0)),
            _layer_spec((1, hd), lambda b, h, tab: (l, 0, 0)),
        ],
        out_specs=pl.BlockSpec((seq_len, hd), lambda b, h, tab: (b, h)),
        scratch_shapes=[pltpu.VMEM((nq, rows, t), BF16),
                        pltpu.VMEM((2, t, t), F32)] + _attention_scratch(t, nq, rows, 2),
    )
    return pl.pallas_call(
        functools.partial(_diff_body, n_loops=n_loops, lam_init=lam_init),
        grid_spec=grid_spec,
        out_shape=jax.ShapeDtypeStruct((batch * seq_len, DIFF_HEADS * hd), BF16),
        compiler_params=_params(2),
        name="diff_attention",
    )(jnp.asarray(tab), qk, qk, vt, lam_vecs, subln)


def _mixout_body(h_ref, fox_ref, diff_ref, conv_ref, wf_ref, wd_ref, wc_ref, o_ref):
    o_ref[...] = (h_ref[...] + _dot(fox_ref[...], wf_ref[...])
                  + _dot(diff_ref[...], wd_ref[...]) + _dot(conv_ref[...], wc_ref[...]))


def _mixout(h, fox, diff, conv, w_out, l):
    n_tok, d = h.shape
    tm = TOKEN_TILE
    fw, dw, cw = fox.shape[1], diff.shape[1], conv.shape[1]
    assert fw == dw and (fw + dw) % cw == 0
    row = lambda i: (i, 0)
    resident = lambda rows, blk: pl.BlockSpec((None, rows, d), lambda i: (l, blk, 0),
                                              pipeline_mode=pl.Buffered(1))
    return pl.pallas_call(
        _mixout_body,
        grid=(n_tok // tm,),
        in_specs=[
            pl.BlockSpec((tm, d), row),
            pl.BlockSpec((tm, fw), row),
            pl.BlockSpec((tm, dw), row),
            pl.BlockSpec((tm, cw), row),
            resident(fw, 0),
            resident(dw, 1),
            resident(cw, (fw + dw) // cw),
        ],
        out_specs=pl.BlockSpec((tm, d), row),
        out_shape=jax.ShapeDtypeStruct((n_tok, d), F32),
        compiler_params=_params(1),
        name="mixout",
    )(h, fox, diff, conv, w_out, w_out, w_out)


def _memkv_body(m_ref, g_ref, w_ref, o_ref):
    o_ref[...] = _dot(_rms(m_ref[...], g_ref[...]).astype(BF16), w_ref[...]).astype(BF16)


def _memkv(mem, g, w, l):
    n_mem, d = mem.shape
    tm = min(n_mem, TOKEN_TILE)
    width = w.shape[2]
    return pl.pallas_call(
        _memkv_body,
        grid=(n_mem // tm,),
        in_specs=[
            pl.BlockSpec((tm, d), lambda i: (i, 0)),
            _layer_spec((1, d), lambda i: (l, 0, 0)),
            pl.BlockSpec((None, d, width), lambda i: (l, 0, 0), pipeline_mode=pl.Buffered(1)),
        ],
        out_specs=pl.BlockSpec((tm, width), lambda i: (i, 0)),
        out_shape=jax.ShapeDtypeStruct((n_mem, width), BF16),
        compiler_params=_params(1),
        name="memkv",
    )(mem, g, w)


def _cross_body(h_ref, g_ref, wq_ref, kv_ref, wo_ref, o_ref):
    hd = CROSS_HEAD_DIM
    width = CROSS_HEADS * hd
    scale = hd ** -0.5
    x = h_ref[...]
    q = _dot(_rms(x, g_ref[...]).astype(BF16), wq_ref[...]).astype(BF16)
    heads = []
    for hh in range(CROSS_HEADS):
        k = kv_ref[:, hh * hd:(hh + 1) * hd]
        v = kv_ref[:, width + hh * hd:width + (hh + 1) * hd]
        s = _dot_t(q[:, hh * hd:(hh + 1) * hd], k) * scale
        e = jnp.exp(s - jnp.max(s, axis=-1, keepdims=True))
        p = e / jnp.sum(e, axis=-1, keepdims=True)
        heads.append(_dot(p.astype(BF16), v))
    o = jnp.concatenate(heads, axis=-1).astype(BF16)
    o_ref[...] = x + _dot(o, wo_ref[...])


def _cross(h, g, wq, kv, wo, l, *, seq_len, mem_len):
    n_tok, d = h.shape
    tm = TOKEN_TILE
    tiles_per_seq = seq_len // tm
    resident = lambda a: pl.BlockSpec((None,) + a.shape[1:], lambda i: (l, 0, 0),
                                      pipeline_mode=pl.Buffered(1))
    return pl.pallas_call(
        _cross_body,
        grid=(n_tok // tm,),
        in_specs=[
            pl.BlockSpec((tm, d), lambda i: (i, 0)),
            _layer_spec((1, d), lambda i: (l, 0, 0)),
            resident(wq),
            pl.BlockSpec((mem_len, kv.shape[1]), lambda i: (i // tiles_per_seq, 0)),
            resident(wo),
        ],
        out_specs=pl.BlockSpec((tm, d), lambda i: (i, 0)),
        out_shape=jax.ShapeDtypeStruct((n_tok, d), F32),
        compiler_params=_params(1),
        name="cross_attention",
    )(h, g, wq, kv, wo)


def _rope_tables(positions):
    half = ROT_DIM // 2
    inv_freq = ROPE_THETA ** (-jnp.arange(0, ROT_DIM, 2, dtype=F32) / ROT_DIM)
    ang = positions.astype(F32)[..., None] * inv_freq
    cos = jnp.cos(ang).reshape(-1, half)
    sin = jnp.sin(ang).reshape(-1, half)
    n = cos.shape[0]
    rest = LANES // 2 - ROT_DIM
    ones, zeros = jnp.ones((n, rest), F32), jnp.zeros((n, rest), F32)
    ct = jnp.concatenate([cos, cos, ones, cos, cos, ones], axis=-1)
    st = jnp.concatenate([-sin, -sin, zeros, sin, sin, zeros], axis=-1)
    return ct, st


def _diff_head_lanes():
    half = ROT_DIM // 2
    src = np.zeros(LANES, np.int64)
    comp0 = np.zeros(LANES, bool)
    for c in range(2):
        for d in range(DIFF_QK_DIM):
            if d < half:
                lane = c * half + d
            elif d < ROT_DIM:
                lane = LANES // 2 + c * half + (d - half)
            else:
                lane = (ROT_DIM if c == 0 else LANES // 2 + ROT_DIM) + (d - ROT_DIM)
            src[lane] = c * DIFF_QK_DIM + d
            comp0[lane] = c == 0
    return src, comp0


def _arrange_in_weights(w_in):
    fw, dw, cc = FOX_HEADS * FOX_HEAD_DIM, DIFF_HEADS * DIFF_V_DIM, CONV_CH
    o = 0
    fq, fk, fv = (w_in[..., o + k * fw:o + (k + 1) * fw] for k in range(3))
    o += 3 * fw
    ff = w_in[..., o:o + FOX_HEADS]
    o += FOX_HEADS
    dq, dk, dv = (w_in[..., o + k * dw:o + (k + 1) * dw] for k in range(3))
    o += 3 * dw
    gb, gc, hc = (w_in[..., o + k * cc:o + (k + 1) * cc] for k in range(3))
    src, _ = _diff_head_lanes()
    cols = np.concatenate([h * LANES + src for h in range(DIFF_HEADS)])
    dq, dk = dq[..., cols], dk[..., cols]
    parts = [fq, fk, fv, dq, dk, dv]
    for t in range(_N_CONV_TILES):
        sl = slice(t * CONV_TILE, (t + 1) * CONV_TILE)
        parts += [gb[..., sl], gc[..., sl], hc[..., sl]]
    w = jnp.concatenate(parts, axis=-1).astype(BF16)
    wff = jnp.pad(ff, ((0, 0), (0, 0), (0, LANES - FOX_HEADS))).astype(BF16)
    return w, wff


def kernel(x, mem, positions, ffn1_norm, ffn1_w_gate, ffn1_w_up, ffn1_w_down, mix_norm, mix_w_in, forget_bias, conv_w, conv_b, lambda_q1, lambda_k1, lambda_q2, lambda_k2, diff_subln, mix_w_out, cross_norm, mem_norm, cross_w_q, cross_w_kv, cross_w_o, ffn2_norm, ffn2_w_gate, ffn2_w_up, ffn2_w_down, final_norm):
    batch, seq_len, d = x.shape
    mem_len = mem.shape[1]
    depth = ffn1_norm.shape[0]
    n_tok = batch * seq_len
    assert seq_len % TOKEN_TILE == 0 and seq_len % ATT_TILE == 0
    assert ffn1_w_gate.shape[2] % FF_TILE == 0

    bf = lambda a: a.astype(BF16)
    rows3 = lambda a: a.astype(F32).reshape(a.shape[0], 1, a.shape[1])
    w1g, w1u, w1d = bf(ffn1_w_gate), bf(ffn1_w_up), bf(ffn1_w_down)
    w2g, w2u, w2d = bf(ffn2_w_gate), bf(ffn2_w_up), bf(ffn2_w_down)
    w_in, w_ff = _arrange_in_weights(mix_w_in)
    w_out = bf(mix_w_out)
    wq, wkv, wo = bf(cross_w_q), bf(cross_w_kv), bf(cross_w_o)
    g_ffn1, g_mix, g_cross, g_mem, g_ffn2 = (rows3(a) for a in (ffn1_norm, mix_norm, cross_norm,
                                                                 mem_norm, ffn2_norm))
    g_final = final_norm.astype(F32).reshape(1, 1, d)
    fbias = rows3(jnp.pad(forget_bias, ((0, 0), (0, LANES - FOX_HEADS))))
    cbias = rows3(conv_b)
    subln = rows3(diff_subln)
    lam_vecs = jnp.stack([lambda_q1, lambda_k1, lambda_q2, lambda_k2], axis=1).astype(F32)
    ctab, stab = _rope_tables(positions)

    h = x.reshape(n_tok, d)
    mem2 = mem.reshape(batch * mem_len, d)
    nq = seq_len // ATT_TILE
    for l in range(depth):
        h = _ffn(h, g_ffn1, w1g, w1u, w1d, g_final, l, final=False)

        qk, conv, c, fox_vt, diff_vt = _inproj(h, g_mix, w_in, w_ff, fbias, ctab, stab,
                                               conv_w.astype(F32), cbias, l, seq_len=seq_len)
        c_rows = (c.reshape(batch, seq_len, LANES)[:, :, :FOX_HEADS]
                  .transpose(0, 2, 1).reshape(batch * FOX_HEADS, nq, 1, ATT_TILE))
        fox = _fox_attention(qk, fox_vt, c, c_rows, batch=batch, seq_len=seq_len)
        lam_init = 0.8 - 0.6 * math.exp(-0.3 * l)
        diff = _diff_attention(qk, diff_vt, lam_vecs, subln, l, batch=batch, seq_len=seq_len,
                               lam_init=lam_init)
        h = _mixout(h, fox, diff, conv, w_out, l)

        kv = _memkv(mem2, g_mem, wkv, l)
        h = _cross(h, g_cross, wq, kv, wo, l, seq_len=seq_len, mem_len=mem_len)

        h = _ffn(h, g_ffn2, w2g, w2u, w2d, g_final, l, final=(l == depth - 1))
    return h.reshape(batch, seq_len, d)
```

```python
import functools
import math

import numpy as np
import jax
import jax.numpy as jnp
from jax import lax
from jax.experimental import pallas as pl
from jax.experimental.pallas import tpu as pltpu

F32 = jnp.float32
BF16 = jnp.bfloat16

FOX_HEADS = 6
FOX_HEAD_DIM = 128
DIFF_HEADS = 6
DIFF_QK_DIM = 64
DIFF_V_DIM = 128
CONV_CH = 512
CONV_WIDTH = 3
ROT_DIM = 16
ROPE_THETA = 500000.0
CROSS_HEADS = 4
CROSS_HEAD_DIM = 128
EPS = 1e-6
NEG_INF = -1e30
LOG2E = math.log2(math.e)
FOX_Q_SCALE = FOX_HEAD_DIM ** -0.5 * LOG2E
DIFF_Q_SCALE = DIFF_QK_DIM ** -0.5 * LOG2E

LANES = 128
SUBLANES = 8
BF16_ROWS = 16

TOKEN_TILE = 512
FF_TILE = 512
IN_TILE = 768
CONV_TILE = 256
ATT_TILE = 512
ONES_ROWS = BF16_ROWS
FFN_TOKEN_TILE = 1024
IN_TOKEN_TILE = 1024
VMEM_LIMIT = 56 * 1024 * 1024
FFN_VMEM_LIMIT = 60 * 1024 * 1024

_ARB = "arbitrary"


def _params(n_axes, vmem_limit=VMEM_LIMIT):
    return pltpu.CompilerParams(dimension_semantics=(_ARB,) * n_axes,
                                vmem_limit_bytes=vmem_limit)


def _rms(x, g):
    return x * lax.rsqrt(jnp.mean(x * x, axis=-1, keepdims=True) + EPS) * g


def _dot(a, b):
    return jnp.dot(a, b, preferred_element_type=F32)


def _dot_t(a, b):
    return lax.dot_general(a, b, (((1,), (1,)), ((), ())), preferred_element_type=F32)


def _layer_spec(shape, index_map):
    return pl.BlockSpec((None,) + tuple(shape), index_map)


def _ffn_body(x_ref, g_ref, wg_ref, wu_ref, wd_ref, fg_ref, o_ref, n_ref, *, final):
    j = pl.program_id(1)

    @pl.when(j == 0)
    def _():
        x = x_ref[...]
        n_ref[...] = _rms(x, g_ref[...]).astype(BF16)
        o_ref[...] = x

    n = n_ref[...]
    gate = _dot(n, wg_ref[...])
    up = _dot(n, wu_ref[...])
    act = (gate * jax.nn.sigmoid(gate) * (0.5 * up)).astype(BF16)
    o_ref[...] += _dot(act, wd_ref[...])

    if final:
        @pl.when(j == pl.num_programs(1) - 1)
        def _():
            o_ref[...] = _rms(o_ref[...], fg_ref[...])


def _ffn(h, g, wg, wu, wd, final_g, l, *, final):
    n_tok, d = h.shape
    d_ff = wg.shape[2]
    tm, tf = FFN_TOKEN_TILE, FF_TILE
    return pl.pallas_call(
        functools.partial(_ffn_body, final=final),
        grid=(n_tok // tm, d_ff // tf),
        in_specs=[
            pl.BlockSpec((tm, d), lambda i, j: (i, 0)),
            _layer_spec((1, d), lambda i, j: (l, 0, 0)),
            _layer_spec((d, tf), lambda i, j: (l, 0, j)),
            _layer_spec((d, tf), lambda i, j: (l, 0, j)),
            _layer_spec((tf, d), lambda i, j: (l, j, 0)),
            _layer_spec((1, d), lambda i, j: (0, 0, 0)),
        ],
        out_specs=pl.BlockSpec((tm, d), lambda i, j: (i, 0)),
        out_shape=jax.ShapeDtypeStruct((n_tok, d), F32),
        scratch_shapes=[pltpu.VMEM((tm, d), BF16)],
        compiler_params=_params(2, FFN_VMEM_LIMIT),
        name="ffn_final" if final else "ffn",
    )(h, g, wg, wu, wd, final_g)


_J_FOX_Q, _J_FOX_K, _J_FOX_V, _J_DIFF_Q, _J_DIFF_K, _J_DIFF_V, _J_CONV = 0, 1, 2, 3, 4, 5, 6
_N_CONV_TILES = CONV_CH // CONV_TILE
_N_IN_TILES = _J_CONV + _N_CONV_TILES


def _inproj_body(x_ref, g_ref, w_ref, wff_ref, fb_ref, ct_ref, st_ref, cw_ref, cb_ref,
                 qk_ref, conv_ref, c_ref, fvt_ref, dvt_ref, n_ref, zbuf_ref, carry_ref,
                 *, tiles_per_seq):
    i = pl.program_id(0)
    j = pl.program_id(1)
    tm = x_ref.shape[0]
    seq_start = (i % tiles_per_seq) == 0

    @pl.when(j == 0)
    def _():
        n_ref[...] = _rms(x_ref[...], g_ref[...]).astype(BF16)

        @pl.when(seq_start)
        def _():
            zbuf_ref[:, 0:SUBLANES, :] = jnp.zeros((_N_CONV_TILES, SUBLANES, CONV_TILE), F32)

    def project():
        return _dot(n_ref[...], w_ref[...])

    @pl.when(j == _J_FOX_Q)
    def _():
        qk_ref[...] = (project() * FOX_Q_SCALE).astype(BF16)
        logf = jax.nn.log_sigmoid(_dot(n_ref[...], wff_ref[...]) + fb_ref[...])
        lt = logf.T[0:BF16_ROWS, :]
        hi = lt.astype(BF16)
        r1 = lt - hi.astype(F32)
        mid = r1.astype(BF16)
        lo = (r1 - mid.astype(F32)).astype(BF16)
        src = lax.broadcasted_iota(jnp.int32, (tm, tm), 0)
        dst = lax.broadcasted_iota(jnp.int32, (tm, tm), 1)
        tri = jnp.where(src <= dst, 1.0, 0.0).astype(BF16)
        parts = _dot(jnp.concatenate([hi, mid, lo], axis=0), tri)
        prev = jnp.where(seq_start, 0.0, carry_ref[...])
        ct = (parts[0:BF16_ROWS] + parts[BF16_ROWS:2 * BF16_ROWS] + parts[2 * BF16_ROWS:]) + prev
        carry_ref[...] = ct[:, tm - 1:tm]
        c_ref[...] = jnp.concatenate([ct, jnp.zeros((LANES - BF16_ROWS, tm), F32)], axis=0).T

    @pl.when(j == _J_FOX_K)
    def _():
        qk_ref[...] = project().astype(BF16)

    @pl.when(j == _J_FOX_V)
    def _():
        fvt_ref[...] = project().T.astype(BF16)

    @pl.when(j == _J_DIFF_V)
    def _():
        dvt_ref[...] = project().T.astype(BF16)

    for jj, q_scale in ((_J_DIFF_Q, DIFF_Q_SCALE), (_J_DIFF_K, None)):
        @pl.when(j == jj)
        def _(q_scale=q_scale):
            y = project()
            ct = ct_ref[...]
            st = st_ref[...]
            for c0 in range(0, IN_TILE, LANES):
                yc = y[:, c0:c0 + LANES]
                out = yc * ct + pltpu.roll(yc, LANES // 2, axis=1) * st
                if q_scale is not None:
                    out = out * q_scale
                qk_ref[:, c0:c0 + LANES] = out.astype(BF16)

    for t in range(_N_CONV_TILES):
        @pl.when(j == _J_CONV + t)
        def _(t=t):
            y = project()
            gb = y[:, 0:CONV_TILE]
            z = y[:, CONV_TILE:2 * CONV_TILE] * y[:, 2 * CONV_TILE:3 * CONV_TILE]
            zb = zbuf_ref.at[t]
            zb[SUBLANES:SUBLANES + tm, :] = z
            z1 = zb[SUBLANES - 1:SUBLANES - 1 + tm, :]
            z2 = zb[SUBLANES - 2:SUBLANES - 2 + tm, :]
            cw = cw_ref[...]
            conv = z2 * cw[0:1, :] + z1 * cw[1:2, :] + z * cw[2:3, :] + cb_ref[...]
            conv_ref[...] = (gb * conv).astype(BF16)
            zb[0:SUBLANES, :] = z[tm - SUBLANES:tm, :]


def _inproj(h, g, w, wff, fbias, ctab, stab, conv_w, conv_b, l, *, seq_len):
    n_tok, d = h.shape
    tm = IN_TOKEN_TILE
    conv_col = lambda j: jnp.clip(j - _J_CONV, 0, _N_CONV_TILES - 1)
    qk_col = lambda j: jnp.minimum(j - (j >= _J_FOX_V).astype(jnp.int32)
                                   - (j >= _J_DIFF_V).astype(jnp.int32), 3)
    return pl.pallas_call(
        functools.partial(_inproj_body, tiles_per_seq=seq_len // tm),
        grid=(n_tok // tm, _N_IN_TILES),
        in_specs=[
            pl.BlockSpec((tm, d), lambda i, j: (i, 0)),
            _layer_spec((1, d), lambda i, j: (l, 0, 0)),
            _layer_spec((d, IN_TILE), lambda i, j: (l, 0, j)),
            _layer_spec((d, LANES), lambda i, j: (l, 0, 0)),
            _layer_spec((1, LANES), lambda i, j: (l, 0, 0)),
            pl.BlockSpec((tm, LANES), lambda i, j: (i, 0)),
            pl.BlockSpec((tm, LANES), lambda i, j: (i, 0)),
            _layer_spec((CONV_WIDTH, CONV_TILE), lambda i, j: (l, 0, conv_col(j))),
            _layer_spec((1, CONV_TILE), lambda i, j: (l, 0, conv_col(j))),
        ],
        out_specs=[
            pl.BlockSpec((tm, IN_TILE), lambda i, j: (i, qk_col(j))),
            pl.BlockSpec((tm, CONV_TILE), lambda i, j: (i, conv_col(j))),
            pl.BlockSpec((tm, LANES), lambda i, j: (i, 0)),
            pl.BlockSpec((IN_TILE, tm), lambda i, j: (0, i)),
            pl.BlockSpec((IN_TILE, tm), lambda i, j: (0, i)),
        ],
        out_shape=[
            jax.ShapeDtypeStruct((n_tok, 4 * IN_TILE), BF16),
            jax.ShapeDtypeStruct((n_tok, CONV_CH), BF16),
            jax.ShapeDtypeStruct((n_tok, LANES), F32),
            jax.ShapeDtypeStruct((IN_TILE, n_tok), BF16),
            jax.ShapeDtypeStruct((IN_TILE, n_tok), BF16),
        ],
        scratch_shapes=[
            pltpu.VMEM((tm, d), BF16),
            pltpu.VMEM((_N_CONV_TILES, tm + SUBLANES, CONV_TILE), F32),
            pltpu.VMEM((BF16_ROWS, 1), F32),
        ],
        compiler_params=_params(2),
        name="inproj",
    )(h, g, w, wff, fbias, ctab, stab, conv_w, conv_b)


_T_QI, _T_KB, _T_DIAG, _T_STATE = 0, 1, 2, 3
PIPE_LAG = 2
PIPE_SLOTS = 2 * PIPE_LAG


def _item_table(nq):
    items = [(qi, kb) for qi in range(nq) for kb in range(qi + 1)]
    n_loops = -(-(len(items) + PIPE_LAG) // PIPE_SLOTS)
    n_pos = PIPE_SLOTS * n_loops + 2 * PIPE_LAG
    tab = np.zeros((4, n_pos), np.int32)
    for pos in range(n_pos):
        item = pos - PIPE_LAG
        qi, kb = items[min(max(item, 0), len(items) - 1)]
        real = 0 <= item < len(items)
        tab[:, pos] = (qi, kb, int(real and kb == qi), qi if real else nq)
    return tab, n_loops


def _fill_bias(bias_ref, t):
    key = lax.broadcasted_iota(jnp.int32, (t, t), 0)
    qry = lax.broadcasted_iota(jnp.int32, (t, t), 1)
    bias_ref[0] = jnp.zeros((t, t), F32)
    bias_ref[1] = jnp.where(key <= qry, 0.0, NEG_INF)


def _stage_values(vt_ref, vte_ref, t):
    hd = vt_ref.shape[0]
    for kb in range(vte_ref.shape[0]):
        vte_ref[kb, 0:hd, :] = vt_ref[:, kb * t:(kb + 1) * t]
        vte_ref[kb, hd:, :] = jnp.ones((ONES_ROWS, t), BF16)


def _store_scores(u, u_ref, mx_ref):
    u_ref[...] = u
    mx_ref[...] = jnp.max(u, axis=0, keepdims=True)


def _softmax_stage(u_ref, mx_ref, shift, m_ref, p_ref, al_ref):
    m_old = m_ref[...]
    m_new = jnp.maximum(m_old, mx_ref[...] + shift)
    al_ref[...] = jnp.exp2(m_old - m_new)
    p_ref[...] = jnp.exp2(u_ref[...] - (m_new - shift)).astype(BF16)
    m_ref[...] = m_new


def _value_stage(vt, p_ref, al_ref, acc_ref):
    acc_ref[...] = al_ref[...] * acc_ref[...] + _dot(vt, p_ref[...])


def _fox_body(tab_ref, q_ref, k_ref, vt_ref, c_ref, crow_ref, o_ref,
              ckb_ref, vte_ref, bias_ref, u_ref, mx_ref, p_ref, al_ref, m_ref, acc_ref, *, n_loops):
    b = pl.program_id(0)
    h = pl.program_id(1)
    t = ATT_TILE
    hd = FOX_HEAD_DIM

    @pl.when((b == 0) & (h == 0))
    def _():
        _fill_bias(bias_ref, t)

    lane = lax.broadcasted_iota(jnp.int32, c_ref.shape, 1)
    ck = jnp.sum(jnp.where(lane == h, c_ref[...], 0.0), axis=-1, keepdims=True)
    ckb_ref[...] = jnp.broadcast_to(ck * LOG2E, ckb_ref.shape)
    _stage_values(vt_ref, vte_ref, t)
    p_ref[PIPE_LAG:] = jnp.zeros((PIPE_LAG, t, t), BF16)
    al_ref[PIPE_LAG:] = jnp.zeros((PIPE_LAG, 1, t), F32)
    acc_ref[...] = jnp.zeros_like(acc_ref)
    m_ref[...] = jnp.full_like(m_ref, NEG_INF)

    def rows(idx):
        return pl.ds(pl.multiple_of(idx * t, t), t)

    def score_stage(pos, slot):
        kb = tab_ref[_T_KB, pos]
        u = (_dot_t(k_ref[rows(kb), :], q_ref[rows(tab_ref[_T_QI, pos]), :])
             - jnp.tile(ckb_ref[rows(kb), :], (1, t // LANES)) + bias_ref[tab_ref[_T_DIAG, pos]])
        _store_scores(u, u_ref.at[slot], mx_ref.at[slot])

    def step(tau, slot):
        lagged = (slot + PIPE_LAG) % PIPE_SLOTS
        score_stage(tau + 2 * PIPE_LAG, lagged)
        pos = tau + PIPE_LAG
        cq = crow_ref[0, tab_ref[_T_QI, pos]] * LOG2E
        _softmax_stage(u_ref.at[slot], mx_ref.at[slot], cq, m_ref.at[tab_ref[_T_STATE, pos]],
                       p_ref.at[slot], al_ref.at[slot])
        _value_stage(vte_ref[tab_ref[_T_KB, tau]], p_ref.at[lagged], al_ref.at[lagged],
                     acc_ref.at[tab_ref[_T_STATE, tau]])

    for item in range(PIPE_LAG):
        score_stage(item + PIPE_LAG, item)

    def unrolled(it, carry):
        for slot in range(PIPE_SLOTS):
            step(PIPE_SLOTS * it + slot, slot)
        return carry

    lax.fori_loop(0, n_loops, unrolled, 0)

    for qi in range(acc_ref.shape[0] - 1):
        acc = acc_ref[qi]
        o_ref[qi * t:(qi + 1) * t, :] = (acc[:hd, :] / acc[hd:hd + 1, :]).T.astype(BF16)


def _attention_scratch(t, nq, rows, n_streams):
    shape = lambda *s: ((n_streams,) if n_streams > 1 else ()) + s
    return [
        pltpu.VMEM((PIPE_SLOTS,) + shape(t, t), F32),
        pltpu.VMEM((PIPE_SLOTS,) + shape(1, t), F32),
        pltpu.VMEM((PIPE_SLOTS,) + shape(t, t), BF16),
        pltpu.VMEM((PIPE_SLOTS,) + shape(1, t), F32),
        pltpu.VMEM(shape(nq + 1, 1, t), F32),
        pltpu.VMEM(shape(nq + 1, rows, t), F32),
    ]


def _fox_attention(qk, vt, c, c_rows, *, batch, seq_len):
    t = ATT_TILE
    nq = seq_len // t
    hd = FOX_HEAD_DIM
    rows = hd + ONES_ROWS
    tab, n_loops = _item_table(nq)
    grid_spec = pltpu.PrefetchScalarGridSpec(
        num_scalar_prefetch=1,
        grid=(batch, FOX_HEADS),
        in_specs=[
            pl.BlockSpec((seq_len, hd), lambda b, h, tab: (b, h)),
            pl.BlockSpec((seq_len, hd), lambda b, h, tab: (b, FOX_HEADS + h)),
            pl.BlockSpec((hd, seq_len), lambda b, h, tab: (h, b)),
            pl.BlockSpec((seq_len, LANES), lambda b, h, tab: (b, 0)),
            pl.BlockSpec((1, nq, 1, t), lambda b, h, tab: (b * FOX_HEADS + h, 0, 0, 0)),
        ],
        out_specs=pl.BlockSpec((seq_len, hd), lambda b, h, tab: (b, h)),
        scratch_shapes=[pltpu.VMEM((seq_len, LANES), F32),
                        pltpu.VMEM((nq, rows, t), BF16),
                        pltpu.VMEM((2, t, t), F32)] + _attention_scratch(t, nq, rows, 1),
    )
    return pl.pallas_call(
        functools.partial(_fox_body, n_loops=n_loops),
        grid_spec=grid_spec,
        out_shape=jax.ShapeDtypeStruct((batch * seq_len, FOX_HEADS * hd), BF16),
        compiler_params=_params(2),
        name="fox_attention",
    )(jnp.asarray(tab), qk, qk, vt, c, c_rows)


def _diff_body(tab_ref, q_ref, k_ref, vt_ref, lam_ref, sg_ref, o_ref,
               vte_ref, bias_ref, u_ref, mx_ref, p_ref, al_ref, m_ref, acc_ref, *, n_loops, lam_init):
    b = pl.program_id(0)
    h = pl.program_id(1)
    t = ATT_TILE
    hd = DIFF_V_DIM

    @pl.when((b == 0) & (h == 0))
    def _():
        _fill_bias(bias_ref, t)

    _stage_values(vt_ref, vte_ref, t)
    p_ref[PIPE_LAG:] = jnp.zeros((PIPE_LAG, 2, t, t), BF16)
    al_ref[PIPE_LAG:] = jnp.zeros((PIPE_LAG, 2, 1, t), F32)
    acc_ref[...] = jnp.zeros_like(acc_ref)
    m_ref[...] = jnp.full_like(m_ref, NEG_INF)
    no_shift = jnp.zeros((1, t), F32)

    def rows(idx):
        return pl.ds(pl.multiple_of(idx * t, t), t)

    def score_stage(pos, slot):
        q = q_ref[rows(tab_ref[_T_QI, pos]), :]
        k = k_ref[rows(tab_ref[_T_KB, pos]), :]
        bias = bias_ref[tab_ref[_T_DIAG, pos]]
        lane = lax.broadcasted_iota(jnp.int32, q.shape, 1)
        comp0 = (lane < ROT_DIM // 2) | ((lane >= ROT_DIM) & (lane < LANES // 2 + ROT_DIM // 2))
        zero = jnp.zeros_like(q)
        for s, keep in enumerate((comp0, jnp.logical_not(comp0))):
            _store_scores(_dot_t(k, jnp.where(keep, q, zero)) + bias,
                          u_ref.at[slot, s], mx_ref.at[slot, s])

    def step(tau, slot):
        lagged = (slot + PIPE_LAG) % PIPE_SLOTS
        score_stage(tau + 2 * PIPE_LAG, lagged)
        state = tab_ref[_T_STATE, tau + PIPE_LAG]
        vt = vte_ref[tab_ref[_T_KB, tau]]
        for s in range(2):
            _softmax_stage(u_ref.at[slot, s], mx_ref.at[slot, s], no_shift, m_ref.at[s, state],
                           p_ref.at[slot, s], al_ref.at[slot, s])
            _value_stage(vt, p_ref.at[lagged, s], al_ref.at[lagged, s],
                         acc_ref.at[s, tab_ref[_T_STATE, tau]])

    for item in range(PIPE_LAG):
        score_stage(item + PIPE_LAG, item)

    def unrolled(it, carry):
        for slot in range(PIPE_SLOTS):
            step(PIPE_SLOTS * it + slot, slot)
        return carry

    lax.fori_loop(0, n_loops, unrolled, 0)

    lv = lam_ref[...]
    lam = (jnp.exp(jnp.sum(lv[0:1, :] * lv[1:2, :], axis=-1, keepdims=True))
           - jnp.exp(jnp.sum(lv[2:3, :] * lv[3:4, :], axis=-1, keepdims=True))
           + lam_init)
    for qi in range(acc_ref.shape[1] - 1):
        a1 = acc_ref[0, qi]
        a2 = acc_ref[1, qi]
        o = a1[:hd, :] / a1[hd:hd + 1, :] - lam * (a2[:hd, :] / a2[hd:hd + 1, :])
        o = o * lax.rsqrt(jnp.mean(o * o, axis=0, keepdims=True) + EPS)
        o_ref[qi * t:(qi + 1) * t, :] = (o.T * sg_ref[...] * (1.0 - lam_init)).astype(BF16)


def _diff_attention(qk, vt, lam_vecs, subln, l, *, batch, seq_len, lam_init):
    t = ATT_TILE
    nq = seq_len // t
    hd = DIFF_V_DIM
    rows = hd + ONES_ROWS
    base = 2 * FOX_HEADS
    tab, n_loops = _item_table(nq)
    grid_spec = pltpu.PrefetchScalarGridSpec(
        num_scalar_prefetch=1,
        grid=(batch, DIFF_HEADS),
        in_specs=[
            pl.BlockSpec((seq_len, hd), lambda b, h, tab: (b, base + h)),
            pl.BlockSpec((seq_len, hd), lambda b, h, tab: (b, base + DIFF_HEADS + h)),
            pl.BlockSpec((hd, seq_len), lambda b, h, tab: (h, b)),
            _layer_spec((4, DIFF_QK_DIM), lambda b, h, tab: (l, 0, 0)),
            _layer_spec((1, hd), lambda b, h, tab: (l, 0, 0)),
        ],
        out_specs=pl.BlockSpec((seq_len, hd), lambda b, h, tab: (b, h)),
        scratch_shapes=[pltpu.VMEM((nq, rows, t), BF16),
                        pltpu.VMEM((2, t, t), F32)] + _attention_scratch(t, nq, rows, 2),
    )
    return pl.pallas_call(
        functools.partial(_diff_body, n_loops=n_loops, lam_init=lam_init),
        grid_spec=grid_spec,
        out_shape=jax.ShapeDtypeStruct((batch * seq_len, DIFF_HEADS * hd), BF16),
        compiler_params=_params(2),
        name="diff_attention",
    )(jnp.asarray(tab), qk, qk, vt, lam_vecs, subln)


def _mixout_body(h_ref, fox_ref, diff_ref, conv_ref, wf_ref, wd_ref, wc_ref, o_ref):
    o_ref[...] = (h_ref[...] + _dot(fox_ref[...], wf_ref[...])
                  + _dot(diff_ref[...], wd_ref[...]) + _dot(conv_ref[...], wc_ref[...]))


def _mixout(h, fox, diff, conv, w_out, l):
    n_tok, d = h.shape
    tm = TOKEN_TILE
    fw, dw, cw = fox.shape[1], diff.shape[1], conv.shape[1]
    assert fw == dw and (fw + dw) % cw == 0
    row = lambda i: (i, 0)
    resident = lambda rows, blk: pl.BlockSpec((None, rows, d), lambda i: (l, blk, 0),
                                              pipeline_mode=pl.Buffered(1))
    return pl.pallas_call(
        _mixout_body,
        grid=(n_tok // tm,),
        in_specs=[
            pl.BlockSpec((tm, d), row),
            pl.BlockSpec((tm, fw), row),
            pl.BlockSpec((tm, dw), row),
            pl.BlockSpec((tm, cw), row),
            resident(fw, 0),
            resident(dw, 1),
            resident(cw, (fw + dw) // cw),
        ],
        out_specs=pl.BlockSpec((tm, d), row),
        out_shape=jax.ShapeDtypeStruct((n_tok, d), F32),
        compiler_params=_params(1),
        name="mixout",
    )(h, fox, diff, conv, w_out, w_out, w_out)


def _memkv_body(m_ref, g_ref, w_ref, o_ref):
    o_ref[...] = _dot(_rms(m_ref[...], g_ref[...]).astype(BF16), w_ref[...]).astype(BF16)


def _memkv(mem, g, w, l):
    n_mem, d = mem.shape
    tm = min(n_mem, TOKEN_TILE)
    width = w.shape[2]
    return pl.pallas_call(
        _memkv_body,
        grid=(n_mem // tm,),
        in_specs=[
            pl.BlockSpec((tm, d), lambda i: (i, 0)),
            _layer_spec((1, d), lambda i: (l, 0, 0)),
            pl.BlockSpec((None, d, width), lambda i: (l, 0, 0), pipeline_mode=pl.Buffered(1)),
        ],
        out_specs=pl.BlockSpec((tm, width), lambda i: (i, 0)),
        out_shape=jax.ShapeDtypeStruct((n_mem, width), BF16),
        compiler_params=_params(1),
        name="memkv",
    )(mem, g, w)


def _cross_body(h_ref, g_ref, wq_ref, kv_ref, wo_ref, o_ref):
    hd = CROSS_HEAD_DIM
    width = CROSS_HEADS * hd
    scale = hd ** -0.5
    x = h_ref[...]
    q = _dot(_rms(x, g_ref[...]).astype(BF16), wq_ref[...]).astype(BF16)
    heads = []
    for hh in range(CROSS_HEADS):
        k = kv_ref[:, hh * hd:(hh + 1) * hd]
        v = kv_ref[:, width + hh * hd:width + (hh + 1) * hd]
        s = _dot_t(q[:, hh * hd:(hh + 1) * hd], k) * scale
        e = jnp.exp(s - jnp.max(s, axis=-1, keepdims=True))
        p = e / jnp.sum(e, axis=-1, keepdims=True)
        heads.append(_dot(p.astype(BF16), v))
    o = jnp.concatenate(heads, axis=-1).astype(BF16)
    o_ref[...] = x + _dot(o, wo_ref[...])


def _cross(h, g, wq, kv, wo, l, *, seq_len, mem_len):
    n_tok, d = h.shape
    tm = TOKEN_TILE
    tiles_per_seq = seq_len // tm
    resident = lambda a: pl.BlockSpec((None,) + a.shape[1:], lambda i: (l, 0, 0),
                                      pipeline_mode=pl.Buffered(1))
    return pl.pallas_call(
        _cross_body,
        grid=(n_tok // tm,),
        in_specs=[
            pl.BlockSpec((tm, d), lambda i: (i, 0)),
            _layer_spec((1, d), lambda i: (l, 0, 0)),
            resident(wq),
            pl.BlockSpec((mem_len, kv.shape[1]), lambda i: (i // tiles_per_seq, 0)),
            resident(wo),
        ],
        out_specs=pl.BlockSpec((tm, d), lambda i: (i, 0)),
        out_shape=jax.ShapeDtypeStruct((n_tok, d), F32),
        compiler_params=_params(1),
        name="cross_attention",
    )(h, g, wq, kv, wo)


def _rope_tables(positions):
    half = ROT_DIM // 2
    inv_freq = ROPE_THETA ** (-jnp.arange(0, ROT_DIM, 2, dtype=F32) / ROT_DIM)
    ang = positions.astype(F32)[..., None] * inv_freq
    cos = jnp.cos(ang).reshape(-1, half)
    sin = jnp.sin(ang).reshape(-1, half)
    n = cos.shape[0]
    rest = LANES // 2 - ROT_DIM
    ones, zeros = jnp.ones((n, rest), F32), jnp.zeros((n, rest), F32)
    ct = jnp.concatenate([cos, cos, ones, cos, cos, ones], axis=-1)
    st = jnp.concatenate([-sin, -sin, zeros, sin, sin, zeros], axis=-1)
    return ct, st


def _diff_head_lanes():
    half = ROT_DIM // 2
    src = np.zeros(LANES, np.int64)
    comp0 = np.zeros(LANES, bool)
    for c in range(2):
        for d in range(DIFF_QK_DIM):
            if d < half:
                lane = c * half + d
            elif d < ROT_DIM:
                lane = LANES // 2 + c * half + (d - half)
            else:
                lane = (ROT_DIM if c == 0 else LANES // 2 + ROT_DIM) + (d - ROT_DIM)
            src[lane] = c * DIFF_QK_DIM + d
            comp0[lane] = c == 0
    return src, comp0


def _arrange_in_weights(w_in):
    fw, dw, cc = FOX_HEADS * FOX_HEAD_DIM, DIFF_HEADS * DIFF_V_DIM, CONV_CH
    o = 0
    fq, fk, fv = (w_in[..., o + k * fw:o + (k + 1) * fw] for k in range(3))
    o += 3 * fw
    ff = w_in[..., o:o + FOX_HEADS]
    o += FOX_HEADS
    dq, dk, dv = (w_in[..., o + k * dw:o + (k + 1) * dw] for k in range(3))
    o += 3 * dw
    gb, gc, hc = (w_in[..., o + k * cc:o + (k + 1) * cc] for k in range(3))
    src, _ = _diff_head_lanes()
    cols = np.concatenate([h * LANES + src for h in range(DIFF_HEADS)])
    dq, dk = dq[..., cols], dk[..., cols]
    parts = [fq, fk, fv, dq, dk, dv]
    for t in range(_N_CONV_TILES):
        sl = slice(t * CONV_TILE, (t + 1) * CONV_TILE)
        parts += [gb[..., sl], gc[..., sl], hc[..., sl]]
    w = jnp.concatenate(parts, axis=-1).astype(BF16)
    wff = jnp.pad(ff, ((0, 0), (0, 0), (0, LANES - FOX_HEADS))).astype(BF16)
    return w, wff


def kernel(x, mem, positions, ffn1_norm, ffn1_w_gate, ffn1_w_up, ffn1_w_down, mix_norm, mix_w_in, forget_bias, conv_w, conv_b, lambda_q1, lambda_k1, lambda_q2, lambda_k2, diff_subln, mix_w_out, cross_norm, mem_norm, cross_w_q, cross_w_kv, cross_w_o, ffn2_norm, ffn2_w_gate, ffn2_w_up, ffn2_w_down, final_norm):
    batch, seq_len, d = x.shape
    mem_len = mem.shape[1]
    depth = ffn1_norm.shape[0]
    n_tok = batch * seq_len
    assert seq_len % TOKEN_TILE == 0 and seq_len % ATT_TILE == 0
    assert ffn1_w_gate.shape[2] % FF_TILE == 0

    bf = lambda a: a.astype(BF16)
    rows3 = lambda a: a.astype(F32).reshape(a.shape[0], 1, a.shape[1])
    w1g, w1u, w1d = bf(ffn1_w_gate), bf(ffn1_w_up), bf(ffn1_w_down)
    w2g, w2u, w2d = bf(ffn2_w_gate), bf(ffn2_w_up), bf(ffn2_w_down)
    w_in, w_ff = _arrange_in_weights(mix_w_in)
    w_out = bf(mix_w_out)
    wq, wkv, wo = bf(cross_w_q), bf(cross_w_kv), bf(cross_w_o)
    g_ffn1, g_mix, g_cross, g_mem, g_ffn2 = (rows3(a) for a in (ffn1_norm, mix_norm, cross_norm,
                                                                 mem_norm, ffn2_norm))
    g_final = final_norm.astype(F32).reshape(1, 1, d)
    fbias = rows3(jnp.pad(forget_bias, ((0, 0), (0, LANES - FOX_HEADS))))
    cbias = rows3(conv_b)
    subln = rows3(diff_subln)
    lam_vecs = jnp.stack([lambda_q1, lambda_k1, lambda_q2, lambda_k2], axis=1).astype(F32)
    ctab, stab = _rope_tables(positions)

    h = x.reshape(n_tok, d)
    mem2 = mem.reshape(batch * mem_len, d)
    nq = seq_len // ATT_TILE
    for l in range(depth):
        h = _ffn(h, g_ffn1, w1g, w1u, w1d, g_final, l, final=False)

        qk, conv, c, fox_vt, diff_vt = _inproj(h, g_mix, w_in, w_ff, fbias, ctab, stab,
                                               conv_w.astype(F32), cbias, l, seq_len=seq_len)
        c_rows = (c.reshape(batch, seq_len, LANES)[:, :, :FOX_HEADS]
                  .transpose(0, 2, 1).reshape(batch * FOX_HEADS, nq, 1, ATT_TILE))
        fox = _fox_attention(qk, fox_vt, c, c_rows, batch=batch, seq_len=seq_len)
        lam_init = 0.8 - 0.6 * math.exp(-0.3 * l)
        diff = _diff_attention(qk, diff_vt, lam_vecs, subln, l, batch=batch, seq_len=seq_len,
                               lam_init=lam_init)
        h = _mixout(h, fox, diff, conv, w_out, l)

        kv = _memkv(mem2, g_mem, wkv, l)
        h = _cross(h, g_cross, wq, kv, wo, l, seq_len=seq_len, mem_len=mem_len)

        h = _ffn(h, g_ffn2, w2g, w2u, w2d, g_final, l, final=(l == depth - 1))
    return h.reshape(batch, seq_len, d)
```

```python
import functools
import math

import numpy as np
import jax
import jax.numpy as jnp
from jax import lax
from jax.experimental import pallas as pl
from jax.experimental.pallas import tpu as pltpu

F32 = jnp.float32
BF16 = jnp.bfloat16

FOX_HEADS = 6
FOX_HEAD_DIM = 128
DIFF_HEADS = 6
DIFF_QK_DIM = 64
DIFF_V_DIM = 128
CONV_CH = 512
CONV_WIDTH = 3
ROT_DIM = 16
ROPE_THETA = 500000.0
CROSS_HEADS = 4
CROSS_HEAD_DIM = 128
EPS = 1e-6
NEG_INF = -1e30
LOG2E = math.log2(math.e)
FOX_Q_SCALE = FOX_HEAD_DIM ** -0.5 * LOG2E
DIFF_Q_SCALE = DIFF_QK_DIM ** -0.5 * LOG2E

LANES = 128
SUBLANES = 8
BF16_ROWS = 16

TOKEN_TILE = 1024
FF_TILE = 512
IN_TILE = 768
CONV_TILE = 256
ATT_TILE = 512
ONES_ROWS = BF16_ROWS
VMEM_LIMIT = 56 * 1024 * 1024
FFN_VMEM_LIMIT = 60 * 1024 * 1024

_ARB = "arbitrary"


def _params(n_axes, vmem_limit=VMEM_LIMIT):
    return pltpu.CompilerParams(dimension_semantics=(_ARB,) * n_axes,
                                vmem_limit_bytes=vmem_limit)


def _rms(x, g):
    return x * lax.rsqrt(jnp.mean(x * x, axis=-1, keepdims=True) + EPS) * g


def _dot(a, b):
    return jnp.dot(a, b, preferred_element_type=F32)


def _dot_t(a, b):
    return lax.dot_general(a, b, (((1,), (1,)), ((), ())), preferred_element_type=F32)


def _layer_spec(shape, index_map):
    return pl.BlockSpec((None,) + tuple(shape), index_map)


def _ffn_body(x_ref, g_ref, wg_ref, wu_ref, wd_ref, fg_ref, o_ref, n_ref, *, final):
    j = pl.program_id(1)

    @pl.when(j == 0)
    def _():
        x = x_ref[...]
        n_ref[...] = _rms(x, g_ref[...]).astype(BF16)
        o_ref[...] = x

    n = n_ref[...]
    gate = _dot(n, wg_ref[...])
    up = _dot(n, wu_ref[...])
    act = (gate * jax.nn.sigmoid(gate) * (0.5 * up)).astype(BF16)
    o_ref[...] += _dot(act, wd_ref[...])

    if final:
        @pl.when(j == pl.num_programs(1) - 1)
        def _():
            o_ref[...] = _rms(o_ref[...], fg_ref[...])


def _ffn(h, g, wg, wu, wd, final_g, l, *, final):
    n_tok, d = h.shape
    d_ff = wg.shape[2]
    tm, tf = TOKEN_TILE, FF_TILE
    return pl.pallas_call(
        functools.partial(_ffn_body, final=final),
        grid=(n_tok // tm, d_ff // tf),
        in_specs=[
            pl.BlockSpec((tm, d), lambda i, j: (i, 0)),
            _layer_spec((1, d), lambda i, j: (l, 0, 0)),
            _layer_spec((d, tf), lambda i, j: (l, 0, j)),
            _layer_spec((d, tf), lambda i, j: (l, 0, j)),
            _layer_spec((tf, d), lambda i, j: (l, j, 0)),
            _layer_spec((1, d), lambda i, j: (0, 0, 0)),
        ],
        out_specs=pl.BlockSpec((tm, d), lambda i, j: (i, 0)),
        out_shape=jax.ShapeDtypeStruct((n_tok, d), F32),
        scratch_shapes=[pltpu.VMEM((tm, d), BF16)],
        compiler_params=_params(2, FFN_VMEM_LIMIT),
        name="ffn_final" if final else "ffn",
    )(h, g, wg, wu, wd, final_g)


_J_FOX_Q, _J_FOX_K, _J_FOX_V, _J_DIFF_Q, _J_DIFF_K, _J_DIFF_V, _J_CONV = 0, 1, 2, 3, 4, 5, 6
_N_CONV_TILES = CONV_CH // CONV_TILE
_N_IN_TILES = _J_CONV + _N_CONV_TILES


def _inproj_body(x_ref, g_ref, w_ref, wff_ref, fb_ref, ct_ref, st_ref, cw_ref, cb_ref,
                 qk_ref, conv_ref, c_ref, fvt_ref, dvt_ref, n_ref, zbuf_ref, carry_ref,
                 *, tiles_per_seq):
    i = pl.program_id(0)
    j = pl.program_id(1)
    tm = x_ref.shape[0]
    seq_start = (i % tiles_per_seq) == 0

    @pl.when(j == 0)
    def _():
        n_ref[...] = _rms(x_ref[...], g_ref[...]).astype(BF16)

        @pl.when(seq_start)
        def _():
            zbuf_ref[:, 0:SUBLANES, :] = jnp.zeros((_N_CONV_TILES, SUBLANES, CONV_TILE), F32)

    def project():
        return _dot(n_ref[...], w_ref[...])

    @pl.when(j == _J_FOX_Q)
    def _():
        qk_ref[...] = (project() * FOX_Q_SCALE).astype(BF16)
        logf = jax.nn.log_sigmoid(_dot(n_ref[...], wff_ref[...]) + fb_ref[...])
        lt = logf.T[0:BF16_ROWS, :]
        hi = lt.astype(BF16)
        r1 = lt - hi.astype(F32)
        mid = r1.astype(BF16)
        lo = (r1 - mid.astype(F32)).astype(BF16)
        src = lax.broadcasted_iota(jnp.int32, (tm, tm), 0)
        dst = lax.broadcasted_iota(jnp.int32, (tm, tm), 1)
        tri = jnp.where(src <= dst, 1.0, 0.0).astype(BF16)
        parts = _dot(jnp.concatenate([hi, mid, lo], axis=0), tri)
        prev = jnp.where(seq_start, 0.0, carry_ref[...])
        ct = (parts[0:BF16_ROWS] + parts[BF16_ROWS:2 * BF16_ROWS] + parts[2 * BF16_ROWS:]) + prev
        carry_ref[...] = ct[:, tm - 1:tm]
        c_ref[...] = jnp.concatenate([ct, jnp.zeros((LANES - BF16_ROWS, tm), F32)], axis=0).T

    @pl.when(j == _J_FOX_K)
    def _():
        qk_ref[...] = project().astype(BF16)

    @pl.when(j == _J_FOX_V)
    def _():
        fvt_ref[...] = project().T.astype(BF16)

    @pl.when(j == _J_DIFF_V)
    def _():
        dvt_ref[...] = project().T.astype(BF16)

    for jj, q_scale in ((_J_DIFF_Q, DIFF_Q_SCALE), (_J_DIFF_K, None)):
        @pl.when(j == jj)
        def _(q_scale=q_scale):
            y = project()
            ct = ct_ref[...]
            st = st_ref[...]
            for c0 in range(0, IN_TILE, LANES):
                yc = y[:, c0:c0 + LANES]
                out = yc * ct + pltpu.roll(yc, LANES // 2, axis=1) * st
                if q_scale is not None:
                    out = out * q_scale
                qk_ref[:, c0:c0 + LANES] = out.astype(BF16)

    for t in range(_N_CONV_TILES):
        @pl.when(j == _J_CONV + t)
        def _(t=t):
            y = project()
            gb = y[:, 0:CONV_TILE]
            z = y[:, CONV_TILE:2 * CONV_TILE] * y[:, 2 * CONV_TILE:3 * CONV_TILE]
            zb = zbuf_ref.at[t]
            zb[SUBLANES:SUBLANES + tm, :] = z
            z1 = zb[SUBLANES - 1:SUBLANES - 1 + tm, :]
            z2 = zb[SUBLANES - 2:SUBLANES - 2 + tm, :]
            cw = cw_ref[...]
            conv = z2 * cw[0:1, :] + z1 * cw[1:2, :] + z * cw[2:3, :] + cb_ref[...]
            conv_ref[...] = (gb * conv).astype(BF16)
            zb[0:SUBLANES, :] = z[tm - SUBLANES:tm, :]


def _inproj(h, g, w, wff, fbias, ctab, stab, conv_w, conv_b, l, *, seq_len):
    n_tok, d = h.shape
    tm = TOKEN_TILE
    conv_col = lambda j: jnp.clip(j - _J_CONV, 0, _N_CONV_TILES - 1)
    qk_col = lambda j: jnp.minimum(j - (j >= _J_FOX_V).astype(jnp.int32)
                                   - (j >= _J_DIFF_V).astype(jnp.int32), 3)
    return pl.pallas_call(
        functools.partial(_inproj_body, tiles_per_seq=seq_len // tm),
        grid=(n_tok // tm, _N_IN_TILES),
        in_specs=[
            pl.BlockSpec((tm, d), lambda i, j: (i, 0)),
            _layer_spec((1, d), lambda i, j: (l, 0, 0)),
            _layer_spec((d, IN_TILE), lambda i, j: (l, 0, j)),
            _layer_spec((d, LANES), lambda i, j: (l, 0, 0)),
            _layer_spec((1, LANES), lambda i, j: (l, 0, 0)),
            pl.BlockSpec((tm, LANES), lambda i, j: (i, 0)),
            pl.BlockSpec((tm, LANES), lambda i, j: (i, 0)),
            _layer_spec((CONV_WIDTH, CONV_TILE), lambda i, j: (l, 0, conv_col(j))),
            _layer_spec((1, CONV_TILE), lambda i, j: (l, 0, conv_col(j))),
        ],
        out_specs=[
            pl.BlockSpec((tm, IN_TILE), lambda i, j: (i, qk_col(j))),
            pl.BlockSpec((tm, CONV_TILE), lambda i, j: (i, conv_col(j))),
            pl.BlockSpec((tm, LANES), lambda i, j: (i, 0)),
            pl.BlockSpec((IN_TILE, tm), lambda i, j: (0, i)),
            pl.BlockSpec((IN_TILE, tm), lambda i, j: (0, i)),
        ],
        out_shape=[
            jax.ShapeDtypeStruct((n_tok, 4 * IN_TILE), BF16),
            jax.ShapeDtypeStruct((n_tok, CONV_CH), BF16),
            jax.ShapeDtypeStruct((n_tok, LANES), F32),
            jax.ShapeDtypeStruct((IN_TILE, n_tok), BF16),
            jax.ShapeDtypeStruct((IN_TILE, n_tok), BF16),
        ],
        scratch_shapes=[
            pltpu.VMEM((tm, d), BF16),
            pltpu.VMEM((_N_CONV_TILES, tm + SUBLANES, CONV_TILE), F32),
            pltpu.VMEM((BF16_ROWS, 1), F32),
        ],
        compiler_params=_params(2),
        name="inproj",
    )(h, g, w, wff, fbias, ctab, stab, conv_w, conv_b)


_T_QI, _T_KB, _T_DIAG, _T_STATE = 0, 1, 2, 3
PIPE_LAG = 2
PIPE_SLOTS = 2 * PIPE_LAG


def _item_table(nq):
    items = [(qi, kb) for qi in range(nq) for kb in range(qi + 1)]
    n_loops = -(-len(items) // PIPE_SLOTS)
    n_pos = PIPE_SLOTS * n_loops + 2 * PIPE_LAG
    tab = np.zeros((4, n_pos), np.int32)
    for pos in range(n_pos):
        item = pos - PIPE_LAG
        qi, kb = items[min(max(item, 0), len(items) - 1)]
        real = 0 <= item < len(items)
        tab[:, pos] = (qi, kb, int(real and kb == qi), qi if real else nq)
    return tab, n_loops


def _fill_bias(bias_ref, t):
    key = lax.broadcasted_iota(jnp.int32, (t, t), 0)
    qry = lax.broadcasted_iota(jnp.int32, (t, t), 1)
    bias_ref[0] = jnp.zeros((t, t), F32)
    bias_ref[1] = jnp.where(key <= qry, 0.0, NEG_INF)


def _stage_values(vt_ref, vte_ref, t):
    hd = vt_ref.shape[0]
    for kb in range(vte_ref.shape[0]):
        vte_ref[kb, 0:hd, :] = vt_ref[:, kb * t:(kb + 1) * t]
        vte_ref[kb, hd:, :] = jnp.ones((ONES_ROWS, t), BF16)


def _store_scores(u, u_ref, mx_ref):
    u_ref[...] = u
    mx_ref[...] = jnp.max(u, axis=0, keepdims=True)


def _softmax_stage(u_ref, mx_ref, shift, m_ref, p_ref, al_ref):
    m_old = m_ref[...]
    m_new = jnp.maximum(m_old, mx_ref[...] + shift)
    al_ref[...] = jnp.exp2(m_old - m_new)
    p_ref[...] = jnp.exp2(u_ref[...] - (m_new - shift)).astype(BF16)
    m_ref[...] = m_new


def _value_stage(vt, p_ref, al_ref, acc_ref):
    acc_ref[...] = al_ref[...] * acc_ref[...] + _dot(vt, p_ref[...])


def _fox_body(tab_ref, q_ref, k_ref, vt_ref, c_ref, crow_ref, o_ref,
              ckb_ref, vte_ref, bias_ref, u_ref, mx_ref, p_ref, al_ref, m_ref, acc_ref, *, n_loops):
    b = pl.program_id(0)
    h = pl.program_id(1)
    t = ATT_TILE
    hd = FOX_HEAD_DIM

    @pl.when((b == 0) & (h == 0))
    def _():
        _fill_bias(bias_ref, t)

    lane = lax.broadcasted_iota(jnp.int32, c_ref.shape, 1)
    ck = jnp.sum(jnp.where(lane == h, c_ref[...], 0.0), axis=-1, keepdims=True)
    ckb_ref[...] = jnp.broadcast_to(ck * LOG2E, ckb_ref.shape)
    _stage_values(vt_ref, vte_ref, t)
    p_ref[PIPE_LAG:] = jnp.zeros((PIPE_LAG, t, t), BF16)
    al_ref[PIPE_LAG:] = jnp.zeros((PIPE_LAG, 1, t), F32)
    acc_ref[...] = jnp.zeros_like(acc_ref)
    m_ref[...] = jnp.full_like(m_ref, NEG_INF)

    def rows(idx):
        return pl.ds(pl.multiple_of(idx * t, t), t)

    def score_stage(pos, slot):
        kb = tab_ref[_T_KB, pos]
        u = (_dot_t(k_ref[rows(kb), :], q_ref[rows(tab_ref[_T_QI, pos]), :])
             - jnp.tile(ckb_ref[rows(kb), :], (1, t // LANES)) + bias_ref[tab_ref[_T_DIAG, pos]])
        _store_scores(u, u_ref.at[slot], mx_ref.at[slot])

    def step(tau, slot):
        lagged = (slot + PIPE_LAG) % PIPE_SLOTS
        score_stage(tau + 2 * PIPE_LAG, lagged)
        pos = tau + PIPE_LAG
        cq = crow_ref[0, tab_ref[_T_QI, pos]] * LOG2E
        _softmax_stage(u_ref.at[slot], mx_ref.at[slot], cq, m_ref.at[tab_ref[_T_STATE, pos]],
                       p_ref.at[slot], al_ref.at[slot])
        value_stage(tau, lagged)

    def value_stage(tau, lagged):
        _value_stage(vte_ref[tab_ref[_T_KB, tau]], p_ref.at[lagged], al_ref.at[lagged],
                     acc_ref.at[tab_ref[_T_STATE, tau]])

    for item in range(PIPE_LAG):
        score_stage(item + PIPE_LAG, item)

    def unrolled(it, carry):
        for slot in range(PIPE_SLOTS):
            step(PIPE_SLOTS * it + slot, slot)
        return carry

    lax.fori_loop(0, n_loops, unrolled, 0)
    for tau in range(PIPE_SLOTS * n_loops, PIPE_SLOTS * n_loops + PIPE_LAG):
        value_stage(tau, (tau + PIPE_LAG) % PIPE_SLOTS)

    for qi in range(acc_ref.shape[0] - 1):
        acc = acc_ref[qi]
        o_ref[qi * t:(qi + 1) * t, :] = (acc[:hd, :] / acc[hd:hd + 1, :]).T.astype(BF16)


def _attention_scratch(t, nq, rows, n_streams):
    shape = lambda *s: ((n_streams,) if n_streams > 1 else ()) + s
    return [
        pltpu.VMEM((PIPE_SLOTS,) + shape(t, t), F32),
        pltpu.VMEM((PIPE_SLOTS,) + shape(1, t), F32),
        pltpu.VMEM((PIPE_SLOTS,) + shape(t, t), BF16),
        pltpu.VMEM((PIPE_SLOTS,) + shape(1, t), F32),
        pltpu.VMEM(shape(nq + 1, 1, t), F32),
        pltpu.VMEM(shape(nq + 1, rows, t), F32),
    ]


def _fox_attention(qk, vt, c, c_rows, *, batch, seq_len):
    t = ATT_TILE
    nq = seq_len // t
    hd = FOX_HEAD_DIM
    rows = hd + ONES_ROWS
    tab, n_loops = _item_table(nq)
    grid_spec = pltpu.PrefetchScalarGridSpec(
        num_scalar_prefetch=1,
        grid=(batch, FOX_HEADS),
        in_specs=[
            pl.BlockSpec((seq_len, hd), lambda b, h, tab: (b, h)),
            pl.BlockSpec((seq_len, hd), lambda b, h, tab: (b, FOX_HEADS + h)),
            pl.BlockSpec((hd, seq_len), lambda b, h, tab: (h, b)),
            pl.BlockSpec((seq_len, LANES), lambda b, h, tab: (b, 0)),
            pl.BlockSpec((1, nq, 1, t), lambda b, h, tab: (b * FOX_HEADS + h, 0, 0, 0)),
        ],
        out_specs=pl.BlockSpec((seq_len, hd), lambda b, h, tab: (b, h)),
        scratch_shapes=[pltpu.VMEM((seq_len, LANES), F32),
                        pltpu.VMEM((nq, rows, t), BF16),
                        pltpu.VMEM((2, t, t), F32)] + _attention_scratch(t, nq, rows, 1),
    )
    return pl.pallas_call(
        functools.partial(_fox_body, n_loops=n_loops),
        grid_spec=grid_spec,
        out_shape=jax.ShapeDtypeStruct((batch * seq_len, FOX_HEADS * hd), BF16),
        compiler_params=_params(2),
        name="fox_attention",
    )(jnp.asarray(tab), qk, qk, vt, c, c_rows)


def _diff_body(tab_ref, q_ref, k_ref, vt_ref, lam_ref, sg_ref, o_ref,
               vte_ref, bias_ref, u_ref, mx_ref, p_ref, al_ref, m_ref, acc_ref, *, n_loops, lam_init):
    b = pl.program_id(0)
    h = pl.program_id(1)
    t = ATT_TILE
    hd = DIFF_V_DIM

    @pl.when((b == 0) & (h == 0))
    def _():
        _fill_bias(bias_ref, t)

    _stage_values(vt_ref, vte_ref, t)
    p_ref[PIPE_LAG:] = jnp.zeros((PIPE_LAG, 2, t, t), BF16)
    al_ref[PIPE_LAG:] = jnp.zeros((PIPE_LAG, 2, 1, t), F32)
    acc_ref[...] = jnp.zeros_like(acc_ref)
    m_ref[...] = jnp.full_like(m_ref, NEG_INF)
    no_shift = jnp.zeros((1, t), F32)

    def rows(idx):
        return pl.ds(pl.multiple_of(idx * t, t), t)

    def score_stage(pos, slot):
        q = q_ref[rows(tab_ref[_T_QI, pos]), :]
        k = k_ref[rows(tab_ref[_T_KB, pos]), :]
        bias = bias_ref[tab_ref[_T_DIAG, pos]]
        lane = lax.broadcasted_iota(jnp.int32, q.shape, 1)
        comp0 = (lane < ROT_DIM // 2) | ((lane >= ROT_DIM) & (lane < LANES // 2 + ROT_DIM // 2))
        zero = jnp.zeros_like(q)
        for s, keep in enumerate((comp0, jnp.logical_not(comp0))):
            _store_scores(_dot_t(k, jnp.where(keep, q, zero)) + bias,
                          u_ref.at[slot, s], mx_ref.at[slot, s])

    def step(tau, slot):
        lagged = (slot + PIPE_LAG) % PIPE_SLOTS
        score_stage(tau + 2 * PIPE_LAG, lagged)
        state = tab_ref[_T_STATE, tau + PIPE_LAG]
        for s in range(2):
            _softmax_stage(u_ref.at[slot, s], mx_ref.at[slot, s], no_shift, m_ref.at[s, state],
                           p_ref.at[slot, s], al_ref.at[slot, s])
        value_stage(tau, lagged)

    def value_stage(tau, lagged):
        vt = vte_ref[tab_ref[_T_KB, tau]]
        for s in range(2):
            _value_stage(vt, p_ref.at[lagged, s], al_ref.at[lagged, s],
                         acc_ref.at[s, tab_ref[_T_STATE, tau]])

    for item in range(PIPE_LAG):
        score_stage(item + PIPE_LAG, item)

    def unrolled(it, carry):
        for slot in range(PIPE_SLOTS):
            step(PIPE_SLOTS * it + slot, slot)
        return carry

    lax.fori_loop(0, n_loops, unrolled, 0)
    for tau in range(PIPE_SLOTS * n_loops, PIPE_SLOTS * n_loops + PIPE_LAG):
        value_stage(tau, (tau + PIPE_LAG) % PIPE_SLOTS)

    lv = lam_ref[...]
    lam = (jnp.exp(jnp.sum(lv[0:1, :] * lv[1:2, :], axis=-1, keepdims=True))
           - jnp.exp(jnp.sum(lv[2:3, :] * lv[3:4, :], axis=-1, keepdims=True))
           + lam_init)
    for qi in range(acc_ref.shape[1] - 1):
        a1 = acc_ref[0, qi]
        a2 = acc_ref[1, qi]
        o = a1[:hd, :] / a1[hd:hd + 1, :] - lam * (a2[:hd, :] / a2[hd:hd + 1, :])
        o = o * lax.rsqrt(jnp.mean(o * o, axis=0, keepdims=True) + EPS)
        o_ref[qi * t:(qi + 1) * t, :] = (o.T * sg_ref[...] * (1.0 - lam_init)).astype(BF16)


def _diff_attention(qk, vt, lam_vecs, subln, l, *, batch, seq_len, lam_init):
    t = ATT_TILE
    nq = seq_len // t
    hd = DIFF_V_DIM
    rows = hd + ONES_ROWS
    base = 2 * FOX_HEADS
    tab, n_loops = _item_table(nq)
    grid_spec = pltpu.PrefetchScalarGridSpec(
        num_scalar_prefetch=1,
        grid=(batch, DIFF_HEADS),
        in_specs=[
            pl.BlockSpec((seq_len, hd), lambda b, h, tab: (b, base + h)),
            pl.BlockSpec((seq_len, hd), lambda b, h, tab: (b, base + DIFF_HEADS + h)),
            pl.BlockSpec((hd, seq_len), lambda b, h, tab: (h, b)),
            _layer_spec((4, DIFF_QK_DIM), lambda b, h, tab: (l, 0, 0)),
            _layer_spec((1, hd), lambda b, h, tab: (l, 0, 0)),
        ],
        out_specs=pl.BlockSpec((seq_len, hd), lambda b, h, tab: (b, h)),
        scratch_shapes=[pltpu.VMEM((nq, rows, t), BF16),
                        pltpu.VMEM((2, t, t), F32)] + _attention_scratch(t, nq, rows, 2),
    )
    return pl.pallas_call(
        functools.partial(_diff_body, n_loops=n_loops, lam_init=lam_init),
        grid_spec=grid_spec,
        out_shape=jax.ShapeDtypeStruct((batch * seq_len, DIFF_HEADS * hd), BF16),
        compiler_params=_params(2),
        name="diff_attention",
    )(jnp.asarray(tab), qk, qk, vt, lam_vecs, subln)


def _mixout_body(h_ref, fox_ref, diff_ref, conv_ref, wf_ref, wd_ref, wc_ref, o_ref):
    o_ref[...] = (h_ref[...] + _dot(fox_ref[...], wf_ref[...])
                  + _dot(diff_ref[...], wd_ref[...]) + _dot(conv_ref[...], wc_ref[...]))


def _mixout(h, fox, diff, conv, w_out, l):
    n_tok, d = h.shape
    tm = TOKEN_TILE
    fw, dw, cw = fox.shape[1], diff.shape[1], conv.shape[1]
    assert fw == dw and (fw + dw) % cw == 0
    row = lambda i: (i, 0)
    resident = lambda rows, blk: pl.BlockSpec((None, rows, d), lambda i: (l, blk, 0),
                                              pipeline_mode=pl.Buffered(1))
    return pl.pallas_call(
        _mixout_body,
        grid=(n_tok // tm,),
        in_specs=[
            pl.BlockSpec((tm, d), row),
            pl.BlockSpec((tm, fw), row),
            pl.BlockSpec((tm, dw), row),
            pl.BlockSpec((tm, cw), row),
            resident(fw, 0),
            resident(dw, 1),
            resident(cw, (fw + dw) // cw),
        ],
        out_specs=pl.BlockSpec((tm, d), row),
        out_shape=jax.ShapeDtypeStruct((n_tok, d), F32),
        compiler_params=_params(1),
        name="mixout",
    )(h, fox, diff, conv, w_out, w_out, w_out)


def _memkv_body(m_ref, g_ref, w_ref, o_ref):
    o_ref[...] = _dot(_rms(m_ref[...], g_ref[...]).astype(BF16), w_ref[...]).astype(BF16)


def _memkv(mem, g, w, l):
    n_mem, d = mem.shape
    tm = min(n_mem, TOKEN_TILE)
    width = w.shape[2]
    return pl.pallas_call(
        _memkv_body,
        grid=(n_mem // tm,),
        in_specs=[
            pl.BlockSpec((tm, d), lambda i: (i, 0)),
            _layer_spec((1, d), lambda i: (l, 0, 0)),
            pl.BlockSpec((None, d, width), lambda i: (l, 0, 0), pipeline_mode=pl.Buffered(1)),
        ],
        out_specs=pl.BlockSpec((tm, width), lambda i: (i, 0)),
        out_shape=jax.ShapeDtypeStruct((n_mem, width), BF16),
        compiler_params=_params(1),
        name="memkv",
    )(mem, g, w)


def _cross_body(h_ref, g_ref, wq_ref, kv_ref, wo_ref, o_ref):
    hd = CROSS_HEAD_DIM
    width = CROSS_HEADS * hd
    scale = hd ** -0.5
    x = h_ref[...]
    q = _dot(_rms(x, g_ref[...]).astype(BF16), wq_ref[...]).astype(BF16)
    heads = []
    for hh in range(CROSS_HEADS):
        k = kv_ref[:, hh * hd:(hh + 1) * hd]
        v = kv_ref[:, width + hh * hd:width + (hh + 1) * hd]
        s = _dot_t(q[:, hh * hd:(hh + 1) * hd], k) * scale
        e = jnp.exp(s - jnp.max(s, axis=-1, keepdims=True))
        p = e / jnp.sum(e, axis=-1, keepdims=True)
        heads.append(_dot(p.astype(BF16), v))
    o = jnp.concatenate(heads, axis=-1).astype(BF16)
    o_ref[...] = x + _dot(o, wo_ref[...])


def _cross(h, g, wq, kv, wo, l, *, seq_len, mem_len):
    n_tok, d = h.shape
    tm = TOKEN_TILE
    tiles_per_seq = seq_len // tm
    resident = lambda a: pl.BlockSpec((None,) + a.shape[1:], lambda i: (l, 0, 0),
                                      pipeline_mode=pl.Buffered(1))
    return pl.pallas_call(
        _cross_body,
        grid=(n_tok // tm,),
        in_specs=[
            pl.BlockSpec((tm, d), lambda i: (i, 0)),
            _layer_spec((1, d), lambda i: (l, 0, 0)),
            resident(wq),
            pl.BlockSpec((mem_len, kv.shape[1]), lambda i: (i // tiles_per_seq, 0)),
            resident(wo),
        ],
        out_specs=pl.BlockSpec((tm, d), lambda i: (i, 0)),
        out_shape=jax.ShapeDtypeStruct((n_tok, d), F32),
        compiler_params=_params(1),
        name="cross_attention",
    )(h, g, wq, kv, wo)


def _rope_tables(positions):
    half = ROT_DIM // 2
    inv_freq = ROPE_THETA ** (-jnp.arange(0, ROT_DIM, 2, dtype=F32) / ROT_DIM)
    ang = positions.astype(F32)[..., None] * inv_freq
    cos = jnp.cos(ang).reshape(-1, half)
    sin = jnp.sin(ang).reshape(-1, half)
    lane = np.arange(LANES)
    rotated = (lane % (LANES // 2)) < ROT_DIM
    sign = np.where(lane < LANES // 2, -1.0, 1.0).astype(np.float32)
    reps = LANES // half
    ct = jnp.where(rotated[None, :], jnp.tile(cos, (1, reps)), 1.0)
    st = jnp.where(rotated[None, :], jnp.tile(sin, (1, reps)) * sign[None, :], 0.0)
    return ct, st


def _diff_head_lanes():
    half = ROT_DIM // 2
    src = np.zeros(LANES, np.int64)
    comp0 = np.zeros(LANES, bool)
    for c in range(2):
        for d in range(DIFF_QK_DIM):
            if d < half:
                lane = c * half + d
            elif d < ROT_DIM:
                lane = LANES // 2 + c * half + (d - half)
            else:
                lane = (ROT_DIM if c == 0 else LANES // 2 + ROT_DIM) + (d - ROT_DIM)
            src[lane] = c * DIFF_QK_DIM + d
            comp0[lane] = c == 0
    return src, comp0


def _arrange_in_weights(w_in):
    fw, dw, cc = FOX_HEADS * FOX_HEAD_DIM, DIFF_HEADS * DIFF_V_DIM, CONV_CH
    o = 0
    fq, fk, fv = (w_in[..., o + k * fw:o + (k + 1) * fw] for k in range(3))
    o += 3 * fw
    ff = w_in[..., o:o + FOX_HEADS]
    o += FOX_HEADS
    dq, dk, dv = (w_in[..., o + k * dw:o + (k + 1) * dw] for k in range(3))
    o += 3 * dw
    gb, gc, hc = (w_in[..., o + k * cc:o + (k + 1) * cc] for k in range(3))
    src, _ = _diff_head_lanes()
    cols = np.concatenate([h * LANES + src for h in range(DIFF_HEADS)])
    dq, dk = dq[..., cols], dk[..., cols]
    parts = [fq, fk, fv, dq, dk, dv]
    for t in range(_N_CONV_TILES):
        sl = slice(t * CONV_TILE, (t + 1) * CONV_TILE)
        parts += [gb[..., sl], gc[..., sl], hc[..., sl]]
    w = jnp.concatenate(parts, axis=-1).astype(BF16)
    wff = jnp.pad(ff, ((0, 0), (0, 0), (0, LANES - FOX_HEADS))).astype(BF16)
    return w, wff


def kernel(x, mem, positions, ffn1_norm, ffn1_w_gate, ffn1_w_up, ffn1_w_down, mix_norm, mix_w_in, forget_bias, conv_w, conv_b, lambda_q1, lambda_k1, lambda_q2, lambda_k2, diff_subln, mix_w_out, cross_norm, mem_norm, cross_w_q, cross_w_kv, cross_w_o, ffn2_norm, ffn2_w_gate, ffn2_w_up, ffn2_w_down, final_norm):
    batch, seq_len, d = x.shape
    mem_len = mem.shape[1]
    depth = ffn1_norm.shape[0]
    n_tok = batch * seq_len
    assert seq_len % TOKEN_TILE == 0 and seq_len % ATT_TILE == 0
    assert ffn1_w_gate.shape[2] % FF_TILE == 0

    bf = lambda a: a.astype(BF16)
    rows3 = lambda a: a.astype(F32).reshape(a.shape[0], 1, a.shape[1])
    w1g, w1u, w1d = bf(ffn1_w_gate), bf(ffn1_w_up), bf(ffn1_w_down)
    w2g, w2u, w2d = bf(ffn2_w_gate), bf(ffn2_w_up), bf(ffn2_w_down)
    w_in, w_ff = _arrange_in_weights(mix_w_in)
    w_out = bf(mix_w_out)
    wq, wkv, wo = bf(cross_w_q), bf(cross_w_kv), bf(cross_w_o)
    g_ffn1, g_mix, g_cross, g_mem, g_ffn2 = (rows3(a) for a in (ffn1_norm, mix_norm, cross_norm,
                                                                 mem_norm, ffn2_norm))
    g_final = final_norm.astype(F32).reshape(1, 1, d)
    fbias = rows3(jnp.pad(forget_bias, ((0, 0), (0, LANES - FOX_HEADS))))
    cbias = rows3(conv_b)
    subln = rows3(diff_subln)
    lam_vecs = jnp.stack([lambda_q1, lambda_k1, lambda_q2, lambda_k2], axis=1).astype(F32)
    ctab, stab = _rope_tables(positions)

    h = x.reshape(n_tok, d)
    mem2 = mem.reshape(batch * mem_len, d)
    nq = seq_len // ATT_TILE
    for l in range(depth):
        h = _ffn(h, g_ffn1, w1g, w1u, w1d, g_final, l, final=False)

        qk, conv, c, fox_vt, diff_vt = _inproj(h, g_mix, w_in, w_ff, fbias, ctab, stab,
                                               conv_w.astype(F32), cbias, l, seq_len=seq_len)
        c_rows = (c.reshape(batch, seq_len, LANES)[:, :, :FOX_HEADS]
                  .transpose(0, 2, 1).reshape(batch * FOX_HEADS, nq, 1, ATT_TILE))
        fox = _fox_attention(qk, fox_vt, c, c_rows, batch=batch, seq_len=seq_len)
        lam_init = 0.8 - 0.6 * math.exp(-0.3 * l)
        diff = _diff_attention(qk, diff_vt, lam_vecs, subln, l, batch=batch, seq_len=seq_len,
                               lam_init=lam_init)
        h = _mixout(h, fox, diff, conv, w_out, l)

        kv = _memkv(mem2, g_mem, wkv, l)
        h = _cross(h, g_cross, wq, kv, wo, l, seq_len=seq_len, mem_len=mem_len)

        h = _ffn(h, g_ffn2, w2g, w2u, w2d, g_final, l, final=(l == depth - 1))
    return h.reshape(batch, seq_len, d)
```

```python
import functools
import math

import numpy as np
import jax
import jax.numpy as jnp
from jax import lax
from jax.experimental import pallas as pl
from jax.experimental.pallas import tpu as pltpu

F32 = jnp.float32
BF16 = jnp.bfloat16

FOX_HEADS = 6
FOX_HEAD_DIM = 128
DIFF_HEADS = 6
DIFF_QK_DIM = 64
DIFF_V_DIM = 128
CONV_CH = 512
CONV_WIDTH = 3
ROT_DIM = 16
ROPE_THETA = 500000.0
CROSS_HEADS = 4
CROSS_HEAD_DIM = 128
EPS = 1e-6
NEG_INF = -1e30
LOG2E = math.log2(math.e)
FOX_Q_SCALE = FOX_HEAD_DIM ** -0.5 * LOG2E
DIFF_Q_SCALE = DIFF_QK_DIM ** -0.5 * LOG2E

LANES = 128
SUBLANES = 8
BF16_ROWS = 16

TOKEN_TILE = 1024
FF_TILE = 512
IN_TILE = 768
CONV_TILE = 256
ATT_TILE = 512
ONES_ROWS = BF16_ROWS
VMEM_LIMIT = 56 * 1024 * 1024
FFN_VMEM_LIMIT = 60 * 1024 * 1024

_ARB = "arbitrary"


def _params(n_axes, vmem_limit=VMEM_LIMIT):
    return pltpu.CompilerParams(dimension_semantics=(_ARB,) * n_axes,
                                vmem_limit_bytes=vmem_limit)


def _rms(x, g):
    return x * lax.rsqrt(jnp.mean(x * x, axis=-1, keepdims=True) + EPS) * g


def _dot(a, b):
    return jnp.dot(a, b, preferred_element_type=F32)


def _dot_t(a, b):
    return lax.dot_general(a, b, (((1,), (1,)), ((), ())), preferred_element_type=F32)


def _layer_spec(shape, index_map):
    return pl.BlockSpec((None,) + tuple(shape), index_map)


def _ffn_body(*refs, final, convert_next):
    x_ref, g_ref, wg_ref, wu_ref, wd_ref, fg_ref = refs[:6]
    if convert_next:
        src_refs, o_ref, dst_refs, n_ref = refs[6:9], refs[9], refs[10:13], refs[13]
        for src_ref, dst_ref in zip(src_refs, dst_refs):
            dst_ref[...] = src_ref[...].astype(BF16)
    else:
        o_ref, n_ref = refs[6:]
    j = pl.program_id(1)

    @pl.when(j == 0)
    def _():
        x = x_ref[...]
        n_ref[...] = _rms(x, g_ref[...]).astype(BF16)
        o_ref[...] = x

    n = n_ref[...]
    gate = _dot(n, wg_ref[...])
    up = _dot(n, wu_ref[...])
    act = (gate * jax.nn.sigmoid(gate) * (0.5 * up)).astype(BF16)
    o_ref[...] += _dot(act, wd_ref[...])

    if final:
        @pl.when(j == pl.num_programs(1) - 1)
        def _():
            o_ref[...] = _rms(o_ref[...], fg_ref[...])


def _ffn(h, g, weights, final_g, l, next_weights=None, *, final=False):
    n_tok, d = h.shape
    wg, wu, wd = weights
    d_ff = wg.shape[1]
    tm, tf = TOKEN_TILE, FF_TILE
    ni, nj = n_tok // tm, d_ff // tf
    in_specs = [
        pl.BlockSpec((tm, d), lambda i, j: (i, 0)),
        _layer_spec((1, d), lambda i, j: (l, 0, 0)),
        pl.BlockSpec((d, tf), lambda i, j: (0, j)),
        pl.BlockSpec((d, tf), lambda i, j: (0, j)),
        pl.BlockSpec((tf, d), lambda i, j: (j, 0)),
        _layer_spec((1, d), lambda i, j: (0, 0, 0)),
    ]
    out_specs = [pl.BlockSpec((tm, d), lambda i, j: (i, 0))]
    out_shape = [jax.ShapeDtypeStruct((n_tok, d), F32)]
    operands = [h, g, wg, wu, wd, final_g]
    convert_next = next_weights is not None
    if convert_next:
        ng, nu, nd, nl = next_weights
        assert d % ni == 0 and (d // ni) % BF16_ROWS == 0
        dr = d // ni
        in_specs += [_layer_spec((dr, tf), lambda i, j: (nl, i, j)),
                     _layer_spec((dr, tf), lambda i, j: (nl, i, j)),
                     _layer_spec((tf, dr), lambda i, j: (nl, j, i))]
        out_specs += [pl.BlockSpec((dr, tf), lambda i, j: (i, j)),
                      pl.BlockSpec((dr, tf), lambda i, j: (i, j)),
                      pl.BlockSpec((tf, dr), lambda i, j: (j, i))]
        out_shape += [jax.ShapeDtypeStruct(a.shape[1:], BF16) for a in (ng, nu, nd)]
        operands += [ng, nu, nd]
    outs = pl.pallas_call(
        functools.partial(_ffn_body, final=final, convert_next=convert_next),
        grid=(ni, nj),
        in_specs=in_specs,
        out_specs=out_specs,
        out_shape=out_shape,
        scratch_shapes=[pltpu.VMEM((tm, d), BF16)],
        compiler_params=_params(2, FFN_VMEM_LIMIT),
        name="ffn_final" if final else "ffn",
    )(*operands)
    return outs[0], (tuple(outs[1:]) if convert_next else None)


_J_FOX_Q, _J_FOX_K, _J_FOX_V, _J_DIFF_Q, _J_DIFF_K, _J_DIFF_V, _J_CONV = 0, 1, 2, 3, 4, 5, 6
_N_CONV_TILES = CONV_CH // CONV_TILE
_N_IN_TILES = _J_CONV + _N_CONV_TILES


def _inproj_body(x_ref, g_ref, w_ref, wff_ref, fb_ref, ct_ref, st_ref, cw_ref, cb_ref,
                 qk_ref, conv_ref, c_ref, fvt_ref, dvt_ref, n_ref, zbuf_ref, carry_ref,
                 *, tiles_per_seq):
    i = pl.program_id(0)
    j = pl.program_id(1)
    tm = x_ref.shape[0]
    seq_start = (i % tiles_per_seq) == 0

    @pl.when(j == 0)
    def _():
        n_ref[...] = _rms(x_ref[...], g_ref[...]).astype(BF16)

        @pl.when(seq_start)
        def _():
            zbuf_ref[:, 0:SUBLANES, :] = jnp.zeros((_N_CONV_TILES, SUBLANES, CONV_TILE), F32)

    def project():
        return _dot(n_ref[...], w_ref[...])

    @pl.when(j == _J_FOX_Q)
    def _():
        qk_ref[...] = (project() * FOX_Q_SCALE).astype(BF16)
        logf = jax.nn.log_sigmoid(_dot(n_ref[...], wff_ref[...]) + fb_ref[...])
        lt = logf.T[0:BF16_ROWS, :]
        hi = lt.astype(BF16)
        r1 = lt - hi.astype(F32)
        mid = r1.astype(BF16)
        lo = (r1 - mid.astype(F32)).astype(BF16)
        src = lax.broadcasted_iota(jnp.int32, (tm, tm), 0)
        dst = lax.broadcasted_iota(jnp.int32, (tm, tm), 1)
        tri = jnp.where(src <= dst, 1.0, 0.0).astype(BF16)
        parts = _dot(jnp.concatenate([hi, mid, lo], axis=0), tri)
        prev = jnp.where(seq_start, 0.0, carry_ref[...])
        ct = (parts[0:BF16_ROWS] + parts[BF16_ROWS:2 * BF16_ROWS] + parts[2 * BF16_ROWS:]) + prev
        carry_ref[...] = ct[:, tm - 1:tm]
        c_ref[...] = jnp.concatenate([ct, jnp.zeros((LANES - BF16_ROWS, tm), F32)], axis=0).T

    @pl.when(j == _J_FOX_K)
    def _():
        qk_ref[...] = project().astype(BF16)

    @pl.when(j == _J_FOX_V)
    def _():
        fvt_ref[...] = project().T.astype(BF16)

    @pl.when(j == _J_DIFF_V)
    def _():
        dvt_ref[...] = project().T.astype(BF16)

    for jj, q_scale in ((_J_DIFF_Q, DIFF_Q_SCALE), (_J_DIFF_K, None)):
        @pl.when(j == jj)
        def _(q_scale=q_scale):
            y = project()
            ct = ct_ref[...]
            st = st_ref[...]
            for c0 in range(0, IN_TILE, LANES):
                yc = y[:, c0:c0 + LANES]
                out = yc * ct + pltpu.roll(yc, LANES // 2, axis=1) * st
                if q_scale is not None:
                    out = out * q_scale
                qk_ref[:, c0:c0 + LANES] = out.astype(BF16)

    for t in range(_N_CONV_TILES):
        @pl.when(j == _J_CONV + t)
        def _(t=t):
            y = project()
            gb = y[:, 0:CONV_TILE]
            z = y[:, CONV_TILE:2 * CONV_TILE] * y[:, 2 * CONV_TILE:3 * CONV_TILE]
            zb = zbuf_ref.at[t]
            zb[SUBLANES:SUBLANES + tm, :] = z
            z1 = zb[SUBLANES - 1:SUBLANES - 1 + tm, :]
            z2 = zb[SUBLANES - 2:SUBLANES - 2 + tm, :]
            cw = cw_ref[...]
            conv = z2 * cw[0:1, :] + z1 * cw[1:2, :] + z * cw[2:3, :] + cb_ref[...]
            conv_ref[...] = (gb * conv).astype(BF16)
            zb[0:SUBLANES, :] = z[tm - SUBLANES:tm, :]


def _inproj(h, g, w, wff, fbias, ctab, stab, conv_w, conv_b, l, *, seq_len):
    n_tok, d = h.shape
    tm = TOKEN_TILE
    conv_col = lambda j: jnp.clip(j - _J_CONV, 0, _N_CONV_TILES - 1)
    qk_col = lambda j: jnp.minimum(j - (j >= _J_FOX_V).astype(jnp.int32)
                                   - (j >= _J_DIFF_V).astype(jnp.int32), 3)
    return pl.pallas_call(
        functools.partial(_inproj_body, tiles_per_seq=seq_len // tm),
        grid=(n_tok // tm, _N_IN_TILES),
        in_specs=[
            pl.BlockSpec((tm, d), lambda i, j: (i, 0)),
            _layer_spec((1, d), lambda i, j: (l, 0, 0)),
            _layer_spec((d, IN_TILE), lambda i, j: (l, 0, j)),
            _layer_spec((d, LANES), lambda i, j: (l, 0, 0)),
            _layer_spec((1, LANES), lambda i, j: (l, 0, 0)),
            pl.BlockSpec((tm, LANES), lambda i, j: (i, 0)),
            pl.BlockSpec((tm, LANES), lambda i, j: (i, 0)),
            _layer_spec((CONV_WIDTH, CONV_TILE), lambda i, j: (l, 0, conv_col(j))),
            _layer_spec((1, CONV_TILE), lambda i, j: (l, 0, conv_col(j))),
        ],
        out_specs=[
            pl.BlockSpec((tm, IN_TILE), lambda i, j: (i, qk_col(j))),
            pl.BlockSpec((tm, CONV_TILE), lambda i, j: (i, conv_col(j))),
            pl.BlockSpec((tm, LANES), lambda i, j: (i, 0)),
            pl.BlockSpec((IN_TILE, tm), lambda i, j: (0, i)),
            pl.BlockSpec((IN_TILE, tm), lambda i, j: (0, i)),
        ],
        out_shape=[
            jax.ShapeDtypeStruct((n_tok, 4 * IN_TILE), BF16),
            jax.ShapeDtypeStruct((n_tok, CONV_CH), BF16),
            jax.ShapeDtypeStruct((n_tok, LANES), F32),
            jax.ShapeDtypeStruct((IN_TILE, n_tok), BF16),
            jax.ShapeDtypeStruct((IN_TILE, n_tok), BF16),
        ],
        scratch_shapes=[
            pltpu.VMEM((tm, d), BF16),
            pltpu.VMEM((_N_CONV_TILES, tm + SUBLANES, CONV_TILE), F32),
            pltpu.VMEM((BF16_ROWS, 1), F32),
        ],
        compiler_params=_params(2),
        name="inproj",
    )(h, g, w, wff, fbias, ctab, stab, conv_w, conv_b)


_T_QI, _T_KB, _T_DIAG, _T_STATE = 0, 1, 2, 3
PIPE_LAG = 2
PIPE_SLOTS = 2 * PIPE_LAG


def _item_table(nq):
    items = [(qi, kb) for qi in range(nq) for kb in range(qi + 1)]
    n_loops = -(-len(items) // PIPE_SLOTS)
    n_pos = PIPE_SLOTS * n_loops + 2 * PIPE_LAG
    tab = np.zeros((4, n_pos), np.int32)
    for pos in range(n_pos):
        item = pos - PIPE_LAG
        qi, kb = items[min(max(item, 0), len(items) - 1)]
        real = 0 <= item < len(items)
        tab[:, pos] = (qi, kb, int(real and kb == qi), qi if real else nq)
    return tab, n_loops


def _fill_bias(bias_ref, t):
    key = lax.broadcasted_iota(jnp.int32, (t, t), 0)
    qry = lax.broadcasted_iota(jnp.int32, (t, t), 1)
    bias_ref[0] = jnp.zeros((t, t), F32)
    bias_ref[1] = jnp.where(key <= qry, 0.0, NEG_INF)


def _stage_values(vt_ref, vte_ref, t):
    hd = vt_ref.shape[0]
    for kb in range(vte_ref.shape[0]):
        vte_ref[kb, 0:hd, :] = vt_ref[:, kb * t:(kb + 1) * t]
        vte_ref[kb, hd:, :] = jnp.ones((ONES_ROWS, t), BF16)


def _store_scores(u, u_ref, mx_ref):
    u_ref[...] = u
    mx_ref[...] = jnp.max(u, axis=0, keepdims=True)


def _softmax_stage(u_ref, mx_ref, shift, m_ref, p_ref, al_ref):
    m_old = m_ref[...]
    m_new = jnp.maximum(m_old, mx_ref[...] + shift)
    al_ref[...] = jnp.exp2(m_old - m_new)
    p_ref[...] = jnp.exp2(u_ref[...] - (m_new - shift)).astype(BF16)
    m_ref[...] = m_new


def _value_stage(vt, p_ref, al_ref, acc_ref):
    acc_ref[...] = al_ref[...] * acc_ref[...] + _dot(vt, p_ref[...])


def _fox_body(tab_ref, q_ref, k_ref, vt_ref, c_ref, crow_ref, o_ref,
              ckb_ref, vte_ref, bias_ref, u_ref, mx_ref, p_ref, al_ref, m_ref, acc_ref, *, n_loops):
    b = pl.program_id(0)
    h = pl.program_id(1)
    t = ATT_TILE
    hd = FOX_HEAD_DIM

    @pl.when((b == 0) & (h == 0))
    def _():
        _fill_bias(bias_ref, t)

    lane = lax.broadcasted_iota(jnp.int32, c_ref.shape, 1)
    ck = jnp.sum(jnp.where(lane == h, c_ref[...], 0.0), axis=-1, keepdims=True)
    ckb_ref[...] = jnp.broadcast_to(ck * LOG2E, ckb_ref.shape)
    _stage_values(vt_ref, vte_ref, t)
    p_ref[PIPE_LAG:] = jnp.zeros((PIPE_LAG, t, t), BF16)
    al_ref[PIPE_LAG:] = jnp.zeros((PIPE_LAG, 1, t), F32)
    acc_ref[...] = jnp.zeros_like(acc_ref)
    m_ref[...] = jnp.full_like(m_ref, NEG_INF)

    def rows(idx):
        return pl.ds(pl.multiple_of(idx * t, t), t)

    def score_stage(pos, slot):
        kb = tab_ref[_T_KB, pos]
        u = (_dot_t(k_ref[rows(kb), :], q_ref[rows(tab_ref[_T_QI, pos]), :])
             - jnp.tile(ckb_ref[rows(kb), :], (1, t // LANES)) + bias_ref[tab_ref[_T_DIAG, pos]])
        _store_scores(u, u_ref.at[slot], mx_ref.at[slot])

    def step(tau, slot):
        lagged = (slot + PIPE_LAG) % PIPE_SLOTS
        score_stage(tau + 2 * PIPE_LAG, lagged)
        pos = tau + PIPE_LAG
        cq = crow_ref[0, tab_ref[_T_QI, pos]] * LOG2E
        _softmax_stage(u_ref.at[slot], mx_ref.at[slot], cq, m_ref.at[tab_ref[_T_STATE, pos]],
                       p_ref.at[slot], al_ref.at[slot])
        value_stage(tau, lagged)

    def value_stage(tau, lagged):
        _value_stage(vte_ref[tab_ref[_T_KB, tau]], p_ref.at[lagged], al_ref.at[lagged],
                     acc_ref.at[tab_ref[_T_STATE, tau]])

    for item in range(PIPE_LAG):
        score_stage(item + PIPE_LAG, item)

    def unrolled(it, carry):
        for slot in range(PIPE_SLOTS):
            step(PIPE_SLOTS * it + slot, slot)
        return carry

    lax.fori_loop(0, n_loops, unrolled, 0)
    for tau in range(PIPE_SLOTS * n_loops, PIPE_SLOTS * n_loops + PIPE_LAG):
        value_stage(tau, (tau + PIPE_LAG) % PIPE_SLOTS)

    for qi in range(acc_ref.shape[0] - 1):
        acc = acc_ref[qi]
        o_ref[qi * t:(qi + 1) * t, :] = (acc[:hd, :] / acc[hd:hd + 1, :]).T.astype(BF16)


def _attention_scratch(t, nq, rows, n_streams):
    shape = lambda *s: ((n_streams,) if n_streams > 1 else ()) + s
    return [
        pltpu.VMEM((PIPE_SLOTS,) + shape(t, t), F32),
        pltpu.VMEM((PIPE_SLOTS,) + shape(1, t), F32),
        pltpu.VMEM((PIPE_SLOTS,) + shape(t, t), BF16),
        pltpu.VMEM((PIPE_SLOTS,) + shape(1, t), F32),
        pltpu.VMEM(shape(nq + 1, 1, t), F32),
        pltpu.VMEM(shape(nq + 1, rows, t), F32),
    ]


def _fox_attention(qk, vt, c, c_rows, *, batch, seq_len):
    t = ATT_TILE
    nq = seq_len // t
    hd = FOX_HEAD_DIM
    rows = hd + ONES_ROWS
    tab, n_loops = _item_table(nq)
    grid_spec = pltpu.PrefetchScalarGridSpec(
        num_scalar_prefetch=1,
        grid=(batch, FOX_HEADS),
        in_specs=[
            pl.BlockSpec((seq_len, hd), lambda b, h, tab: (b, h)),
            pl.BlockSpec((seq_len, hd), lambda b, h, tab: (b, FOX_HEADS + h)),
            pl.BlockSpec((hd, seq_len), lambda b, h, tab: (h, b)),
            pl.BlockSpec((seq_len, LANES), lambda b, h, tab: (b, 0)),
            pl.BlockSpec((1, nq, 1, t), lambda b, h, tab: (b * FOX_HEADS + h, 0, 0, 0)),
        ],
        out_specs=pl.BlockSpec((seq_len, hd), lambda b, h, tab: (b, h)),
        scratch_shapes=[pltpu.VMEM((seq_len, LANES), F32),
                        pltpu.VMEM((nq, rows, t), BF16),
                        pltpu.VMEM((2, t, t), F32)] + _attention_scratch(t, nq, rows, 1),
    )
    return pl.pallas_call(
        functools.partial(_fox_body, n_loops=n_loops),
        grid_spec=grid_spec,
        out_shape=jax.ShapeDtypeStruct((batch * seq_len, FOX_HEADS * hd), BF16),
        compiler_params=_params(2),
        name="fox_attention",
    )(jnp.asarray(tab), qk, qk, vt, c, c_rows)


def _diff_body(tab_ref, q_ref, k_ref, vt_ref, lam_ref, sg_ref, o_ref,
               vte_ref, bias_ref, u_ref, mx_ref, p_ref, al_ref, m_ref, acc_ref, *, n_loops, lam_init):
    b = pl.program_id(0)
    h = pl.program_id(1)
    t = ATT_TILE
    hd = DIFF_V_DIM

    @pl.when((b == 0) & (h == 0))
    def _():
        _fill_bias(bias_ref, t)

    _stage_values(vt_ref, vte_ref, t)
    p_ref[PIPE_LAG:] = jnp.zeros((PIPE_LAG, 2, t, t), BF16)
    al_ref[PIPE_LAG:] = jnp.zeros((PIPE_LAG, 2, 1, t), F32)
    acc_ref[...] = jnp.zeros_like(acc_ref)
    m_ref[...] = jnp.full_like(m_ref, NEG_INF)
    no_shift = jnp.zeros((1, t), F32)

    def rows(idx):
        return pl.ds(pl.multiple_of(idx * t, t), t)

    def score_stage(pos, slot):
        q = q_ref[rows(tab_ref[_T_QI, pos]), :]
        k = k_ref[rows(tab_ref[_T_KB, pos]), :]
        bias = bias_ref[tab_ref[_T_DIAG, pos]]
        lane = lax.broadcasted_iota(jnp.int32, q.shape, 1)
        comp0 = (lane < ROT_DIM // 2) | ((lane >= ROT_DIM) & (lane < LANES // 2 + ROT_DIM // 2))
        zero = jnp.zeros_like(q)
        for s, keep in enumerate((comp0, jnp.logical_not(comp0))):
            _store_scores(_dot_t(k, jnp.where(keep, q, zero)) + bias,
                          u_ref.at[slot, s], mx_ref.at[slot, s])

    def step(tau, slot):
        lagged = (slot + PIPE_LAG) % PIPE_SLOTS
        score_stage(tau + 2 * PIPE_LAG, lagged)
        state = tab_ref[_T_STATE, tau + PIPE_LAG]
        for s in range(2):
            _softmax_stage(u_ref.at[slot, s], mx_ref.at[slot, s], no_shift, m_ref.at[s, state],
                           p_ref.at[slot, s], al_ref.at[slot, s])
        value_stage(tau, lagged)

    def value_stage(tau, lagged):
        vt = vte_ref[tab_ref[_T_KB, tau]]
        for s in range(2):
            _value_stage(vt, p_ref.at[lagged, s], al_ref.at[lagged, s],
                         acc_ref.at[s, tab_ref[_T_STATE, tau]])

    for item in range(PIPE_LAG):
        score_stage(item + PIPE_LAG, item)

    def unrolled(it, carry):
        for slot in range(PIPE_SLOTS):
            step(PIPE_SLOTS * it + slot, slot)
        return carry

    lax.fori_loop(0, n_loops, unrolled, 0)
    for tau in range(PIPE_SLOTS * n_loops, PIPE_SLOTS * n_loops + PIPE_LAG):
        value_stage(tau, (tau + PIPE_LAG) % PIPE_SLOTS)

    lv = lam_ref[...]
    lam = (jnp.exp(jnp.sum(lv[0:1, :] * lv[1:2, :], axis=-1, keepdims=True))
           - jnp.exp(jnp.sum(lv[2:3, :] * lv[3:4, :], axis=-1, keepdims=True))
           + lam_init)
    for qi in range(acc_ref.shape[1] - 1):
        a1 = acc_ref[0, qi]
        a2 = acc_ref[1, qi]
        o = a1[:hd, :] / a1[hd:hd + 1, :] - lam * (a2[:hd, :] / a2[hd:hd + 1, :])
        o = o * lax.rsqrt(jnp.mean(o * o, axis=0, keepdims=True) + EPS)
        o_ref[qi * t:(qi + 1) * t, :] = (o.T * sg_ref[...] * (1.0 - lam_init)).astype(BF16)


def _diff_attention(qk, vt, lam_vecs, subln, l, *, batch, seq_len, lam_init):
    t = ATT_TILE
    nq = seq_len // t
    hd = DIFF_V_DIM
    rows = hd + ONES_ROWS
    base = 2 * FOX_HEADS
    tab, n_loops = _item_table(nq)
    grid_spec = pltpu.PrefetchScalarGridSpec(
        num_scalar_prefetch=1,
        grid=(batch, DIFF_HEADS),
        in_specs=[
            pl.BlockSpec((seq_len, hd), lambda b, h, tab: (b, base + h)),
            pl.BlockSpec((seq_len, hd), lambda b, h, tab: (b, base + DIFF_HEADS + h)),
            pl.BlockSpec((hd, seq_len), lambda b, h, tab: (h, b)),
            _layer_spec((4, DIFF_QK_DIM), lambda b, h, tab: (l, 0, 0)),
            _layer_spec((1, hd), lambda b, h, tab: (l, 0, 0)),
        ],
        out_specs=pl.BlockSpec((seq_len, hd), lambda b, h, tab: (b, h)),
        scratch_shapes=[pltpu.VMEM((nq, rows, t), BF16),
                        pltpu.VMEM((2, t, t), F32)] + _attention_scratch(t, nq, rows, 2),
    )
    return pl.pallas_call(
        functools.partial(_diff_body, n_loops=n_loops, lam_init=lam_init),
        grid_spec=grid_spec,
        out_shape=jax.ShapeDtypeStruct((batch * seq_len, DIFF_HEADS * hd), BF16),
        compiler_params=_params(2),
        name="diff_attention",
    )(jnp.asarray(tab), qk, qk, vt, lam_vecs, subln)


def _mixout_body(h_ref, fox_ref, diff_ref, conv_ref, wf_ref, wd_ref, wc_ref, o_ref):
    o_ref[...] = (h_ref[...] + _dot(fox_ref[...], wf_ref[...])
                  + _dot(diff_ref[...], wd_ref[...]) + _dot(conv_ref[...], wc_ref[...]))


def _mixout(h, fox, diff, conv, w_out, l):
    n_tok, d = h.shape
    tm = TOKEN_TILE
    fw, dw, cw = fox.shape[1], diff.shape[1], conv.shape[1]
    assert fw == dw and (fw + dw) % cw == 0
    row = lambda i: (i, 0)
    resident = lambda rows, blk: pl.BlockSpec((None, rows, d), lambda i: (l, blk, 0),
                                              pipeline_mode=pl.Buffered(1))
    return pl.pallas_call(
        _mixout_body,
        grid=(n_tok // tm,),
        in_specs=[
            pl.BlockSpec((tm, d), row),
            pl.BlockSpec((tm, fw), row),
            pl.BlockSpec((tm, dw), row),
            pl.BlockSpec((tm, cw), row),
            resident(fw, 0),
            resident(dw, 1),
            resident(cw, (fw + dw) // cw),
        ],
        out_specs=pl.BlockSpec((tm, d), row),
        out_shape=jax.ShapeDtypeStruct((n_tok, d), F32),
        compiler_params=_params(1),
        name="mixout",
    )(h, fox, diff, conv, w_out, w_out, w_out)


def _memkv_body(m_ref, g_ref, w_ref, o_ref):
    o_ref[...] = _dot(_rms(m_ref[...], g_ref[...]).astype(BF16), w_ref[...]).astype(BF16)


def _memkv(mem, g, w, l):
    n_mem, d = mem.shape
    tm = min(n_mem, TOKEN_TILE)
    width = w.shape[2]
    return pl.pallas_call(
        _memkv_body,
        grid=(n_mem // tm,),
        in_specs=[
            pl.BlockSpec((tm, d), lambda i: (i, 0)),
            _layer_spec((1, d), lambda i: (l, 0, 0)),
            pl.BlockSpec((None, d, width), lambda i: (l, 0, 0), pipeline_mode=pl.Buffered(1)),
        ],
        out_specs=pl.BlockSpec((tm, width), lambda i: (i, 0)),
        out_shape=jax.ShapeDtypeStruct((n_mem, width), BF16),
        compiler_params=_params(1),
        name="memkv",
    )(mem, g, w)


def _cross_body(h_ref, g_ref, wq_ref, kv_ref, wo_ref, o_ref):
    hd = CROSS_HEAD_DIM
    width = CROSS_HEADS * hd
    scale = hd ** -0.5
    x = h_ref[...]
    q = _dot(_rms(x, g_ref[...]).astype(BF16), wq_ref[...]).astype(BF16)
    heads = []
    for hh in range(CROSS_HEADS):
        k = kv_ref[:, hh * hd:(hh + 1) * hd]
        v = kv_ref[:, width + hh * hd:width + (hh + 1) * hd]
        s = _dot_t(q[:, hh * hd:(hh + 1) * hd], k) * scale
        e = jnp.exp(s - jnp.max(s, axis=-1, keepdims=True))
        p = e / jnp.sum(e, axis=-1, keepdims=True)
        heads.append(_dot(p.astype(BF16), v))
    o = jnp.concatenate(heads, axis=-1).astype(BF16)
    o_ref[...] = x + _dot(o, wo_ref[...])


def _cross(h, g, wq, kv, wo, l, *, seq_len, mem_len):
    n_tok, d = h.shape
    tm = TOKEN_TILE
    tiles_per_seq = seq_len // tm
    resident = lambda a: pl.BlockSpec((None,) + a.shape[1:], lambda i: (l, 0, 0),
                                      pipeline_mode=pl.Buffered(1))
    return pl.pallas_call(
        _cross_body,
        grid=(n_tok // tm,),
        in_specs=[
            pl.BlockSpec((tm, d), lambda i: (i, 0)),
            _layer_spec((1, d), lambda i: (l, 0, 0)),
            resident(wq),
            pl.BlockSpec((mem_len, kv.shape[1]), lambda i: (i // tiles_per_seq, 0)),
            resident(wo),
        ],
        out_specs=pl.BlockSpec((tm, d), lambda i: (i, 0)),
        out_shape=jax.ShapeDtypeStruct((n_tok, d), F32),
        compiler_params=_params(1),
        name="cross_attention",
    )(h, g, wq, kv, wo)


def _rope_tables(positions):
    half = ROT_DIM // 2
    inv_freq = ROPE_THETA ** (-jnp.arange(0, ROT_DIM, 2, dtype=F32) / ROT_DIM)
    ang = positions.astype(F32)[..., None] * inv_freq
    cos = jnp.cos(ang).reshape(-1, half)
    sin = jnp.sin(ang).reshape(-1, half)
    lane = np.arange(LANES)
    rotated = (lane % (LANES // 2)) < ROT_DIM
    sign = np.where(lane < LANES // 2, -1.0, 1.0).astype(np.float32)
    reps = LANES // half
    ct = jnp.where(rotated[None, :], jnp.tile(cos, (1, reps)), 1.0)
    st = jnp.where(rotated[None, :], jnp.tile(sin, (1, reps)) * sign[None, :], 0.0)
    return ct, st


def _diff_head_lanes():
    half = ROT_DIM // 2
    src = np.zeros(LANES, np.int64)
    comp0 = np.zeros(LANES, bool)
    for c in range(2):
        for d in range(DIFF_QK_DIM):
            if d < half:
                lane = c * half + d
            elif d < ROT_DIM:
                lane = LANES // 2 + c * half + (d - half)
            else:
                lane = (ROT_DIM if c == 0 else LANES // 2 + ROT_DIM) + (d - ROT_DIM)
            src[lane] = c * DIFF_QK_DIM + d
            comp0[lane] = c == 0
    return src, comp0


def _arrange_in_weights(w_in):
    fw, dw, cc = FOX_HEADS * FOX_HEAD_DIM, DIFF_HEADS * DIFF_V_DIM, CONV_CH
    o = 0
    fq, fk, fv = (w_in[..., o + k * fw:o + (k + 1) * fw] for k in range(3))
    o += 3 * fw
    ff = w_in[..., o:o + FOX_HEADS]
    o += FOX_HEADS
    dq, dk, dv = (w_in[..., o + k * dw:o + (k + 1) * dw] for k in range(3))
    o += 3 * dw
    gb, gc, hc = (w_in[..., o + k * cc:o + (k + 1) * cc] for k in range(3))
    src, _ = _diff_head_lanes()
    cols = np.concatenate([h * LANES + src for h in range(DIFF_HEADS)])
    dq, dk = dq[..., cols], dk[..., cols]
    parts = [fq, fk, fv, dq, dk, dv]
    for t in range(_N_CONV_TILES):
        sl = slice(t * CONV_TILE, (t + 1) * CONV_TILE)
        parts += [gb[..., sl], gc[..., sl], hc[..., sl]]
    w = jnp.concatenate(parts, axis=-1).astype(BF16)
    wff = jnp.pad(ff, ((0, 0), (0, 0), (0, LANES - FOX_HEADS))).astype(BF16)
    return w, wff


def kernel(x, mem, positions, ffn1_norm, ffn1_w_gate, ffn1_w_up, ffn1_w_down, mix_norm, mix_w_in, forget_bias, conv_w, conv_b, lambda_q1, lambda_k1, lambda_q2, lambda_k2, diff_subln, mix_w_out, cross_norm, mem_norm, cross_w_q, cross_w_kv, cross_w_o, ffn2_norm, ffn2_w_gate, ffn2_w_up, ffn2_w_down, final_norm):
    batch, seq_len, d = x.shape
    mem_len = mem.shape[1]
    depth = ffn1_norm.shape[0]
    n_tok = batch * seq_len
    assert seq_len % TOKEN_TILE == 0 and seq_len % ATT_TILE == 0
    assert ffn1_w_gate.shape[2] % FF_TILE == 0

    bf = lambda a: a.astype(BF16)
    rows3 = lambda a: a.astype(F32).reshape(a.shape[0], 1, a.shape[1])
    ffn1_w = (ffn1_w_gate, ffn1_w_up, ffn1_w_down)
    ffn2_w = (ffn2_w_gate, ffn2_w_up, ffn2_w_down)
    ffn_w = tuple(bf(a[0]) for a in ffn1_w)
    w_in, w_ff = _arrange_in_weights(mix_w_in)
    w_out = bf(mix_w_out)
    wq, wkv, wo = bf(cross_w_q), bf(cross_w_kv), bf(cross_w_o)
    g_ffn1, g_mix, g_cross, g_mem, g_ffn2 = (rows3(a) for a in (ffn1_norm, mix_norm, cross_norm,
                                                                 mem_norm, ffn2_norm))
    g_final = final_norm.astype(F32).reshape(1, 1, d)
    fbias = rows3(jnp.pad(forget_bias, ((0, 0), (0, LANES - FOX_HEADS))))
    cbias = rows3(conv_b)
    subln = rows3(diff_subln)
    lam_vecs = jnp.stack([lambda_q1, lambda_k1, lambda_q2, lambda_k2], axis=1).astype(F32)
    ctab, stab = _rope_tables(positions)

    h = x.reshape(n_tok, d)
    mem2 = mem.reshape(batch * mem_len, d)
    nq = seq_len // ATT_TILE
    for l in range(depth):
        h, ffn_w = _ffn(h, g_ffn1, ffn_w, g_final, l, ffn2_w + (l,))

        qk, conv, c, fox_vt, diff_vt = _inproj(h, g_mix, w_in, w_ff, fbias, ctab, stab,
                                               conv_w.astype(F32), cbias, l, seq_len=seq_len)
        c_rows = (c.reshape(batch, seq_len, LANES)[:, :, :FOX_HEADS]
                  .transpose(0, 2, 1).reshape(batch * FOX_HEADS, nq, 1, ATT_TILE))
        fox = _fox_attention(qk, fox_vt, c, c_rows, batch=batch, seq_len=seq_len)
        lam_init = 0.8 - 0.6 * math.exp(-0.3 * l)
        diff = _diff_attention(qk, diff_vt, lam_vecs, subln, l, batch=batch, seq_len=seq_len,
                               lam_init=lam_init)
        h = _mixout(h, fox, diff, conv, w_out, l)

        kv = _memkv(mem2, g_mem, wkv, l)
        h = _cross(h, g_cross, wq, kv, wo, l, seq_len=seq_len, mem_len=mem_len)

        last = l == depth - 1
        h, ffn_w = _ffn(h, g_ffn2, ffn_w, g_final, l, None if last else ffn1_w + (l + 1,), final=last)
    return h.reshape(batch, seq_len, d)
```

```python
import functools
import math

import numpy as np
import jax
import jax.numpy as jnp
from jax import lax
from jax.experimental import pallas as pl
from jax.experimental.pallas import tpu as pltpu

F32 = jnp.float32
BF16 = jnp.bfloat16

FOX_HEADS = 6
FOX_HEAD_DIM = 128
DIFF_HEADS = 6
DIFF_QK_DIM = 64
DIFF_V_DIM = 128
CONV_CH = 512
CONV_WIDTH = 3
ROT_DIM = 16
ROPE_THETA = 500000.0
CROSS_HEADS = 4
CROSS_HEAD_DIM = 128
EPS = 1e-6
NEG_INF = -1e30
LOG2E = math.log2(math.e)
FOX_Q_SCALE = FOX_HEAD_DIM ** -0.5 * LOG2E
DIFF_Q_SCALE = DIFF_QK_DIM ** -0.5 * LOG2E

LANES = 128
SUBLANES = 8
BF16_ROWS = 16

TOKEN_TILE = 1024
FF_TILE = 512
IN_TILE = 768
CONV_TILE = 256
ATT_TILE = 512
PREP_ROWS = 256
ONES_ROWS = BF16_ROWS
VMEM_LIMIT = 56 * 1024 * 1024
FFN_VMEM_LIMIT = 60 * 1024 * 1024

_ARB = "arbitrary"


def _params(n_axes, vmem_limit=VMEM_LIMIT):
    return pltpu.CompilerParams(dimension_semantics=(_ARB,) * n_axes,
                                vmem_limit_bytes=vmem_limit)


def _rms(x, g):
    return x * lax.rsqrt(jnp.mean(x * x, axis=-1, keepdims=True) + EPS) * g


def _dot(a, b):
    return jnp.dot(a, b, preferred_element_type=F32)


def _dot_t(a, b):
    return lax.dot_general(a, b, (((1,), (1,)), ((), ())), preferred_element_type=F32)


def _layer_spec(shape, index_map):
    return pl.BlockSpec((None,) + tuple(shape), index_map)


def _ffn_body(*refs, final, convert_next):
    x_ref, g_ref, wg_ref, wu_ref, wd_ref, fg_ref = refs[:6]
    if convert_next:
        src_refs, o_ref, dst_refs, n_ref = refs[6:9], refs[9], refs[10:13], refs[13]
        for src_ref, dst_ref in zip(src_refs, dst_refs):
            dst_ref[...] = src_ref[...].astype(BF16)
    else:
        o_ref, n_ref = refs[6:]
    j = pl.program_id(1)

    @pl.when(j == 0)
    def _():
        x = x_ref[...]
        n_ref[...] = _rms(x, g_ref[...]).astype(BF16)
        o_ref[...] = x

    n = n_ref[...]
    gate = _dot(n, wg_ref[...])
    up = _dot(n, wu_ref[...])
    act = (gate * jax.nn.sigmoid(gate) * (0.5 * up)).astype(BF16)
    o_ref[...] += _dot(act, wd_ref[...])

    if final:
        @pl.when(j == pl.num_programs(1) - 1)
        def _():
            o_ref[...] = _rms(o_ref[...], fg_ref[...])


def _ffn(h, g, weights, final_g, l, next_weights=None, *, final=False):
    n_tok, d = h.shape
    wg, wu, wd = weights
    d_ff = wg.shape[1]
    tm, tf = TOKEN_TILE, FF_TILE
    ni, nj = n_tok // tm, d_ff // tf
    in_specs = [
        pl.BlockSpec((tm, d), lambda i, j: (i, 0)),
        _layer_spec((1, d), lambda i, j: (l, 0, 0)),
        pl.BlockSpec((d, tf), lambda i, j: (0, j)),
        pl.BlockSpec((d, tf), lambda i, j: (0, j)),
        pl.BlockSpec((tf, d), lambda i, j: (j, 0)),
        _layer_spec((1, d), lambda i, j: (0, 0, 0)),
    ]
    out_specs = [pl.BlockSpec((tm, d), lambda i, j: (i, 0))]
    out_shape = [jax.ShapeDtypeStruct((n_tok, d), F32)]
    operands = [h, g, wg, wu, wd, final_g]
    convert_next = next_weights is not None
    if convert_next:
        ng, nu, nd, nl = next_weights
        assert d % ni == 0 and (d // ni) % BF16_ROWS == 0
        dr = d // ni
        in_specs += [_layer_spec((dr, tf), lambda i, j: (nl, i, j)),
                     _layer_spec((dr, tf), lambda i, j: (nl, i, j)),
                     _layer_spec((tf, dr), lambda i, j: (nl, j, i))]
        out_specs += [pl.BlockSpec((dr, tf), lambda i, j: (i, j)),
                      pl.BlockSpec((dr, tf), lambda i, j: (i, j)),
                      pl.BlockSpec((tf, dr), lambda i, j: (j, i))]
        out_shape += [jax.ShapeDtypeStruct(a.shape[1:], BF16) for a in (ng, nu, nd)]
        operands += [ng, nu, nd]
    outs = pl.pallas_call(
        functools.partial(_ffn_body, final=final, convert_next=convert_next),
        grid=(ni, nj),
        in_specs=in_specs,
        out_specs=out_specs,
        out_shape=out_shape,
        scratch_shapes=[pltpu.VMEM((tm, d), BF16)],
        compiler_params=_params(2, FFN_VMEM_LIMIT),
        name="ffn_final" if final else "ffn",
    )(*operands)
    return outs[0], (tuple(outs[1:]) if convert_next else None)


_J_FOX_Q, _J_FOX_K, _J_FOX_V, _J_DIFF_Q, _J_DIFF_K, _J_DIFF_V, _J_CONV = 0, 1, 2, 3, 4, 5, 6
_N_CONV_TILES = CONV_CH // CONV_TILE
_N_IN_TILES = _J_CONV + _N_CONV_TILES


def _inproj_body(x_ref, g_ref, w_ref, wff_ref, fb_ref, ct_ref, st_ref, cw_ref, cb_ref,
                 qk_ref, conv_ref, c_ref, crow_ref, fvt_ref, dvt_ref, n_ref, zbuf_ref, carry_ref,
                 *, tiles_per_seq):
    i = pl.program_id(0)
    j = pl.program_id(1)
    tm = x_ref.shape[0]
    seq_start = (i % tiles_per_seq) == 0

    @pl.when(j == 0)
    def _():
        n_ref[...] = _rms(x_ref[...], g_ref[...]).astype(BF16)

        @pl.when(seq_start)
        def _():
            zbuf_ref[:, 0:SUBLANES, :] = jnp.zeros((_N_CONV_TILES, SUBLANES, CONV_TILE), F32)

    def project():
        return _dot(n_ref[...], w_ref[...])

    @pl.when(j == _J_FOX_Q)
    def _():
        qk_ref[...] = (project() * FOX_Q_SCALE).astype(BF16)
        logf = jax.nn.log_sigmoid(_dot(n_ref[...], wff_ref[...]) + fb_ref[...])
        lt = logf.T[0:BF16_ROWS, :]
        hi = lt.astype(BF16)
        r1 = lt - hi.astype(F32)
        mid = r1.astype(BF16)
        lo = (r1 - mid.astype(F32)).astype(BF16)
        src = lax.broadcasted_iota(jnp.int32, (tm, tm), 0)
        dst = lax.broadcasted_iota(jnp.int32, (tm, tm), 1)
        tri = jnp.where(src <= dst, 1.0, 0.0).astype(BF16)
        parts = _dot(jnp.concatenate([hi, mid, lo], axis=0), tri)
        prev = jnp.where(seq_start, 0.0, carry_ref[...])
        ct = (parts[0:BF16_ROWS] + parts[BF16_ROWS:2 * BF16_ROWS] + parts[2 * BF16_ROWS:]) + prev
        carry_ref[...] = ct[:, tm - 1:tm]
        c_ref[...] = jnp.concatenate([ct, jnp.zeros((LANES - BF16_ROWS, tm), F32)], axis=0).T
        for k in range(tm // ATT_TILE):
            crow_ref[k] = ct[:, k * ATT_TILE:(k + 1) * ATT_TILE]

    @pl.when(j == _J_FOX_K)
    def _():
        qk_ref[...] = project().astype(BF16)

    @pl.when(j == _J_FOX_V)
    def _():
        fvt_ref[...] = project().T.astype(BF16)

    @pl.when(j == _J_DIFF_V)
    def _():
        dvt_ref[...] = project().T.astype(BF16)

    for jj, q_scale in ((_J_DIFF_Q, DIFF_Q_SCALE), (_J_DIFF_K, None)):
        @pl.when(j == jj)
        def _(q_scale=q_scale):
            y = project()
            ct = ct_ref[...]
            st = st_ref[...]
            for c0 in range(0, IN_TILE, LANES):
                yc = y[:, c0:c0 + LANES]
                out = yc * ct + pltpu.roll(yc, LANES // 2, axis=1) * st
                if q_scale is not None:
                    out = out * q_scale
                qk_ref[:, c0:c0 + LANES] = out.astype(BF16)

    for t in range(_N_CONV_TILES):
        @pl.when(j == _J_CONV + t)
        def _(t=t):
            y = project()
            gb = y[:, 0:CONV_TILE]
            z = y[:, CONV_TILE:2 * CONV_TILE] * y[:, 2 * CONV_TILE:3 * CONV_TILE]
            zb = zbuf_ref.at[t]
            zb[SUBLANES:SUBLANES + tm, :] = z
            z1 = zb[SUBLANES - 1:SUBLANES - 1 + tm, :]
            z2 = zb[SUBLANES - 2:SUBLANES - 2 + tm, :]
            cw = cw_ref[...]
            conv = z2 * cw[0:1, :] + z1 * cw[1:2, :] + z * cw[2:3, :] + cb_ref[...]
            conv_ref[...] = (gb * conv).astype(BF16)
            zb[0:SUBLANES, :] = z[tm - SUBLANES:tm, :]


def _inproj(h, g, w, wff, fbias, ctab, stab, conv_w, conv_b, l, *, seq_len):
    n_tok, d = h.shape
    tm = TOKEN_TILE
    conv_col = lambda j: jnp.clip(j - _J_CONV, 0, _N_CONV_TILES - 1)
    qk_col = lambda j: jnp.minimum(j - (j >= _J_FOX_V).astype(jnp.int32)
                                   - (j >= _J_DIFF_V).astype(jnp.int32), 3)
    return pl.pallas_call(
        functools.partial(_inproj_body, tiles_per_seq=seq_len // tm),
        grid=(n_tok // tm, _N_IN_TILES),
        in_specs=[
            pl.BlockSpec((tm, d), lambda i, j: (i, 0)),
            _layer_spec((1, d), lambda i, j: (l, 0, 0)),
            _layer_spec((d, IN_TILE), lambda i, j: (l, 0, j)),
            _layer_spec((d, LANES), lambda i, j: (l, 0, 0)),
            _layer_spec((1, LANES), lambda i, j: (l, 0, 0)),
            pl.BlockSpec((tm, LANES), lambda i, j: (i, 0)),
            pl.BlockSpec((tm, LANES), lambda i, j: (i, 0)),
            _layer_spec((CONV_WIDTH, CONV_TILE), lambda i, j: (l, 0, conv_col(j))),
            _layer_spec((1, CONV_TILE), lambda i, j: (l, 0, conv_col(j))),
        ],
        out_specs=[
            pl.BlockSpec((tm, IN_TILE), lambda i, j: (i, qk_col(j))),
            pl.BlockSpec((tm, CONV_TILE), lambda i, j: (i, conv_col(j))),
            pl.BlockSpec((tm, LANES), lambda i, j: (i, 0)),
            pl.BlockSpec((tm // ATT_TILE, BF16_ROWS, ATT_TILE), lambda i, j: (i, 0, 0)),
            pl.BlockSpec((IN_TILE, tm), lambda i, j: (0, i)),
            pl.BlockSpec((IN_TILE, tm), lambda i, j: (0, i)),
        ],
        out_shape=[
            jax.ShapeDtypeStruct((n_tok, 4 * IN_TILE), BF16),
            jax.ShapeDtypeStruct((n_tok, CONV_CH), BF16),
            jax.ShapeDtypeStruct((n_tok, LANES), F32),
            jax.ShapeDtypeStruct((n_tok // ATT_TILE, BF16_ROWS, ATT_TILE), F32),
            jax.ShapeDtypeStruct((IN_TILE, n_tok), BF16),
            jax.ShapeDtypeStruct((IN_TILE, n_tok), BF16),
        ],
        scratch_shapes=[
            pltpu.VMEM((tm, d), BF16),
            pltpu.VMEM((_N_CONV_TILES, tm + SUBLANES, CONV_TILE), F32),
            pltpu.VMEM((BF16_ROWS, 1), F32),
        ],
        compiler_params=_params(2),
        name="inproj",
    )(h, g, w, wff, fbias, ctab, stab, conv_w, conv_b)


_T_QI, _T_KB, _T_DIAG, _T_STATE = 0, 1, 2, 3
PIPE_LAG = 2
PIPE_SLOTS = 2 * PIPE_LAG


def _item_table(nq):
    items = [(qi, kb) for qi in range(nq) for kb in range(qi + 1)]
    n_loops = -(-len(items) // PIPE_SLOTS)
    n_pos = PIPE_SLOTS * n_loops + 2 * PIPE_LAG
    tab = np.zeros((4, n_pos), np.int32)
    for pos in range(n_pos):
        item = pos - PIPE_LAG
        qi, kb = items[min(max(item, 0), len(items) - 1)]
        real = 0 <= item < len(items)
        tab[:, pos] = (qi, kb, int(real and kb == qi), qi if real else nq)
    return tab, n_loops


def _fill_bias(bias_ref, t):
    key = lax.broadcasted_iota(jnp.int32, (t, t), 0)
    qry = lax.broadcasted_iota(jnp.int32, (t, t), 1)
    bias_ref[0] = jnp.zeros((t, t), F32)
    bias_ref[1] = jnp.where(key <= qry, 0.0, NEG_INF)


def _stage_values(vt_ref, vte_ref, t):
    hd = vt_ref.shape[0]
    for kb in range(vte_ref.shape[0]):
        vte_ref[kb, 0:hd, :] = vt_ref[:, kb * t:(kb + 1) * t]
        vte_ref[kb, hd:, :] = jnp.ones((ONES_ROWS, t), BF16)


def _store_scores(u, u_ref, mx_ref):
    u_ref[...] = u
    mx_ref[...] = jnp.max(u, axis=0, keepdims=True)


def _softmax_stage(u_ref, mx_ref, shift, m_ref, p_ref, al_ref):
    m_old = m_ref[...]
    m_new = jnp.maximum(m_old, mx_ref[...] + shift)
    al_ref[...] = jnp.exp2(m_old - m_new)
    p_ref[...] = jnp.exp2(u_ref[...] - (m_new - shift)).astype(BF16)
    m_ref[...] = m_new


def _value_stage(vt, p_ref, al_ref, acc_ref):
    acc_ref[...] = al_ref[...] * acc_ref[...] + _dot(vt, p_ref[...])


def _fox_body(tab_ref, q_ref, k_ref, vt_ref, c_ref, crow_ref, o_ref,
              ckb_ref, vte_ref, bias_ref, u_ref, mx_ref, p_ref, al_ref, m_ref, acc_ref, *, n_loops):
    b = pl.program_id(0)
    h = pl.program_id(1)
    t = ATT_TILE
    hd = FOX_HEAD_DIM

    @pl.when((b == 0) & (h == 0))
    def _():
        _fill_bias(bias_ref, t)

    lane = lax.broadcasted_iota(jnp.int32, c_ref.shape, 1)
    ck = jnp.sum(jnp.where(lane == h, c_ref[...], 0.0), axis=-1, keepdims=True)
    ckb_ref[...] = jnp.broadcast_to(ck * LOG2E, ckb_ref.shape)
    _stage_values(vt_ref, vte_ref, t)
    p_ref[PIPE_LAG:] = jnp.zeros((PIPE_LAG, t, t), BF16)
    al_ref[PIPE_LAG:] = jnp.zeros((PIPE_LAG, 1, t), F32)
    acc_ref[...] = jnp.zeros_like(acc_ref)
    m_ref[...] = jnp.full_like(m_ref, NEG_INF)

    def rows(idx):
        return pl.ds(pl.multiple_of(idx * t, t), t)

    def score_stage(pos, slot):
        kb = tab_ref[_T_KB, pos]
        u = (_dot_t(k_ref[rows(kb), :], q_ref[rows(tab_ref[_T_QI, pos]), :])
             - jnp.tile(ckb_ref[rows(kb), :], (1, t // LANES)) + bias_ref[tab_ref[_T_DIAG, pos]])
        _store_scores(u, u_ref.at[slot], mx_ref.at[slot])

    def step(tau, slot):
        lagged = (slot + PIPE_LAG) % PIPE_SLOTS
        score_stage(tau + 2 * PIPE_LAG, lagged)
        pos = tau + PIPE_LAG
        cq = crow_ref[tab_ref[_T_QI, pos], pl.ds(h, 1), :] * LOG2E
        _softmax_stage(u_ref.at[slot], mx_ref.at[slot], cq, m_ref.at[tab_ref[_T_STATE, pos]],
                       p_ref.at[slot], al_ref.at[slot])
        value_stage(tau, lagged)

    def value_stage(tau, lagged):
        _value_stage(vte_ref[tab_ref[_T_KB, tau]], p_ref.at[lagged], al_ref.at[lagged],
                     acc_ref.at[tab_ref[_T_STATE, tau]])

    for item in range(PIPE_LAG):
        score_stage(item + PIPE_LAG, item)

    def unrolled(it, carry):
        for slot in range(PIPE_SLOTS):
            step(PIPE_SLOTS * it + slot, slot)
        return carry

    lax.fori_loop(0, n_loops, unrolled, 0)
    for tau in range(PIPE_SLOTS * n_loops, PIPE_SLOTS * n_loops + PIPE_LAG):
        value_stage(tau, (tau + PIPE_LAG) % PIPE_SLOTS)

    for qi in range(acc_ref.shape[0] - 1):
        acc = acc_ref[qi]
        o_ref[qi * t:(qi + 1) * t, :] = (acc[:hd, :] / acc[hd:hd + 1, :]).T.astype(BF16)


def _attention_scratch(t, nq, rows, n_streams):
    shape = lambda *s: ((n_streams,) if n_streams > 1 else ()) + s
    return [
        pltpu.VMEM((PIPE_SLOTS,) + shape(t, t), F32),
        pltpu.VMEM((PIPE_SLOTS,) + shape(1, t), F32),
        pltpu.VMEM((PIPE_SLOTS,) + shape(t, t), BF16),
        pltpu.VMEM((PIPE_SLOTS,) + shape(1, t), F32),
        pltpu.VMEM(shape(nq + 1, 1, t), F32),
        pltpu.VMEM(shape(nq + 1, rows, t), F32),
    ]


def _fox_attention(qk, vt, c, c_rows, *, batch, seq_len):
    t = ATT_TILE
    nq = seq_len // t
    hd = FOX_HEAD_DIM
    rows = hd + ONES_ROWS
    tab, n_loops = _item_table(nq)
    grid_spec = pltpu.PrefetchScalarGridSpec(
        num_scalar_prefetch=1,
        grid=(batch, FOX_HEADS),
        in_specs=[
            pl.BlockSpec((seq_len, hd), lambda b, h, tab: (b, h)),
            pl.BlockSpec((seq_len, hd), lambda b, h, tab: (b, FOX_HEADS + h)),
            pl.BlockSpec((hd, seq_len), lambda b, h, tab: (h, b)),
            pl.BlockSpec((seq_len, LANES), lambda b, h, tab: (b, 0)),
            pl.BlockSpec((nq, BF16_ROWS, t), lambda b, h, tab: (b, 0, 0)),
        ],
        out_specs=pl.BlockSpec((seq_len, hd), lambda b, h, tab: (b, h)),
        scratch_shapes=[pltpu.VMEM((seq_len, LANES), F32),
                        pltpu.VMEM((nq, rows, t), BF16),
                        pltpu.VMEM((2, t, t), F32)] + _attention_scratch(t, nq, rows, 1),
    )
    return pl.pallas_call(
        functools.partial(_fox_body, n_loops=n_loops),
        grid_spec=grid_spec,
        out_shape=jax.ShapeDtypeStruct((batch * seq_len, FOX_HEADS * hd), BF16),
        compiler_params=_params(2),
        name="fox_attention",
    )(jnp.asarray(tab), qk, qk, vt, c, c_rows)


def _diff_body(tab_ref, q_ref, k_ref, vt_ref, lam_ref, sg_ref, o_ref,
               vte_ref, bias_ref, u_ref, mx_ref, p_ref, al_ref, m_ref, acc_ref, *, n_loops, lam_init):
    b = pl.program_id(0)
    h = pl.program_id(1)
    t = ATT_TILE
    hd = DIFF_V_DIM

    @pl.when((b == 0) & (h == 0))
    def _():
        _fill_bias(bias_ref, t)

    _stage_values(vt_ref, vte_ref, t)
    p_ref[PIPE_LAG:] = jnp.zeros((PIPE_LAG, 2, t, t), BF16)
    al_ref[PIPE_LAG:] = jnp.zeros((PIPE_LAG, 2, 1, t), F32)
    acc_ref[...] = jnp.zeros_like(acc_ref)
    m_ref[...] = jnp.full_like(m_ref, NEG_INF)
    no_shift = jnp.zeros((1, t), F32)

    def rows(idx):
        return pl.ds(pl.multiple_of(idx * t, t), t)

    def score_stage(pos, slot):
        q = q_ref[rows(tab_ref[_T_QI, pos]), :]
        k = k_ref[rows(tab_ref[_T_KB, pos]), :]
        bias = bias_ref[tab_ref[_T_DIAG, pos]]
        lane = lax.broadcasted_iota(jnp.int32, q.shape, 1)
        comp0 = (lane < ROT_DIM // 2) | ((lane >= ROT_DIM) & (lane < LANES // 2 + ROT_DIM // 2))
        zero = jnp.zeros_like(q)
        for s, keep in enumerate((comp0, jnp.logical_not(comp0))):
            _store_scores(_dot_t(k, jnp.where(keep, q, zero)) + bias,
                          u_ref.at[slot, s], mx_ref.at[slot, s])

    def step(tau, slot):
        lagged = (slot + PIPE_LAG) % PIPE_SLOTS
        score_stage(tau + 2 * PIPE_LAG, lagged)
        state = tab_ref[_T_STATE, tau + PIPE_LAG]
        for s in range(2):
            _softmax_stage(u_ref.at[slot, s], mx_ref.at[slot, s], no_shift, m_ref.at[s, state],
                           p_ref.at[slot, s], al_ref.at[slot, s])
        value_stage(tau, lagged)

    def value_stage(tau, lagged):
        vt = vte_ref[tab_ref[_T_KB, tau]]
        for s in range(2):
            _value_stage(vt, p_ref.at[lagged, s], al_ref.at[lagged, s],
                         acc_ref.at[s, tab_ref[_T_STATE, tau]])

    for item in range(PIPE_LAG):
        score_stage(item + PIPE_LAG, item)

    def unrolled(it, carry):
        for slot in range(PIPE_SLOTS):
            step(PIPE_SLOTS * it + slot, slot)
        return carry

    lax.fori_loop(0, n_loops, unrolled, 0)
    for tau in range(PIPE_SLOTS * n_loops, PIPE_SLOTS * n_loops + PIPE_LAG):
        value_stage(tau, (tau + PIPE_LAG) % PIPE_SLOTS)

    lv = lam_ref[...]
    lam = (jnp.exp(jnp.sum(lv[0:1, :] * lv[1:2, :], axis=-1, keepdims=True))
           - jnp.exp(jnp.sum(lv[2:3, :] * lv[3:4, :], axis=-1, keepdims=True))
           + lam_init)
    for qi in range(acc_ref.shape[1] - 1):
        a1 = acc_ref[0, qi]
        a2 = acc_ref[1, qi]
        o = a1[:hd, :] / a1[hd:hd + 1, :] - lam * (a2[:hd, :] / a2[hd:hd + 1, :])
        o = o * lax.rsqrt(jnp.mean(o * o, axis=0, keepdims=True) + EPS)
        o_ref[qi * t:(qi + 1) * t, :] = (o.T * sg_ref[...] * (1.0 - lam_init)).astype(BF16)


def _diff_attention(qk, vt, lam_vecs, subln, l, *, batch, seq_len, lam_init):
    t = ATT_TILE
    nq = seq_len // t
    hd = DIFF_V_DIM
    rows = hd + ONES_ROWS
    base = 2 * FOX_HEADS
    tab, n_loops = _item_table(nq)
    grid_spec = pltpu.PrefetchScalarGridSpec(
        num_scalar_prefetch=1,
        grid=(batch, DIFF_HEADS),
        in_specs=[
            pl.BlockSpec((seq_len, hd), lambda b, h, tab: (b, base + h)),
            pl.BlockSpec((seq_len, hd), lambda b, h, tab: (b, base + DIFF_HEADS + h)),
            pl.BlockSpec((hd, seq_len), lambda b, h, tab: (h, b)),
            _layer_spec((4, DIFF_QK_DIM), lambda b, h, tab: (l, 0, 0)),
            _layer_spec((1, hd), lambda b, h, tab: (l, 0, 0)),
        ],
        out_specs=pl.BlockSpec((seq_len, hd), lambda b, h, tab: (b, h)),
        scratch_shapes=[pltpu.VMEM((nq, rows, t), BF16),
                        pltpu.VMEM((2, t, t), F32)] + _attention_scratch(t, nq, rows, 2),
    )
    return pl.pallas_call(
        functools.partial(_diff_body, n_loops=n_loops, lam_init=lam_init),
        grid_spec=grid_spec,
        out_shape=jax.ShapeDtypeStruct((batch * seq_len, DIFF_HEADS * hd), BF16),
        compiler_params=_params(2),
        name="diff_attention",
    )(jnp.asarray(tab), qk, qk, vt, lam_vecs, subln)


def _mixout_body(h_ref, fox_ref, diff_ref, conv_ref, wf_ref, wd_ref, wc_ref, o_ref):
    o_ref[...] = (h_ref[...] + _dot(fox_ref[...], wf_ref[...])
                  + _dot(diff_ref[...], wd_ref[...]) + _dot(conv_ref[...], wc_ref[...]))


def _mixout(h, fox, diff, conv, w_out, l):
    n_tok, d = h.shape
    tm = TOKEN_TILE
    fw, dw, cw = fox.shape[1], diff.shape[1], conv.shape[1]
    assert fw == dw and (fw + dw) % cw == 0
    row = lambda i: (i, 0)
    resident = lambda rows, blk: pl.BlockSpec((None, rows, d), lambda i: (l, blk, 0),
                                              pipeline_mode=pl.Buffered(1))
    return pl.pallas_call(
        _mixout_body,
        grid=(n_tok // tm,),
        in_specs=[
            pl.BlockSpec((tm, d), row),
            pl.BlockSpec((tm, fw), row),
            pl.BlockSpec((tm, dw), row),
            pl.BlockSpec((tm, cw), row),
            resident(fw, 0),
            resident(dw, 1),
            resident(cw, (fw + dw) // cw),
        ],
        out_specs=pl.BlockSpec((tm, d), row),
        out_shape=jax.ShapeDtypeStruct((n_tok, d), F32),
        compiler_params=_params(1),
        name="mixout",
    )(h, fox, diff, conv, w_out, w_out, w_out)


def _memkv_body(m_ref, g_ref, w_ref, o_ref):
    o_ref[...] = _dot(_rms(m_ref[...], g_ref[...]).astype(BF16), w_ref[...]).astype(BF16)


def _memkv(mem, g, w, l):
    n_mem, d = mem.shape
    tm = min(n_mem, TOKEN_TILE)
    width = w.shape[2]
    return pl.pallas_call(
        _memkv_body,
        grid=(n_mem // tm,),
        in_specs=[
            pl.BlockSpec((tm, d), lambda i: (i, 0)),
            _layer_spec((1, d), lambda i: (l, 0, 0)),
            pl.BlockSpec((None, d, width), lambda i: (l, 0, 0), pipeline_mode=pl.Buffered(1)),
        ],
        out_specs=pl.BlockSpec((tm, width), lambda i: (i, 0)),
        out_shape=jax.ShapeDtypeStruct((n_mem, width), BF16),
        compiler_params=_params(1),
        name="memkv",
    )(mem, g, w)


def _cross_body(h_ref, g_ref, wq_ref, kv_ref, wo_ref, o_ref):
    hd = CROSS_HEAD_DIM
    width = CROSS_HEADS * hd
    scale = hd ** -0.5
    x = h_ref[...]
    q = _dot(_rms(x, g_ref[...]).astype(BF16), wq_ref[...]).astype(BF16)
    heads = []
    for hh in range(CROSS_HEADS):
        k = kv_ref[:, hh * hd:(hh + 1) * hd]
        v = kv_ref[:, width + hh * hd:width + (hh + 1) * hd]
        s = _dot_t(q[:, hh * hd:(hh + 1) * hd], k) * scale
        e = jnp.exp(s - jnp.max(s, axis=-1, keepdims=True))
        p = e / jnp.sum(e, axis=-1, keepdims=True)
        heads.append(_dot(p.astype(BF16), v))
    o = jnp.concatenate(heads, axis=-1).astype(BF16)
    o_ref[...] = x + _dot(o, wo_ref[...])


def _cross(h, g, wq, kv, wo, l, *, seq_len, mem_len):
    n_tok, d = h.shape
    tm = TOKEN_TILE
    tiles_per_seq = seq_len // tm
    resident = lambda a: pl.BlockSpec((None,) + a.shape[1:], lambda i: (l, 0, 0),
                                      pipeline_mode=pl.Buffered(1))
    return pl.pallas_call(
        _cross_body,
        grid=(n_tok // tm,),
        in_specs=[
            pl.BlockSpec((tm, d), lambda i: (i, 0)),
            _layer_spec((1, d), lambda i: (l, 0, 0)),
            resident(wq),
            pl.BlockSpec((mem_len, kv.shape[1]), lambda i: (i // tiles_per_seq, 0)),
            resident(wo),
        ],
        out_specs=pl.BlockSpec((tm, d), lambda i: (i, 0)),
        out_shape=jax.ShapeDtypeStruct((n_tok, d), F32),
        compiler_params=_params(1),
        name="cross_attention",
    )(h, g, wq, kv, wo)


def _rope_tables(positions):
    half = ROT_DIM // 2
    inv_freq = ROPE_THETA ** (-jnp.arange(0, ROT_DIM, 2, dtype=F32) / ROT_DIM)
    ang = positions.astype(F32)[..., None] * inv_freq
    cos = jnp.cos(ang).reshape(-1, half)
    sin = jnp.sin(ang).reshape(-1, half)
    lane = np.arange(LANES)
    rotated = (lane % (LANES // 2)) < ROT_DIM
    sign = np.where(lane < LANES // 2, -1.0, 1.0).astype(np.float32)
    reps = LANES // half
    ct = jnp.where(rotated[None, :], jnp.tile(cos, (1, reps)), 1.0)
    st = jnp.where(rotated[None, :], jnp.tile(sin, (1, reps)) * sign[None, :], 0.0)
    return ct, st


def _diff_head_lanes():
    half = ROT_DIM // 2
    src = np.zeros(LANES, np.int64)
    comp0 = np.zeros(LANES, bool)
    for c in range(2):
        for d in range(DIFF_QK_DIM):
            if d < half:
                lane = c * half + d
            elif d < ROT_DIM:
                lane = LANES // 2 + c * half + (d - half)
            else:
                lane = (ROT_DIM if c == 0 else LANES // 2 + ROT_DIM) + (d - ROT_DIM)
            src[lane] = c * DIFF_QK_DIM + d
            comp0[lane] = c == 0
    return src, comp0


def _in_weight_columns():
    fw, dw, cc = FOX_HEADS * FOX_HEAD_DIM, DIFF_HEADS * DIFF_V_DIM, CONV_CH
    ff0 = 3 * fw
    d0 = ff0 + FOX_HEADS
    c0 = d0 + 3 * dw
    src, _ = _diff_head_lanes()
    head_perm = np.concatenate([h * LANES + src for h in range(DIFF_HEADS)])
    cols = [np.arange(3 * fw), d0 + head_perm, d0 + dw + head_perm, d0 + 2 * dw + np.arange(dw)]
    for t in range(_N_CONV_TILES):
        cols += [c0 + k * cc + t * CONV_TILE + np.arange(CONV_TILE) for k in range(3)]
    return np.concatenate(cols), ff0


def _runs(cols):
    cuts = np.flatnonzero(np.diff(cols) != 1) + 1
    return [(int(c[0]), int(c[-1]) + 1) for c in np.split(cols, cuts)]


def _arrange_body(x_ref, w_ref, wff_ref, *, runs, ff0):
    x = x_ref[...]
    w_ref[...] = jnp.concatenate([x[:, a:b] for a, b in runs], axis=-1).astype(BF16)
    ff = x[:, ff0:ff0 + FOX_HEADS]
    pad = jnp.zeros((x.shape[0], LANES - FOX_HEADS), F32)
    wff_ref[...] = jnp.concatenate([ff, pad], axis=-1).astype(BF16)


def _arrange_in_weights(w_in):
    n_layers, d, width = w_in.shape
    cols, ff0 = _in_weight_columns()
    rb = PREP_ROWS
    return pl.pallas_call(
        functools.partial(_arrange_body, runs=_runs(cols), ff0=ff0),
        grid=(n_layers, d // rb),
        in_specs=[_layer_spec((rb, width), lambda l, i: (l, i, 0))],
        out_specs=[_layer_spec((rb, cols.size), lambda l, i: (l, i, 0)),
                   _layer_spec((rb, LANES), lambda l, i: (l, i, 0))],
        out_shape=[jax.ShapeDtypeStruct((n_layers, d, cols.size), BF16),
                   jax.ShapeDtypeStruct((n_layers, d, LANES), BF16)],
        compiler_params=_params(2),
        name="arrange_in_weights",
    )(w_in)


def kernel(x, mem, positions, ffn1_norm, ffn1_w_gate, ffn1_w_up, ffn1_w_down, mix_norm, mix_w_in, forget_bias, conv_w, conv_b, lambda_q1, lambda_k1, lambda_q2, lambda_k2, diff_subln, mix_w_out, cross_norm, mem_norm, cross_w_q, cross_w_kv, cross_w_o, ffn2_norm, ffn2_w_gate, ffn2_w_up, ffn2_w_down, final_norm):
    batch, seq_len, d = x.shape
    mem_len = mem.shape[1]
    depth = ffn1_norm.shape[0]
    n_tok = batch * seq_len
    assert seq_len % TOKEN_TILE == 0 and seq_len % ATT_TILE == 0
    assert ffn1_w_gate.shape[2] % FF_TILE == 0

    bf = lambda a: a.astype(BF16)
    rows3 = lambda a: a.astype(F32).reshape(a.shape[0], 1, a.shape[1])
    ffn1_w = (ffn1_w_gate, ffn1_w_up, ffn1_w_down)
    ffn2_w = (ffn2_w_gate, ffn2_w_up, ffn2_w_down)
    ffn_w = tuple(bf(a[0]) for a in ffn1_w)
    w_in, w_ff = _arrange_in_weights(mix_w_in)
    w_out = bf(mix_w_out)
    wq, wkv, wo = bf(cross_w_q), bf(cross_w_kv), bf(cross_w_o)
    g_ffn1, g_mix, g_cross, g_mem, g_ffn2 = (rows3(a) for a in (ffn1_norm, mix_norm, cross_norm,
                                                                 mem_norm, ffn2_norm))
    g_final = final_norm.astype(F32).reshape(1, 1, d)
    fbias = rows3(jnp.pad(forget_bias, ((0, 0), (0, LANES - FOX_HEADS))))
    cbias = rows3(conv_b)
    subln = rows3(diff_subln)
    lam_vecs = jnp.stack([lambda_q1, lambda_k1, lambda_q2, lambda_k2], axis=1).astype(F32)
    ctab, stab = _rope_tables(positions)

    h = x.reshape(n_tok, d)
    mem2 = mem.reshape(batch * mem_len, d)
    nq = seq_len // ATT_TILE
    for l in range(depth):
        h, ffn_w = _ffn(h, g_ffn1, ffn_w, g_final, l, ffn2_w + (l,))

        qk, conv, c, c_rows, fox_vt, diff_vt = _inproj(h, g_mix, w_in, w_ff, fbias, ctab, stab,
                                                       conv_w.astype(F32), cbias, l, seq_len=seq_len)
        fox = _fox_attention(qk, fox_vt, c, c_rows, batch=batch, seq_len=seq_len)
        lam_init = 0.8 - 0.6 * math.exp(-0.3 * l)
        diff = _diff_attention(qk, diff_vt, lam_vecs, subln, l, batch=batch, seq_len=seq_len,
                               lam_init=lam_init)
        h = _mixout(h, fox, diff, conv, w_out, l)

        kv = _memkv(mem2, g_mem, wkv, l)
        h = _cross(h, g_cross, wq, kv, wo, l, seq_len=seq_len, mem_len=mem_len)

        last = l == depth - 1
        h, ffn_w = _ffn(h, g_ffn2, ffn_w, g_final, l, None if last else ffn1_w + (l + 1,), final=last)
    return h.reshape(batch, seq_len, d)
```

```python
import functools
import math

import numpy as np
import jax
import jax.numpy as jnp
from jax import lax
from jax.experimental import pallas as pl
from jax.experimental.pallas import tpu as pltpu

F32 = jnp.float32
BF16 = jnp.bfloat16

FOX_HEADS = 6
FOX_HEAD_DIM = 128
DIFF_HEADS = 6
DIFF_QK_DIM = 64
DIFF_V_DIM = 128
CONV_CH = 512
CONV_WIDTH = 3
ROT_DIM = 16
ROPE_THETA = 500000.0
CROSS_HEADS = 4
CROSS_HEAD_DIM = 128
EPS = 1e-6
NEG_INF = -1e30
LOG2E = math.log2(math.e)
FOX_Q_SCALE = FOX_HEAD_DIM ** -0.5 * LOG2E
DIFF_Q_SCALE = DIFF_QK_DIM ** -0.5 * LOG2E

LANES = 128
SUBLANES = 8
BF16_ROWS = 16

TOKEN_TILE = 1024
FF_TILE = 512
IN_TILE = 768
CONV_TILE = 256
ATT_TILE = 512
PREP_ROWS = 256
ONES_ROWS = BF16_ROWS
VMEM_LIMIT = 56 * 1024 * 1024
BIG_VMEM_LIMIT = 60 * 1024 * 1024

_ARB = "arbitrary"


def _params(n_axes, vmem_limit=VMEM_LIMIT):
    return pltpu.CompilerParams(dimension_semantics=(_ARB,) * n_axes,
                                vmem_limit_bytes=vmem_limit)


def _rms(x, g):
    return x * lax.rsqrt(jnp.mean(x * x, axis=-1, keepdims=True) + EPS) * g


def _dot(a, b):
    return jnp.dot(a, b, preferred_element_type=F32)


def _dot_t(a, b):
    return lax.dot_general(a, b, (((1,), (1,)), ((), ())), preferred_element_type=F32)


def _layer_spec(shape, index_map):
    return pl.BlockSpec((None,) + tuple(shape), index_map)


def _ffn_body(*refs, final, convert_next):
    x_ref, g_ref, wg_ref, wu_ref, wd_ref, fg_ref = refs[:6]
    if convert_next:
        src_refs, o_ref, dst_refs, n_ref = refs[6:9], refs[9], refs[10:13], refs[13]
        for src_ref, dst_ref in zip(src_refs, dst_refs):
            dst_ref[...] = src_ref[...].astype(BF16)
    else:
        o_ref, n_ref = refs[6:]
    j = pl.program_id(1)

    @pl.when(j == 0)
    def _():
        x = x_ref[...]
        n_ref[...] = _rms(x, g_ref[...]).astype(BF16)
        o_ref[...] = x

    n = n_ref[...]
    gate = _dot(n, wg_ref[...])
    up = _dot(n, wu_ref[...])
    act = (gate * jax.nn.sigmoid(gate) * (0.5 * up)).astype(BF16)
    o_ref[...] += _dot(act, wd_ref[...])

    if final:
        @pl.when(j == pl.num_programs(1) - 1)
        def _():
            o_ref[...] = _rms(o_ref[...], fg_ref[...])


def _ffn(h, g, weights, final_g, l, next_weights=None, *, final=False):
    n_tok, d = h.shape
    wg, wu, wd = weights
    d_ff = wg.shape[1]
    tm, tf = TOKEN_TILE, FF_TILE
    ni, nj = n_tok // tm, d_ff // tf
    in_specs = [
        pl.BlockSpec((tm, d), lambda i, j: (i, 0)),
        _layer_spec((1, d), lambda i, j: (l, 0, 0)),
        pl.BlockSpec((d, tf), lambda i, j: (0, j)),
        pl.BlockSpec((d, tf), lambda i, j: (0, j)),
        pl.BlockSpec((tf, d), lambda i, j: (j, 0)),
        _layer_spec((1, d), lambda i, j: (0, 0, 0)),
    ]
    out_specs = [pl.BlockSpec((tm, d), lambda i, j: (i, 0))]
    out_shape = [jax.ShapeDtypeStruct((n_tok, d), F32)]
    operands = [h, g, wg, wu, wd, final_g]
    convert_next = next_weights is not None
    if convert_next:
        ng, nu, nd, nl = next_weights
        assert d % ni == 0 and (d // ni) % BF16_ROWS == 0
        dr = d // ni
        in_specs += [_layer_spec((dr, tf), lambda i, j: (nl, i, j)),
                     _layer_spec((dr, tf), lambda i, j: (nl, i, j)),
                     _layer_spec((tf, dr), lambda i, j: (nl, j, i))]
        out_specs += [pl.BlockSpec((dr, tf), lambda i, j: (i, j)),
                      pl.BlockSpec((dr, tf), lambda i, j: (i, j)),
                      pl.BlockSpec((tf, dr), lambda i, j: (j, i))]
        out_shape += [jax.ShapeDtypeStruct(a.shape[1:], BF16) for a in (ng, nu, nd)]
        operands += [ng, nu, nd]
    outs = pl.pallas_call(
        functools.partial(_ffn_body, final=final, convert_next=convert_next),
        grid=(ni, nj),
        in_specs=in_specs,
        out_specs=out_specs,
        out_shape=out_shape,
        scratch_shapes=[pltpu.VMEM((tm, d), BF16)],
        compiler_params=_params(2, BIG_VMEM_LIMIT),
        name="ffn_final" if final else "ffn",
    )(*operands)
    return outs[0], (tuple(outs[1:]) if convert_next else None)


_J_FOX_QK, _J_VALUES, _J_DIFF_QK, _J_CONV = 0, 1, 2, 3
_N_CONV_TILES = CONV_CH // CONV_TILE
_N_IN_PAIRS = 4
assert _N_CONV_TILES == 2


def _inproj_body(x_ref, g_ref, w_ref, wff_ref, fb_ref, ct_ref, st_ref, cw_ref, cb_ref,
                 qk_ref, conv_ref, c_ref, crow_ref, fvt_ref, dvt_ref, n_ref, zbuf_ref, carry_ref,
                 *, tiles_per_seq):
    i = pl.program_id(0)
    j = pl.program_id(1)
    tm = x_ref.shape[0]
    seq_start = (i % tiles_per_seq) == 0

    @pl.when(j == 0)
    def _():
        n_ref[...] = _rms(x_ref[...], g_ref[...]).astype(BF16)

        @pl.when(seq_start)
        def _():
            zbuf_ref[:, 0:SUBLANES, :] = jnp.zeros((_N_CONV_TILES, SUBLANES, CONV_TILE), F32)

    def project(half):
        return _dot(n_ref[...], w_ref[:, half * IN_TILE:(half + 1) * IN_TILE])

    @pl.when(j == _J_FOX_QK)
    def _():
        qk_ref[:, 0:IN_TILE] = (project(0) * FOX_Q_SCALE).astype(BF16)
        qk_ref[:, IN_TILE:] = project(1).astype(BF16)
        logf = jax.nn.log_sigmoid(_dot(n_ref[...], wff_ref[...]) + fb_ref[...])
        lt = logf.T[0:BF16_ROWS, :]
        hi = lt.astype(BF16)
        r1 = lt - hi.astype(F32)
        mid = r1.astype(BF16)
        lo = (r1 - mid.astype(F32)).astype(BF16)
        src = lax.broadcasted_iota(jnp.int32, (tm, tm), 0)
        dst = lax.broadcasted_iota(jnp.int32, (tm, tm), 1)
        tri = jnp.where(src <= dst, 1.0, 0.0).astype(BF16)
        parts = _dot(jnp.concatenate([hi, mid, lo], axis=0), tri)
        prev = jnp.where(seq_start, 0.0, carry_ref[...])
        ct = (parts[0:BF16_ROWS] + parts[BF16_ROWS:2 * BF16_ROWS] + parts[2 * BF16_ROWS:]) + prev
        carry_ref[...] = ct[:, tm - 1:tm]
        c_ref[...] = jnp.concatenate([ct, jnp.zeros((LANES - BF16_ROWS, tm), F32)], axis=0).T
        for k in range(tm // ATT_TILE):
            crow_ref[k] = ct[:, k * ATT_TILE:(k + 1) * ATT_TILE]

    @pl.when(j == _J_VALUES)
    def _():
        fvt_ref[...] = project(0).T.astype(BF16)
        dvt_ref[...] = project(1).T.astype(BF16)

    @pl.when(j == _J_DIFF_QK)
    def _():
        ct = ct_ref[...]
        st = st_ref[...]
        for half, q_scale in ((0, DIFF_Q_SCALE), (1, None)):
            y = project(half)
            for c0 in range(0, IN_TILE, LANES):
                yc = y[:, c0:c0 + LANES]
                out = yc * ct + pltpu.roll(yc, LANES // 2, axis=1) * st
                if q_scale is not None:
                    out = out * q_scale
                qk_ref[:, half * IN_TILE + c0:half * IN_TILE + c0 + LANES] = out.astype(BF16)

    @pl.when(j == _J_CONV)
    def _():
        cw = cw_ref[...]
        cb = cb_ref[...]
        for t in range(_N_CONV_TILES):
            ch = slice(t * CONV_TILE, (t + 1) * CONV_TILE)
            y = project(t)
            gb = y[:, 0:CONV_TILE]
            z = y[:, CONV_TILE:2 * CONV_TILE] * y[:, 2 * CONV_TILE:3 * CONV_TILE]
            zb = zbuf_ref.at[t]
            zb[SUBLANES:SUBLANES + tm, :] = z
            z1 = zb[SUBLANES - 1:SUBLANES - 1 + tm, :]
            z2 = zb[SUBLANES - 2:SUBLANES - 2 + tm, :]
            conv = z2 * cw[0:1, ch] + z1 * cw[1:2, ch] + z * cw[2:3, ch] + cb[:, ch]
            conv_ref[:, ch] = (gb * conv).astype(BF16)
            zb[0:SUBLANES, :] = z[tm - SUBLANES:tm, :]


def _inproj(h, g, w, wff, fbias, ctab, stab, conv_w, conv_b, l, *, seq_len):
    n_tok, d = h.shape
    tm = TOKEN_TILE
    pair = 2 * IN_TILE
    return pl.pallas_call(
        functools.partial(_inproj_body, tiles_per_seq=seq_len // tm),
        grid=(n_tok // tm, _N_IN_PAIRS),
        in_specs=[
            pl.BlockSpec((tm, d), lambda i, j: (i, 0)),
            _layer_spec((1, d), lambda i, j: (l, 0, 0)),
            _layer_spec((d, pair), lambda i, j: (l, 0, j)),
            _layer_spec((d, LANES), lambda i, j: (l, 0, 0)),
            _layer_spec((1, LANES), lambda i, j: (l, 0, 0)),
            pl.BlockSpec((tm, LANES), lambda i, j: (i, 0)),
            pl.BlockSpec((tm, LANES), lambda i, j: (i, 0)),
            _layer_spec((CONV_WIDTH, CONV_CH), lambda i, j: (l, 0, 0)),
            _layer_spec((1, CONV_CH), lambda i, j: (l, 0, 0)),
        ],
        out_specs=[
            pl.BlockSpec((tm, pair), lambda i, j: (i, j // 2)),
            pl.BlockSpec((tm, CONV_CH), lambda i, j: (i, 0)),
            pl.BlockSpec((tm, LANES), lambda i, j: (i, 0)),
            pl.BlockSpec((tm // ATT_TILE, BF16_ROWS, ATT_TILE), lambda i, j: (i, 0, 0)),
            pl.BlockSpec((IN_TILE, tm), lambda i, j: (0, i)),
            pl.BlockSpec((IN_TILE, tm), lambda i, j: (0, i)),
        ],
        out_shape=[
            jax.ShapeDtypeStruct((n_tok, 2 * pair), BF16),
            jax.ShapeDtypeStruct((n_tok, CONV_CH), BF16),
            jax.ShapeDtypeStruct((n_tok, LANES), F32),
            jax.ShapeDtypeStruct((n_tok // ATT_TILE, BF16_ROWS, ATT_TILE), F32),
            jax.ShapeDtypeStruct((IN_TILE, n_tok), BF16),
            jax.ShapeDtypeStruct((IN_TILE, n_tok), BF16),
        ],
        scratch_shapes=[
            pltpu.VMEM((tm, d), BF16),
            pltpu.VMEM((_N_CONV_TILES, tm + SUBLANES, CONV_TILE), F32),
            pltpu.VMEM((BF16_ROWS, 1), F32),
        ],
        compiler_params=_params(2, BIG_VMEM_LIMIT),
        name="inproj",
    )(h, g, w, wff, fbias, ctab, stab, conv_w, conv_b)


_T_QI, _T_KB, _T_DIAG, _T_STATE = 0, 1, 2, 3
PIPE_LAG = 2
PIPE_SLOTS = 2 * PIPE_LAG


def _item_table(nq):
    items = [(qi, kb) for qi in range(nq) for kb in range(qi + 1)]
    n_loops = -(-len(items) // PIPE_SLOTS)
    n_pos = PIPE_SLOTS * n_loops + 2 * PIPE_LAG
    tab = np.zeros((4, n_pos), np.int32)
    for pos in range(n_pos):
        item = pos - PIPE_LAG
        qi, kb = items[min(max(item, 0), len(items) - 1)]
        real = 0 <= item < len(items)
        tab[:, pos] = (qi, kb, int(real and kb == qi), qi if real else nq)
    return tab, n_loops


def _fill_bias(bias_ref, t):
    key = lax.broadcasted_iota(jnp.int32, (t, t), 0)
    qry = lax.broadcasted_iota(jnp.int32, (t, t), 1)
    bias_ref[0] = jnp.zeros((t, t), F32)
    bias_ref[1] = jnp.where(key <= qry, 0.0, NEG_INF)


def _stage_values(vt_ref, vte_ref, t):
    hd = vt_ref.shape[0]
    for kb in range(vte_ref.shape[0]):
        vte_ref[kb, 0:hd, :] = vt_ref[:, kb * t:(kb + 1) * t]
        vte_ref[kb, hd:, :] = jnp.ones((ONES_ROWS, t), BF16)


def _store_scores(u, u_ref, mx_ref):
    u_ref[...] = u
    mx_ref[...] = jnp.max(u, axis=0, keepdims=True)


def _softmax_stage(u_ref, mx_ref, shift, m_ref, p_ref, al_ref):
    m_old = m_ref[...]
    m_new = jnp.maximum(m_old, mx_ref[...] + shift)
    al_ref[...] = jnp.exp2(m_old - m_new)
    p_ref[...] = jnp.exp2(u_ref[...] - (m_new - shift)).astype(BF16)
    m_ref[...] = m_new


def _value_stage(vt, p_ref, al_ref, acc_ref):
    acc_ref[...] = al_ref[...] * acc_ref[...] + _dot(vt, p_ref[...])


def _fox_body(tab_ref, q_ref, k_ref, vt_ref, c_ref, crow_ref, o_ref,
              ckb_ref, vte_ref, bias_ref, u_ref, mx_ref, p_ref, al_ref, m_ref, acc_ref, *, n_loops):
    b = pl.program_id(0)
    h = pl.program_id(1)
    t = ATT_TILE
    hd = FOX_HEAD_DIM

    @pl.when((b == 0) & (h == 0))
    def _():
        _fill_bias(bias_ref, t)

    lane = lax.broadcasted_iota(jnp.int32, c_ref.shape, 1)
    ck = jnp.sum(jnp.where(lane == h, c_ref[...], 0.0), axis=-1, keepdims=True)
    ckb_ref[...] = jnp.broadcast_to(ck * LOG2E, ckb_ref.shape)
    _stage_values(vt_ref, vte_ref, t)
    p_ref[PIPE_LAG:] = jnp.zeros((PIPE_LAG, t, t), BF16)
    al_ref[PIPE_LAG:] = jnp.zeros((PIPE_LAG, 1, t), F32)
    acc_ref[...] = jnp.zeros_like(acc_ref)
    m_ref[...] = jnp.full_like(m_ref, NEG_INF)

    def rows(idx):
        return pl.ds(pl.multiple_of(idx * t, t), t)

    def score_stage(pos, slot):
        kb = tab_ref[_T_KB, pos]
        u = (_dot_t(k_ref[rows(kb), :], q_ref[rows(tab_ref[_T_QI, pos]), :])
             - jnp.tile(ckb_ref[rows(kb), :], (1, t // LANES)) + bias_ref[tab_ref[_T_DIAG, pos]])
        _store_scores(u, u_ref.at[slot], mx_ref.at[slot])

    def step(tau, slot):
        lagged = (slot + PIPE_LAG) % PIPE_SLOTS
        score_stage(tau + 2 * PIPE_LAG, lagged)
        pos = tau + PIPE_LAG
        cq = crow_ref[tab_ref[_T_QI, pos], pl.ds(h, 1), :] * LOG2E
        _softmax_stage(u_ref.at[slot], mx_ref.at[slot], cq, m_ref.at[tab_ref[_T_STATE, pos]],
                       p_ref.at[slot], al_ref.at[slot])
        value_stage(tau, lagged)

    def value_stage(tau, lagged):
        _value_stage(vte_ref[tab_ref[_T_KB, tau]], p_ref.at[lagged], al_ref.at[lagged],
                     acc_ref.at[tab_ref[_T_STATE, tau]])

    for item in range(PIPE_LAG):
        score_stage(item + PIPE_LAG, item)

    def unrolled(it, carry):
        for slot in range(PIPE_SLOTS):
            step(PIPE_SLOTS * it + slot, slot)
        return carry

    lax.fori_loop(0, n_loops, unrolled, 0)
    for tau in range(PIPE_SLOTS * n_loops, PIPE_SLOTS * n_loops + PIPE_LAG):
        value_stage(tau, (tau + PIPE_LAG) % PIPE_SLOTS)

    for qi in range(acc_ref.shape[0] - 1):
        acc = acc_ref[qi]
        o_ref[qi * t:(qi + 1) * t, :] = (acc[:hd, :] / acc[hd:hd + 1, :]).T.astype(BF16)


def _attention_scratch(t, nq, rows, n_streams):
    shape = lambda *s: ((n_streams,) if n_streams > 1 else ()) + s
    return [
        pltpu.VMEM((PIPE_SLOTS,) + shape(t, t), F32),
        pltpu.VMEM((PIPE_SLOTS,) + shape(1, t), F32),
        pltpu.VMEM((PIPE_SLOTS,) + shape(t, t), BF16),
        pltpu.VMEM((PIPE_SLOTS,) + shape(1, t), F32),
        pltpu.VMEM(shape(nq + 1, 1, t), F32),
        pltpu.VMEM(shape(nq + 1, rows, t), F32),
    ]


def _fox_attention(qk, vt, c, c_rows, *, batch, seq_len):
    t = ATT_TILE
    nq = seq_len // t
    hd = FOX_HEAD_DIM
    rows = hd + ONES_ROWS
    tab, n_loops = _item_table(nq)
    grid_spec = pltpu.PrefetchScalarGridSpec(
        num_scalar_prefetch=1,
        grid=(batch, FOX_HEADS),
        in_specs=[
            pl.BlockSpec((seq_len, hd), lambda b, h, tab: (b, h)),
            pl.BlockSpec((seq_len, hd), lambda b, h, tab: (b, FOX_HEADS + h)),
            pl.BlockSpec((hd, seq_len), lambda b, h, tab: (h, b)),
            pl.BlockSpec((seq_len, LANES), lambda b, h, tab: (b, 0)),
            pl.BlockSpec((nq, BF16_ROWS, t), lambda b, h, tab: (b, 0, 0)),
        ],
        out_specs=pl.BlockSpec((seq_len, hd), lambda b, h, tab: (b, h)),
        scratch_shapes=[pltpu.VMEM((seq_len, LANES), F32),
                        pltpu.VMEM((nq, rows, t), BF16),
                        pltpu.VMEM((2, t, t), F32)] + _attention_scratch(t, nq, rows, 1),
    )
    return pl.pallas_call(
        functools.partial(_fox_body, n_loops=n_loops),
        grid_spec=grid_spec,
        out_shape=jax.ShapeDtypeStruct((batch * seq_len, FOX_HEADS * hd), BF16),
        compiler_params=_params(2),
        name="fox_attention",
    )(jnp.asarray(tab), qk, qk, vt, c, c_rows)


def _diff_body(tab_ref, q_ref, k_ref, vt_ref, lam_ref, sg_ref, o_ref,
               vte_ref, bias_ref, u_ref, mx_ref, p_ref, al_ref, m_ref, acc_ref, *, n_loops, lam_init):
    b = pl.program_id(0)
    h = pl.program_id(1)
    t = ATT_TILE
    hd = DIFF_V_DIM

    @pl.when((b == 0) & (h == 0))
    def _():
        _fill_bias(bias_ref, t)

    _stage_values(vt_ref, vte_ref, t)
    p_ref[PIPE_LAG:] = jnp.zeros((PIPE_LAG, 2, t, t), BF16)
    al_ref[PIPE_LAG:] = jnp.zeros((PIPE_LAG, 2, 1, t), F32)
    acc_ref[...] = jnp.zeros_like(acc_ref)
    m_ref[...] = jnp.full_like(m_ref, NEG_INF)
    no_shift = jnp.zeros((1, t), F32)

    def rows(idx):
        return pl.ds(pl.multiple_of(idx * t, t), t)

    def score_stage(pos, slot):
        q = q_ref[rows(tab_ref[_T_QI, pos]), :]
        k = k_ref[rows(tab_ref[_T_KB, pos]), :]
        bias = bias_ref[tab_ref[_T_DIAG, pos]]
        lane = lax.broadcasted_iota(jnp.int32, q.shape, 1)
        comp0 = (lane < ROT_DIM // 2) | ((lane >= ROT_DIM) & (lane < LANES // 2 + ROT_DIM // 2))
        zero = jnp.zeros_like(q)
        for s, keep in enumerate((comp0, jnp.logical_not(comp0))):
            _store_scores(_dot_t(k, jnp.where(keep, q, zero)) + bias,
                          u_ref.at[slot, s], mx_ref.at[slot, s])

    def step(tau, slot):
        lagged = (slot + PIPE_LAG) % PIPE_SLOTS
        score_stage(tau + 2 * PIPE_LAG, lagged)
        state = tab_ref[_T_STATE, tau + PIPE_LAG]
        for s in range(2):
            _softmax_stage(u_ref.at[slot, s], mx_ref.at[slot, s], no_shift, m_ref.at[s, state],
                           p_ref.at[slot, s], al_ref.at[slot, s])
        value_stage(tau, lagged)

    def value_stage(tau, lagged):
        vt = vte_ref[tab_ref[_T_KB, tau]]
        for s in range(2):
            _value_stage(vt, p_ref.at[lagged, s], al_ref.at[lagged, s],
                         acc_ref.at[s, tab_ref[_T_STATE, tau]])

    for item in range(PIPE_LAG):
        score_stage(item + PIPE_LAG, item)

    def unrolled(it, carry):
        for slot in range(PIPE_SLOTS):
            step(PIPE_SLOTS * it + slot, slot)
        return carry

    lax.fori_loop(0, n_loops, unrolled, 0)
    for tau in range(PIPE_SLOTS * n_loops, PIPE_SLOTS * n_loops + PIPE_LAG):
        value_stage(tau, (tau + PIPE_LAG) % PIPE_SLOTS)

    lv = lam_ref[...]
    lam = (jnp.exp(jnp.sum(lv[0:1, :] * lv[1:2, :], axis=-1, keepdims=True))
           - jnp.exp(jnp.sum(lv[2:3, :] * lv[3:4, :], axis=-1, keepdims=True))
           + lam_init)
    for qi in range(acc_ref.shape[1] - 1):
        a1 = acc_ref[0, qi]
        a2 = acc_ref[1, qi]
        o = a1[:hd, :] / a1[hd:hd + 1, :] - lam * (a2[:hd, :] / a2[hd:hd + 1, :])
        o = o * lax.rsqrt(jnp.mean(o * o, axis=0, keepdims=True) + EPS)
        o_ref[qi * t:(qi + 1) * t, :] = (o.T * sg_ref[...] * (1.0 - lam_init)).astype(BF16)


def _diff_attention(qk, vt, lam_vecs, subln, l, *, batch, seq_len, lam_init):
    t = ATT_TILE
    nq = seq_len // t
    hd = DIFF_V_DIM
    rows = hd + ONES_ROWS
    base = 2 * FOX_HEADS
    tab, n_loops = _item_table(nq)
    grid_spec = pltpu.PrefetchScalarGridSpec(
        num_scalar_prefetch=1,
        grid=(batch, DIFF_HEADS),
        in_specs=[
            pl.BlockSpec((seq_len, hd), lambda b, h, tab: (b, base + h)),
            pl.BlockSpec((seq_len, hd), lambda b, h, tab: (b, base + DIFF_HEADS + h)),
            pl.BlockSpec((hd, seq_len), lambda b, h, tab: (h, b)),
            _layer_spec((4, DIFF_QK_DIM), lambda b, h, tab: (l, 0, 0)),
            _layer_spec((1, hd), lambda b, h, tab: (l, 0, 0)),
        ],
        out_specs=pl.BlockSpec((seq_len, hd), lambda b, h, tab: (b, h)),
        scratch_shapes=[pltpu.VMEM((nq, rows, t), BF16),
                        pltpu.VMEM((2, t, t), F32)] + _attention_scratch(t, nq, rows, 2),
    )
    return pl.pallas_call(
        functools.partial(_diff_body, n_loops=n_loops, lam_init=lam_init),
        grid_spec=grid_spec,
        out_shape=jax.ShapeDtypeStruct((batch * seq_len, DIFF_HEADS * hd), BF16),
        compiler_params=_params(2),
        name="diff_attention",
    )(jnp.asarray(tab), qk, qk, vt, lam_vecs, subln)


def _mixout_body(h_ref, fox_ref, diff_ref, conv_ref, wf_ref, wd_ref, wc_ref, o_ref):
    o_ref[...] = (h_ref[...] + _dot(fox_ref[...], wf_ref[...])
                  + _dot(diff_ref[...], wd_ref[...]) + _dot(conv_ref[...], wc_ref[...]))


def _mixout(h, fox, diff, conv, w_out, l):
    n_tok, d = h.shape
    tm = TOKEN_TILE
    fw, dw, cw = fox.shape[1], diff.shape[1], conv.shape[1]
    assert fw == dw and (fw + dw) % cw == 0
    row = lambda i: (i, 0)
    resident = lambda rows, blk: pl.BlockSpec((None, rows, d), lambda i: (l, blk, 0),
                                              pipeline_mode=pl.Buffered(1))
    return pl.pallas_call(
        _mixout_body,
        grid=(n_tok // tm,),
        in_specs=[
            pl.BlockSpec((tm, d), row),
            pl.BlockSpec((tm, fw), row),
            pl.BlockSpec((tm, dw), row),
            pl.BlockSpec((tm, cw), row),
            resident(fw, 0),
            resident(dw, 1),
            resident(cw, (fw + dw) // cw),
        ],
        out_specs=pl.BlockSpec((tm, d), row),
        out_shape=jax.ShapeDtypeStruct((n_tok, d), F32),
        compiler_params=_params(1),
        name="mixout",
    )(h, fox, diff, conv, w_out, w_out, w_out)


def _memkv_body(m_ref, g_ref, w_ref, o_ref):
    o_ref[...] = _dot(_rms(m_ref[...], g_ref[...]).astype(BF16), w_ref[...]).astype(BF16)


def _memkv(mem, g, w, l):
    n_mem, d = mem.shape
    tm = min(n_mem, TOKEN_TILE)
    width = w.shape[2]
    return pl.pallas_call(
        _memkv_body,
        grid=(n_mem // tm,),
        in_specs=[
            pl.BlockSpec((tm, d), lambda i: (i, 0)),
            _layer_spec((1, d), lambda i: (l, 0, 0)),
            pl.BlockSpec((None, d, width), lambda i: (l, 0, 0), pipeline_mode=pl.Buffered(1)),
        ],
        out_specs=pl.BlockSpec((tm, width), lambda i: (i, 0)),
        out_shape=jax.ShapeDtypeStruct((n_mem, width), BF16),
        compiler_params=_params(1),
        name="memkv",
    )(mem, g, w)


def _cross_body(h_ref, g_ref, wq_ref, kv_ref, wo_ref, o_ref):
    hd = CROSS_HEAD_DIM
    width = CROSS_HEADS * hd
    scale = hd ** -0.5
    x = h_ref[...]
    q = _dot(_rms(x, g_ref[...]).astype(BF16), wq_ref[...]).astype(BF16)
    heads = []
    for hh in range(CROSS_HEADS):
        k = kv_ref[:, hh * hd:(hh + 1) * hd]
        v = kv_ref[:, width + hh * hd:width + (hh + 1) * hd]
        s = _dot_t(q[:, hh * hd:(hh + 1) * hd], k) * scale
        e = jnp.exp(s - jnp.max(s, axis=-1, keepdims=True))
        p = e / jnp.sum(e, axis=-1, keepdims=True)
        heads.append(_dot(p.astype(BF16), v))
    o = jnp.concatenate(heads, axis=-1).astype(BF16)
    o_ref[...] = x + _dot(o, wo_ref[...])


def _cross(h, g, wq, kv, wo, l, *, seq_len, mem_len):
    n_tok, d = h.shape
    tm = TOKEN_TILE
    tiles_per_seq = seq_len // tm
    resident = lambda a: pl.BlockSpec((None,) + a.shape[1:], lambda i: (l, 0, 0),
                                      pipeline_mode=pl.Buffered(1))
    return pl.pallas_call(
        _cross_body,
        grid=(n_tok // tm,),
        in_specs=[
            pl.BlockSpec((tm, d), lambda i: (i, 0)),
            _layer_spec((1, d), lambda i: (l, 0, 0)),
            resident(wq),
            pl.BlockSpec((mem_len, kv.shape[1]), lambda i: (i // tiles_per_seq, 0)),
            resident(wo),
        ],
        out_specs=pl.BlockSpec((tm, d), lambda i: (i, 0)),
        out_shape=jax.ShapeDtypeStruct((n_tok, d), F32),
        compiler_params=_params(1),
        name="cross_attention",
    )(h, g, wq, kv, wo)


def _rope_tables(positions):
    half = ROT_DIM // 2
    inv_freq = ROPE_THETA ** (-jnp.arange(0, ROT_DIM, 2, dtype=F32) / ROT_DIM)
    ang = positions.astype(F32)[..., None] * inv_freq
    cos = jnp.cos(ang).reshape(-1, half)
    sin = jnp.sin(ang).reshape(-1, half)
    lane = np.arange(LANES)
    rotated = (lane % (LANES // 2)) < ROT_DIM
    sign = np.where(lane < LANES // 2, -1.0, 1.0).astype(np.float32)
    reps = LANES // half
    ct = jnp.where(rotated[None, :], jnp.tile(cos, (1, reps)), 1.0)
    st = jnp.where(rotated[None, :], jnp.tile(sin, (1, reps)) * sign[None, :], 0.0)
    return ct, st


def _diff_head_lanes():
    half = ROT_DIM // 2
    src = np.zeros(LANES, np.int64)
    comp0 = np.zeros(LANES, bool)
    for c in range(2):
        for d in range(DIFF_QK_DIM):
            if d < half:
                lane = c * half + d
            elif d < ROT_DIM:
                lane = LANES // 2 + c * half + (d - half)
            else:
                lane = (ROT_DIM if c == 0 else LANES // 2 + ROT_DIM) + (d - ROT_DIM)
            src[lane] = c * DIFF_QK_DIM + d
            comp0[lane] = c == 0
    return src, comp0


def _in_weight_columns():
    fw, dw, cc = FOX_HEADS * FOX_HEAD_DIM, DIFF_HEADS * DIFF_V_DIM, CONV_CH
    ff0 = 3 * fw
    d0 = ff0 + FOX_HEADS
    c0 = d0 + 3 * dw
    src, _ = _diff_head_lanes()
    head_perm = np.concatenate([h * LANES + src for h in range(DIFF_HEADS)])
    cols = [np.arange(2 * fw),
            2 * fw + np.arange(fw), d0 + 2 * dw + np.arange(dw),
            d0 + head_perm, d0 + dw + head_perm]
    for t in range(_N_CONV_TILES):
        cols += [c0 + k * cc + t * CONV_TILE + np.arange(CONV_TILE) for k in range(3)]
    return np.concatenate(cols), ff0


def _runs(cols):
    cuts = np.flatnonzero(np.diff(cols) != 1) + 1
    return [(int(c[0]), int(c[-1]) + 1) for c in np.split(cols, cuts)]


def _arrange_body(x_ref, w_ref, wff_ref, *, runs, ff0):
    x = x_ref[...]
    w_ref[...] = jnp.concatenate([x[:, a:b] for a, b in runs], axis=-1).astype(BF16)
    ff = x[:, ff0:ff0 + FOX_HEADS]
    pad = jnp.zeros((x.shape[0], LANES - FOX_HEADS), F32)
    wff_ref[...] = jnp.concatenate([ff, pad], axis=-1).astype(BF16)


def _arrange_in_weights(w_in):
    n_layers, d, width = w_in.shape
    cols, ff0 = _in_weight_columns()
    rb = PREP_ROWS
    return pl.pallas_call(
        functools.partial(_arrange_body, runs=_runs(cols), ff0=ff0),
        grid=(n_layers, d // rb),
        in_specs=[_layer_spec((rb, width), lambda l, i: (l, i, 0))],
        out_specs=[_layer_spec((rb, cols.size), lambda l, i: (l, i, 0)),
                   _layer_spec((rb, LANES), lambda l, i: (l, i, 0))],
        out_shape=[jax.ShapeDtypeStruct((n_layers, d, cols.size), BF16),
                   jax.ShapeDtypeStruct((n_layers, d, LANES), BF16)],
        compiler_params=_params(2),
        name="arrange_in_weights",
    )(w_in)


def kernel(x, mem, positions, ffn1_norm, ffn1_w_gate, ffn1_w_up, ffn1_w_down, mix_norm, mix_w_in, forget_bias, conv_w, conv_b, lambda_q1, lambda_k1, lambda_q2, lambda_k2, diff_subln, mix_w_out, cross_norm, mem_norm, cross_w_q, cross_w_kv, cross_w_o, ffn2_norm, ffn2_w_gate, ffn2_w_up, ffn2_w_down, final_norm):
    batch, seq_len, d = x.shape
    mem_len = mem.shape[1]
    depth = ffn1_norm.shape[0]
    n_tok = batch * seq_len
    assert seq_len % TOKEN_TILE == 0 and seq_len % ATT_TILE == 0
    assert ffn1_w_gate.shape[2] % FF_TILE == 0

    bf = lambda a: a.astype(BF16)
    rows3 = lambda a: a.astype(F32).reshape(a.shape[0], 1, a.shape[1])
    ffn1_w = (ffn1_w_gate, ffn1_w_up, ffn1_w_down)
    ffn2_w = (ffn2_w_gate, ffn2_w_up, ffn2_w_down)
    ffn_w = tuple(bf(a[0]) for a in ffn1_w)
    w_in, w_ff = _arrange_in_weights(mix_w_in)
    w_out = bf(mix_w_out)
    wq, wkv, wo = bf(cross_w_q), bf(cross_w_kv), bf(cross_w_o)
    g_ffn1, g_mix, g_cross, g_mem, g_ffn2 = (rows3(a) for a in (ffn1_norm, mix_norm, cross_norm,
                                                                 mem_norm, ffn2_norm))
    g_final = final_norm.astype(F32).reshape(1, 1, d)
    fbias = rows3(jnp.pad(forget_bias, ((0, 0), (0, LANES - FOX_HEADS))))
    cbias = rows3(conv_b)
    subln = rows3(diff_subln)
    lam_vecs = jnp.stack([lambda_q1, lambda_k1, lambda_q2, lambda_k2], axis=1).astype(F32)
    ctab, stab = _rope_tables(positions)

    h = x.reshape(n_tok, d)
    mem2 = mem.reshape(batch * mem_len, d)
    for l in range(depth):
        h, ffn_w = _ffn(h, g_ffn1, ffn_w, g_final, l, ffn2_w + (l,))

        qk, conv, c, c_rows, fox_vt, diff_vt = _inproj(h, g_mix, w_in, w_ff, fbias, ctab, stab,
                                                       conv_w.astype(F32), cbias, l, seq_len=seq_len)
        fox = _fox_attention(qk, fox_vt, c, c_rows, batch=batch, seq_len=seq_len)
        lam_init = 0.8 - 0.6 * math.exp(-0.3 * l)
        diff = _diff_attention(qk, diff_vt, lam_vecs, subln, l, batch=batch, seq_len=seq_len,
                               lam_init=lam_init)
        h = _mixout(h, fox, diff, conv, w_out, l)

        kv = _memkv(mem2, g_mem, wkv, l)
        h = _cross(h, g_cross, wq, kv, wo, l, seq_len=seq_len, mem_len=mem_len)

        last = l == depth - 1
        h, ffn_w = _ffn(h, g_ffn2, ffn_w, g_final, l, None if last else ffn1_w + (l + 1,), final=last)
    return h.reshape(batch, seq_len, d)
```

```python
import functools
import math

import numpy as np
import jax
import jax.numpy as jnp
from jax import lax
from jax.experimental import pallas as pl
from jax.experimental.pallas import tpu as pltpu

F32 = jnp.float32
BF16 = jnp.bfloat16

FOX_HEADS = 6
FOX_HEAD_DIM = 128
DIFF_HEADS = 6
DIFF_QK_DIM = 64
DIFF_V_DIM = 128
CONV_CH = 512
CONV_WIDTH = 3
ROT_DIM = 16
ROPE_THETA = 500000.0
CROSS_HEADS = 4
CROSS_HEAD_DIM = 128
EPS = 1e-6
NEG_INF = -1e30
LOG2E = math.log2(math.e)
FOX_Q_SCALE = FOX_HEAD_DIM ** -0.5 * LOG2E
DIFF_Q_SCALE = DIFF_QK_DIM ** -0.5 * LOG2E

LANES = 128
SUBLANES = 8
BF16_ROWS = 16

TOKEN_TILE = 1024
FF_TILE = 512
IN_TILE = 768
CONV_TILE = 256
ATT_TILE = 512
PREP_ROWS = 256
ONES_ROWS = BF16_ROWS
VMEM_LIMIT = 56 * 1024 * 1024
BIG_VMEM_LIMIT = 60 * 1024 * 1024

_ARB = "arbitrary"


def _params(n_axes, vmem_limit=VMEM_LIMIT):
    return pltpu.CompilerParams(dimension_semantics=(_ARB,) * n_axes,
                                vmem_limit_bytes=vmem_limit)


def _rms(x, g):
    return x * lax.rsqrt(jnp.mean(x * x, axis=-1, keepdims=True) + EPS) * g


def _dot(a, b):
    return jnp.dot(a, b, preferred_element_type=F32)


def _dot_t(a, b):
    return lax.dot_general(a, b, (((1,), (1,)), ((), ())), preferred_element_type=F32)


def _layer_spec(shape, index_map):
    return pl.BlockSpec((None,) + tuple(shape), index_map)


def _ffn_body(*refs, final, convert_next):
    x_ref, g_ref, wg_ref, wu_ref, wd_ref, fg_ref = refs[:6]
    if convert_next:
        src_refs, o_ref, dst_refs, n_ref = refs[6:9], refs[9], refs[10:13], refs[13]
        for src_ref, dst_ref in zip(src_refs, dst_refs):
            dst_ref[...] = src_ref[...].astype(BF16)
    else:
        o_ref, n_ref = refs[6:]
    j = pl.program_id(1)

    @pl.when(j == 0)
    def _():
        x = x_ref[...]
        n_ref[...] = _rms(x, g_ref[...]).astype(BF16)
        o_ref[...] = x

    n = n_ref[...]
    gate = _dot(n, wg_ref[...])
    up = _dot(n, wu_ref[...])
    act = (gate * jax.nn.sigmoid(gate) * (0.5 * up)).astype(BF16)
    o_ref[...] += _dot(act, wd_ref[...])

    if final:
        @pl.when(j == pl.num_programs(1) - 1)
        def _():
            o_ref[...] = _rms(o_ref[...], fg_ref[...])


def _ffn(h, g, weights, final_g, l, next_weights=None, *, final=False):
    n_tok, d = h.shape
    wg, wu, wd = weights
    d_ff = wg.shape[1]
    tm, tf = TOKEN_TILE, FF_TILE
    ni, nj = n_tok // tm, d_ff // tf
    in_specs = [
        pl.BlockSpec((tm, d), lambda i, j: (i, 0)),
        _layer_spec((1, d), lambda i, j: (l, 0, 0)),
        pl.BlockSpec((d, tf), lambda i, j: (0, j)),
        pl.BlockSpec((d, tf), lambda i, j: (0, j)),
        pl.BlockSpec((tf, d), lambda i, j: (j, 0)),
        _layer_spec((1, d), lambda i, j: (0, 0, 0)),
    ]
    out_specs = [pl.BlockSpec((tm, d), lambda i, j: (i, 0))]
    out_shape = [jax.ShapeDtypeStruct((n_tok, d), F32)]
    operands = [h, g, wg, wu, wd, final_g]
    convert_next = next_weights is not None
    if convert_next:
        ng, nu, nd, nl = next_weights
        assert d % ni == 0 and (d // ni) % BF16_ROWS == 0
        dr = d // ni
        in_specs += [_layer_spec((dr, tf), lambda i, j: (nl, i, j)),
                     _layer_spec((dr, tf), lambda i, j: (nl, i, j)),
                     _layer_spec((tf, dr), lambda i, j: (nl, j, i))]
        out_specs += [pl.BlockSpec((dr, tf), lambda i, j: (i, j)),
                      pl.BlockSpec((dr, tf), lambda i, j: (i, j)),
                      pl.BlockSpec((tf, dr), lambda i, j: (j, i))]
        out_shape += [jax.ShapeDtypeStruct(a.shape[1:], BF16) for a in (ng, nu, nd)]
        operands += [ng, nu, nd]
    outs = pl.pallas_call(
        functools.partial(_ffn_body, final=final, convert_next=convert_next),
        grid=(ni, nj),
        in_specs=in_specs,
        out_specs=out_specs,
        out_shape=out_shape,
        scratch_shapes=[pltpu.VMEM((tm, d), BF16)],
        compiler_params=_params(2, BIG_VMEM_LIMIT),
        name="ffn_final" if final else "ffn",
    )(*operands)
    return outs[0], (tuple(outs[1:]) if convert_next else None)


_J_FOX_QK, _J_VALUES, _J_DIFF_QK, _J_CONV = 0, 1, 2, 3
_N_CONV_TILES = CONV_CH // CONV_TILE
_N_IN_PAIRS = 4
assert _N_CONV_TILES == 2


def _inproj_body(x_ref, g_ref, w_ref, wff_ref, fb_ref, ct_ref, st_ref, cw_ref, cb_ref,
                 qk_ref, conv_ref, c_ref, crow_ref, fvt_ref, dvt_ref, n_ref, zbuf_ref, carry_ref,
                 *, tiles_per_seq):
    i = pl.program_id(0)
    j = pl.program_id(1)
    tm = x_ref.shape[0]
    seq_start = (i % tiles_per_seq) == 0

    @pl.when(j == 0)
    def _():
        n_ref[...] = _rms(x_ref[...], g_ref[...]).astype(BF16)

        @pl.when(seq_start)
        def _():
            zbuf_ref[:, 0:SUBLANES, :] = jnp.zeros((_N_CONV_TILES, SUBLANES, CONV_TILE), F32)

    def project(half):
        return _dot_t(n_ref[...], w_ref[half * IN_TILE:(half + 1) * IN_TILE, :])

    @pl.when(j == _J_FOX_QK)
    def _():
        qk_ref[:, 0:IN_TILE] = (project(0) * FOX_Q_SCALE).astype(BF16)
        qk_ref[:, IN_TILE:] = project(1).astype(BF16)
        logf = jax.nn.log_sigmoid(_dot_t(n_ref[...], wff_ref[...]) + fb_ref[...])
        lt = logf.T[0:BF16_ROWS, :]
        hi = lt.astype(BF16)
        r1 = lt - hi.astype(F32)
        mid = r1.astype(BF16)
        lo = (r1 - mid.astype(F32)).astype(BF16)
        src = lax.broadcasted_iota(jnp.int32, (tm, tm), 0)
        dst = lax.broadcasted_iota(jnp.int32, (tm, tm), 1)
        tri = jnp.where(src <= dst, 1.0, 0.0).astype(BF16)
        parts = _dot(jnp.concatenate([hi, mid, lo], axis=0), tri)
        prev = jnp.where(seq_start, 0.0, carry_ref[...])
        ct = (parts[0:BF16_ROWS] + parts[BF16_ROWS:2 * BF16_ROWS] + parts[2 * BF16_ROWS:]) + prev
        carry_ref[...] = ct[:, tm - 1:tm]
        c_ref[...] = jnp.concatenate([ct, jnp.zeros((LANES - BF16_ROWS, tm), F32)], axis=0).T
        for k in range(tm // ATT_TILE):
            crow_ref[k] = ct[:, k * ATT_TILE:(k + 1) * ATT_TILE]

    @pl.when(j == _J_VALUES)
    def _():
        fvt_ref[...] = project(0).T.astype(BF16)
        dvt_ref[...] = project(1).T.astype(BF16)

    @pl.when(j == _J_DIFF_QK)
    def _():
        ct = ct_ref[...]
        st = st_ref[...]
        for half, q_scale in ((0, DIFF_Q_SCALE), (1, None)):
            y = project(half)
            for c0 in range(0, IN_TILE, LANES):
                yc = y[:, c0:c0 + LANES]
                out = yc * ct + pltpu.roll(yc, LANES // 2, axis=1) * st
                if q_scale is not None:
                    out = out * q_scale
                qk_ref[:, half * IN_TILE + c0:half * IN_TILE + c0 + LANES] = out.astype(BF16)

    @pl.when(j == _J_CONV)
    def _():
        cw = cw_ref[...]
        cb = cb_ref[...]
        for t in range(_N_CONV_TILES):
            ch = slice(t * CONV_TILE, (t + 1) * CONV_TILE)
            y = project(t)
            gb = y[:, 0:CONV_TILE]
            z = y[:, CONV_TILE:2 * CONV_TILE] * y[:, 2 * CONV_TILE:3 * CONV_TILE]
            zb = zbuf_ref.at[t]
            zb[SUBLANES:SUBLANES + tm, :] = z
            z1 = zb[SUBLANES - 1:SUBLANES - 1 + tm, :]
            z2 = zb[SUBLANES - 2:SUBLANES - 2 + tm, :]
            conv = z2 * cw[0:1, ch] + z1 * cw[1:2, ch] + z * cw[2:3, ch] + cb[:, ch]
            conv_ref[:, ch] = (gb * conv).astype(BF16)
            zb[0:SUBLANES, :] = z[tm - SUBLANES:tm, :]


def _inproj(h, g, w, wff, fbias, ctab, stab, conv_w, conv_b, l, *, seq_len):
    n_tok, d = h.shape
    tm = TOKEN_TILE
    pair = 2 * IN_TILE
    return pl.pallas_call(
        functools.partial(_inproj_body, tiles_per_seq=seq_len // tm),
        grid=(n_tok // tm, _N_IN_PAIRS),
        in_specs=[
            pl.BlockSpec((tm, d), lambda i, j: (i, 0)),
            _layer_spec((1, d), lambda i, j: (l, 0, 0)),
            _layer_spec((pair, d), lambda i, j: (l, j, 0)),
            _layer_spec((LANES, d), lambda i, j: (l, 0, 0)),
            _layer_spec((1, LANES), lambda i, j: (l, 0, 0)),
            pl.BlockSpec((tm, LANES), lambda i, j: (i, 0)),
            pl.BlockSpec((tm, LANES), lambda i, j: (i, 0)),
            _layer_spec((CONV_WIDTH, CONV_CH), lambda i, j: (l, 0, 0)),
            _layer_spec((1, CONV_CH), lambda i, j: (l, 0, 0)),
        ],
        out_specs=[
            pl.BlockSpec((tm, pair), lambda i, j: (i, j // 2)),
            pl.BlockSpec((tm, CONV_CH), lambda i, j: (i, 0)),
            pl.BlockSpec((tm, LANES), lambda i, j: (i, 0)),
            pl.BlockSpec((tm // ATT_TILE, BF16_ROWS, ATT_TILE), lambda i, j: (i, 0, 0)),
            pl.BlockSpec((IN_TILE, tm), lambda i, j: (0, i)),
            pl.BlockSpec((IN_TILE, tm), lambda i, j: (0, i)),
        ],
        out_shape=[
            jax.ShapeDtypeStruct((n_tok, 2 * pair), BF16),
            jax.ShapeDtypeStruct((n_tok, CONV_CH), BF16),
            jax.ShapeDtypeStruct((n_tok, LANES), F32),
            jax.ShapeDtypeStruct((n_tok // ATT_TILE, BF16_ROWS, ATT_TILE), F32),
            jax.ShapeDtypeStruct((IN_TILE, n_tok), BF16),
            jax.ShapeDtypeStruct((IN_TILE, n_tok), BF16),
        ],
        scratch_shapes=[
            pltpu.VMEM((tm, d), BF16),
            pltpu.VMEM((_N_CONV_TILES, tm + SUBLANES, CONV_TILE), F32),
            pltpu.VMEM((BF16_ROWS, 1), F32),
        ],
        compiler_params=_params(2, BIG_VMEM_LIMIT),
        name="inproj",
    )(h, g, w, wff, fbias, ctab, stab, conv_w, conv_b)


_T_QI, _T_KB, _T_DIAG, _T_STATE = 0, 1, 2, 3
PIPE_LAG = 2
PIPE_SLOTS = 2 * PIPE_LAG


def _item_table(nq):
    items = [(qi, kb) for qi in range(nq) for kb in range(qi + 1)]
    n_loops = -(-len(items) // PIPE_SLOTS)
    n_pos = PIPE_SLOTS * n_loops + 2 * PIPE_LAG
    tab = np.zeros((4, n_pos), np.int32)
    for pos in range(n_pos):
        item = pos - PIPE_LAG
        qi, kb = items[min(max(item, 0), len(items) - 1)]
        real = 0 <= item < len(items)
        tab[:, pos] = (qi, kb, int(real and kb == qi), qi if real else nq)
    return tab, n_loops


def _fill_bias(bias_ref, t):
    key = lax.broadcasted_iota(jnp.int32, (t, t), 0)
    qry = lax.broadcasted_iota(jnp.int32, (t, t), 1)
    bias_ref[0] = jnp.zeros((t, t), F32)
    bias_ref[1] = jnp.where(key <= qry, 0.0, NEG_INF)


def _stage_values(vt_ref, vte_ref, t):
    hd = vt_ref.shape[0]
    for kb in range(vte_ref.shape[0]):
        vte_ref[kb, 0:hd, :] = vt_ref[:, kb * t:(kb + 1) * t]
        vte_ref[kb, hd:, :] = jnp.ones((ONES_ROWS, t), BF16)


def _store_scores(u, u_ref, mx_ref):
    u_ref[...] = u
    mx_ref[...] = jnp.max(u, axis=0, keepdims=True)


def _softmax_stage(u_ref, mx_ref, shift, m_ref, p_ref, al_ref):
    m_old = m_ref[...]
    m_new = jnp.maximum(m_old, mx_ref[...] + shift)
    al_ref[...] = jnp.exp2(m_old - m_new)
    p_ref[...] = jnp.exp2(u_ref[...] - (m_new - shift)).astype(BF16)
    m_ref[...] = m_new


def _value_stage(vt, p_ref, al_ref, acc_ref):
    acc_ref[...] = al_ref[...] * acc_ref[...] + _dot(vt, p_ref[...])


def _fox_body(tab_ref, q_ref, k_ref, vt_ref, c_ref, crow_ref, o_ref,
              ckb_ref, vte_ref, bias_ref, u_ref, mx_ref, p_ref, al_ref, m_ref, acc_ref, *, n_loops):
    b = pl.program_id(0)
    h = pl.program_id(1)
    t = ATT_TILE
    hd = FOX_HEAD_DIM

    @pl.when((b == 0) & (h == 0))
    def _():
        _fill_bias(bias_ref, t)

    lane = lax.broadcasted_iota(jnp.int32, c_ref.shape, 1)
    ck = jnp.sum(jnp.where(lane == h, c_ref[...], 0.0), axis=-1, keepdims=True)
    ckb_ref[...] = jnp.broadcast_to(ck * LOG2E, ckb_ref.shape)
    _stage_values(vt_ref, vte_ref, t)
    p_ref[PIPE_LAG:] = jnp.zeros((PIPE_LAG, t, t), BF16)
    al_ref[PIPE_LAG:] = jnp.zeros((PIPE_LAG, 1, t), F32)
    acc_ref[...] = jnp.zeros_like(acc_ref)
    m_ref[...] = jnp.full_like(m_ref, NEG_INF)

    def rows(idx):
        return pl.ds(pl.multiple_of(idx * t, t), t)

    def score_stage(pos, slot):
        kb = tab_ref[_T_KB, pos]
        u = (_dot_t(k_ref[rows(kb), :], q_ref[rows(tab_ref[_T_QI, pos]), :])
             - jnp.tile(ckb_ref[rows(kb), :], (1, t // LANES)) + bias_ref[tab_ref[_T_DIAG, pos]])
        _store_scores(u, u_ref.at[slot], mx_ref.at[slot])

    def step(tau, slot):
        lagged = (slot + PIPE_LAG) % PIPE_SLOTS
        score_stage(tau + 2 * PIPE_LAG, lagged)
        pos = tau + PIPE_LAG
        cq = crow_ref[tab_ref[_T_QI, pos], pl.ds(h, 1), :] * LOG2E
        _softmax_stage(u_ref.at[slot], mx_ref.at[slot], cq, m_ref.at[tab_ref[_T_STATE, pos]],
                       p_ref.at[slot], al_ref.at[slot])
        value_stage(tau, lagged)

    def value_stage(tau, lagged):
        _value_stage(vte_ref[tab_ref[_T_KB, tau]], p_ref.at[lagged], al_ref.at[lagged],
                     acc_ref.at[tab_ref[_T_STATE, tau]])

    for item in range(PIPE_LAG):
        score_stage(item + PIPE_LAG, item)

    def unrolled(it, carry):
        for slot in range(PIPE_SLOTS):
            step(PIPE_SLOTS * it + slot, slot)
        return carry

    lax.fori_loop(0, n_loops, unrolled, 0)
    for tau in range(PIPE_SLOTS * n_loops, PIPE_SLOTS * n_loops + PIPE_LAG):
        value_stage(tau, (tau + PIPE_LAG) % PIPE_SLOTS)

    for qi in range(acc_ref.shape[0] - 1):
        acc = acc_ref[qi]
        o_ref[qi * t:(qi + 1) * t, :] = (acc[:hd, :] / acc[hd:hd + 1, :]).T.astype(BF16)


def _attention_scratch(t, nq, rows, n_streams):
    shape = lambda *s: ((n_streams,) if n_streams > 1 else ()) + s
    return [
        pltpu.VMEM((PIPE_SLOTS,) + shape(t, t), F32),
        pltpu.VMEM((PIPE_SLOTS,) + shape(1, t), F32),
        pltpu.VMEM((PIPE_SLOTS,) + shape(t, t), BF16),
        pltpu.VMEM((PIPE_SLOTS,) + shape(1, t), F32),
        pltpu.VMEM(shape(nq + 1, 1, t), F32),
        pltpu.VMEM(shape(nq + 1, rows, t), F32),
    ]


def _fox_attention(qk, vt, c, c_rows, *, batch, seq_len):
    t = ATT_TILE
    nq = seq_len // t
    hd = FOX_HEAD_DIM
    rows = hd + ONES_ROWS
    tab, n_loops = _item_table(nq)
    grid_spec = pltpu.PrefetchScalarGridSpec(
        num_scalar_prefetch=1,
        grid=(batch, FOX_HEADS),
        in_specs=[
            pl.BlockSpec((seq_len, hd), lambda b, h, tab: (b, h)),
            pl.BlockSpec((seq_len, hd), lambda b, h, tab: (b, FOX_HEADS + h)),
            pl.BlockSpec((hd, seq_len), lambda b, h, tab: (h, b)),
            pl.BlockSpec((seq_len, LANES), lambda b, h, tab: (b, 0)),
            pl.BlockSpec((nq, BF16_ROWS, t), lambda b, h, tab: (b, 0, 0)),
        ],
        out_specs=pl.BlockSpec((seq_len, hd), lambda b, h, tab: (b, h)),
        scratch_shapes=[pltpu.VMEM((seq_len, LANES), F32),
                        pltpu.VMEM((nq, rows, t), BF16),
                        pltpu.VMEM((2, t, t), F32)] + _attention_scratch(t, nq, rows, 1),
    )
    return pl.pallas_call(
        functools.partial(_fox_body, n_loops=n_loops),
        grid_spec=grid_spec,
        out_shape=jax.ShapeDtypeStruct((batch * seq_len, FOX_HEADS * hd), BF16),
        compiler_params=_params(2),
        name="fox_attention",
    )(jnp.asarray(tab), qk, qk, vt, c, c_rows)


def _diff_body(tab_ref, q_ref, k_ref, vt_ref, lam_ref, sg_ref, o_ref,
               vte_ref, bias_ref, u_ref, mx_ref, p_ref, al_ref, m_ref, acc_ref, *, n_loops, lam_init):
    b = pl.program_id(0)
    h = pl.program_id(1)
    t = ATT_TILE
    hd = DIFF_V_DIM

    @pl.when((b == 0) & (h == 0))
    def _():
        _fill_bias(bias_ref, t)

    _stage_values(vt_ref, vte_ref, t)
    p_ref[PIPE_LAG:] = jnp.zeros((PIPE_LAG, 2, t, t), BF16)
    al_ref[PIPE_LAG:] = jnp.zeros((PIPE_LAG, 2, 1, t), F32)
    acc_ref[...] = jnp.zeros_like(acc_ref)
    m_ref[...] = jnp.full_like(m_ref, NEG_INF)
    no_shift = jnp.zeros((1, t), F32)

    def rows(idx):
        return pl.ds(pl.multiple_of(idx * t, t), t)

    def score_stage(pos, slot):
        q = q_ref[rows(tab_ref[_T_QI, pos]), :]
        k = k_ref[rows(tab_ref[_T_KB, pos]), :]
        bias = bias_ref[tab_ref[_T_DIAG, pos]]
        lane = lax.broadcasted_iota(jnp.int32, q.shape, 1)
        comp0 = (lane < ROT_DIM // 2) | ((lane >= ROT_DIM) & (lane < LANES // 2 + ROT_DIM // 2))
        zero = jnp.zeros_like(q)
        for s, keep in enumerate((comp0, jnp.logical_not(comp0))):
            _store_scores(_dot_t(k, jnp.where(keep, q, zero)) + bias,
                          u_ref.at[slot, s], mx_ref.at[slot, s])

    def step(tau, slot):
        lagged = (slot + PIPE_LAG) % PIPE_SLOTS
        score_stage(tau + 2 * PIPE_LAG, lagged)
        state = tab_ref[_T_STATE, tau + PIPE_LAG]
        for s in range(2):
            _softmax_stage(u_ref.at[slot, s], mx_ref.at[slot, s], no_shift, m_ref.at[s, state],
                           p_ref.at[slot, s], al_ref.at[slot, s])
        value_stage(tau, lagged)

    def value_stage(tau, lagged):
        vt = vte_ref[tab_ref[_T_KB, tau]]
        for s in range(2):
            _value_stage(vt, p_ref.at[lagged, s], al_ref.at[lagged, s],
                         acc_ref.at[s, tab_ref[_T_STATE, tau]])

    for item in range(PIPE_LAG):
        score_stage(item + PIPE_LAG, item)

    def unrolled(it, carry):
        for slot in range(PIPE_SLOTS):
            step(PIPE_SLOTS * it + slot, slot)
        return carry

    lax.fori_loop(0, n_loops, unrolled, 0)
    for tau in range(PIPE_SLOTS * n_loops, PIPE_SLOTS * n_loops + PIPE_LAG):
        value_stage(tau, (tau + PIPE_LAG) % PIPE_SLOTS)

    lv = lam_ref[...]
    lam = (jnp.exp(jnp.sum(lv[0:1, :] * lv[1:2, :], axis=-1, keepdims=True))
           - jnp.exp(jnp.sum(lv[2:3, :] * lv[3:4, :], axis=-1, keepdims=True))
           + lam_init)
    for qi in range(acc_ref.shape[1] - 1):
        a1 = acc_ref[0, qi]
        a2 = acc_ref[1, qi]
        o = a1[:hd, :] / a1[hd:hd + 1, :] - lam * (a2[:hd, :] / a2[hd:hd + 1, :])
        o = o * lax.rsqrt(jnp.mean(o * o, axis=0, keepdims=True) + EPS)
        o_ref[qi * t:(qi + 1) * t, :] = (o.T * sg_ref[...] * (1.0 - lam_init)).astype(BF16)


def _diff_attention(qk, vt, lam_vecs, subln, l, *, batch, seq_len, lam_init):
    t = ATT_TILE
    nq = seq_len // t
    hd = DIFF_V_DIM
    rows = hd + ONES_ROWS
    base = 2 * FOX_HEADS
    tab, n_loops = _item_table(nq)
    grid_spec = pltpu.PrefetchScalarGridSpec(
        num_scalar_prefetch=1,
        grid=(batch, DIFF_HEADS),
        in_specs=[
            pl.BlockSpec((seq_len, hd), lambda b, h, tab: (b, base + h)),
            pl.BlockSpec((seq_len, hd), lambda b, h, tab: (b, base + DIFF_HEADS + h)),
            pl.BlockSpec((hd, seq_len), lambda b, h, tab: (h, b)),
            _layer_spec((4, DIFF_QK_DIM), lambda b, h, tab: (l, 0, 0)),
            _layer_spec((1, hd), lambda b, h, tab: (l, 0, 0)),
        ],
        out_specs=pl.BlockSpec((seq_len, hd), lambda b, h, tab: (b, h)),
        scratch_shapes=[pltpu.VMEM((nq, rows, t), BF16),
                        pltpu.VMEM((2, t, t), F32)] + _attention_scratch(t, nq, rows, 2),
    )
    return pl.pallas_call(
        functools.partial(_diff_body, n_loops=n_loops, lam_init=lam_init),
        grid_spec=grid_spec,
        out_shape=jax.ShapeDtypeStruct((batch * seq_len, DIFF_HEADS * hd), BF16),
        compiler_params=_params(2),
        name="diff_attention",
    )(jnp.asarray(tab), qk, qk, vt, lam_vecs, subln)


def _mixout_body(h_ref, fox_ref, diff_ref, conv_ref, wf_ref, wd_ref, wc_ref, o_ref):
    o_ref[...] = (h_ref[...] + _dot(fox_ref[...], wf_ref[...])
                  + _dot(diff_ref[...], wd_ref[...]) + _dot(conv_ref[...], wc_ref[...]))


def _mixout(h, fox, diff, conv, w_out, l):
    n_tok, d = h.shape
    tm = TOKEN_TILE
    fw, dw, cw = fox.shape[1], diff.shape[1], conv.shape[1]
    assert fw == dw and (fw + dw) % cw == 0
    row = lambda i: (i, 0)
    resident = lambda rows, blk: pl.BlockSpec((None, rows, d), lambda i: (l, blk, 0),
                                              pipeline_mode=pl.Buffered(1))
    return pl.pallas_call(
        _mixout_body,
        grid=(n_tok // tm,),
        in_specs=[
            pl.BlockSpec((tm, d), row),
            pl.BlockSpec((tm, fw), row),
            pl.BlockSpec((tm, dw), row),
            pl.BlockSpec((tm, cw), row),
            resident(fw, 0),
            resident(dw, 1),
            resident(cw, (fw + dw) // cw),
        ],
        out_specs=pl.BlockSpec((tm, d), row),
        out_shape=jax.ShapeDtypeStruct((n_tok, d), F32),
        compiler_params=_params(1),
        name="mixout",
    )(h, fox, diff, conv, w_out, w_out, w_out)


def _memkv_body(m_ref, g_ref, w_ref, o_ref):
    o_ref[...] = _dot(_rms(m_ref[...], g_ref[...]).astype(BF16), w_ref[...]).astype(BF16)


def _memkv(mem, g, w, l):
    n_mem, d = mem.shape
    tm = min(n_mem, TOKEN_TILE)
    width = w.shape[2]
    return pl.pallas_call(
        _memkv_body,
        grid=(n_mem // tm,),
        in_specs=[
            pl.BlockSpec((tm, d), lambda i: (i, 0)),
            _layer_spec((1, d), lambda i: (l, 0, 0)),
            pl.BlockSpec((None, d, width), lambda i: (l, 0, 0), pipeline_mode=pl.Buffered(1)),
        ],
        out_specs=pl.BlockSpec((tm, width), lambda i: (i, 0)),
        out_shape=jax.ShapeDtypeStruct((n_mem, width), BF16),
        compiler_params=_params(1),
        name="memkv",
    )(mem, g, w)


def _cross_body(h_ref, g_ref, wq_ref, kv_ref, wo_ref, o_ref):
    hd = CROSS_HEAD_DIM
    width = CROSS_HEADS * hd
    scale = hd ** -0.5
    x = h_ref[...]
    q = _dot(_rms(x, g_ref[...]).astype(BF16), wq_ref[...]).astype(BF16)
    heads = []
    for hh in range(CROSS_HEADS):
        k = kv_ref[:, hh * hd:(hh + 1) * hd]
        v = kv_ref[:, width + hh * hd:width + (hh + 1) * hd]
        s = _dot_t(q[:, hh * hd:(hh + 1) * hd], k) * scale
        e = jnp.exp(s - jnp.max(s, axis=-1, keepdims=True))
        p = e / jnp.sum(e, axis=-1, keepdims=True)
        heads.append(_dot(p.astype(BF16), v))
    o = jnp.concatenate(heads, axis=-1).astype(BF16)
    o_ref[...] = x + _dot(o, wo_ref[...])


def _cross(h, g, wq, kv, wo, l, *, seq_len, mem_len):
    n_tok, d = h.shape
    tm = TOKEN_TILE
    tiles_per_seq = seq_len // tm
    resident = lambda a: pl.BlockSpec((None,) + a.shape[1:], lambda i: (l, 0, 0),
                                      pipeline_mode=pl.Buffered(1))
    return pl.pallas_call(
        _cross_body,
        grid=(n_tok // tm,),
        in_specs=[
            pl.BlockSpec((tm, d), lambda i: (i, 0)),
            _layer_spec((1, d), lambda i: (l, 0, 0)),
            resident(wq),
            pl.BlockSpec((mem_len, kv.shape[1]), lambda i: (i // tiles_per_seq, 0)),
            resident(wo),
        ],
        out_specs=pl.BlockSpec((tm, d), lambda i: (i, 0)),
        out_shape=jax.ShapeDtypeStruct((n_tok, d), F32),
        compiler_params=_params(1),
        name="cross_attention",
    )(h, g, wq, kv, wo)


def _rope_tables(positions):
    half = ROT_DIM // 2
    inv_freq = ROPE_THETA ** (-jnp.arange(0, ROT_DIM, 2, dtype=F32) / ROT_DIM)
    ang = positions.astype(F32)[..., None] * inv_freq
    cos = jnp.cos(ang).reshape(-1, half)
    sin = jnp.sin(ang).reshape(-1, half)
    lane = np.arange(LANES)
    rotated = (lane % (LANES // 2)) < ROT_DIM
    sign = np.where(lane < LANES // 2, -1.0, 1.0).astype(np.float32)
    reps = LANES // half
    ct = jnp.where(rotated[None, :], jnp.tile(cos, (1, reps)), 1.0)
    st = jnp.where(rotated[None, :], jnp.tile(sin, (1, reps)) * sign[None, :], 0.0)
    return ct, st


def _diff_head_lanes():
    half = ROT_DIM // 2
    src = np.zeros(LANES, np.int64)
    comp0 = np.zeros(LANES, bool)
    for c in range(2):
        for d in range(DIFF_QK_DIM):
            if d < half:
                lane = c * half + d
            elif d < ROT_DIM:
                lane = LANES // 2 + c * half + (d - half)
            else:
                lane = (ROT_DIM if c == 0 else LANES // 2 + ROT_DIM) + (d - ROT_DIM)
            src[lane] = c * DIFF_QK_DIM + d
            comp0[lane] = c == 0
    return src, comp0


def _in_weight_columns():
    fw, dw, cc = FOX_HEADS * FOX_HEAD_DIM, DIFF_HEADS * DIFF_V_DIM, CONV_CH
    ff0 = 3 * fw
    d0 = ff0 + FOX_HEADS
    c0 = d0 + 3 * dw
    src, _ = _diff_head_lanes()
    head_perm = np.concatenate([h * LANES + src for h in range(DIFF_HEADS)])
    cols = [np.arange(2 * fw),
            2 * fw + np.arange(fw), d0 + 2 * dw + np.arange(dw),
            d0 + head_perm, d0 + dw + head_perm]
    for t in range(_N_CONV_TILES):
        cols += [c0 + k * cc + t * CONV_TILE + np.arange(CONV_TILE) for k in range(3)]
    return np.concatenate(cols), ff0


def _runs(cols):
    cuts = np.flatnonzero(np.diff(cols) != 1) + 1
    return [(int(c[0]), int(c[-1]) + 1) for c in np.split(cols, cuts)]


def _arrange_body(x_ref, w_ref, wff_ref, *, runs, ff0):
    n_layers = x_ref.shape[1]
    pad = jnp.zeros((LANES - FOX_HEADS, x_ref.shape[2]), F32)
    for l in range(n_layers):
        w_ref[l] = jnp.concatenate([x_ref[a:b, l, :] for a, b in runs], axis=0).astype(BF16)
        wff_ref[l] = jnp.concatenate([x_ref[ff0:ff0 + FOX_HEADS, l, :], pad], axis=0).astype(BF16)


def _arrange_in_weights(w_in):
    n_layers, d, width = w_in.shape
    cols, ff0 = _in_weight_columns()
    return pl.pallas_call(
        functools.partial(_arrange_body, runs=_runs(cols), ff0=ff0),
        grid=(d // LANES,),
        in_specs=[pl.BlockSpec((width, n_layers, LANES), lambda i: (0, 0, i))],
        out_specs=[pl.BlockSpec((n_layers, cols.size, LANES), lambda i: (0, 0, i)),
                   pl.BlockSpec((n_layers, LANES, LANES), lambda i: (0, 0, i))],
        out_shape=[jax.ShapeDtypeStruct((n_layers, cols.size, d), BF16),
                   jax.ShapeDtypeStruct((n_layers, LANES, d), BF16)],
        compiler_params=_params(1),
        name="arrange_in_weights",
    )(jnp.transpose(w_in, (2, 0, 1)))


def kernel(x, mem, positions, ffn1_norm, ffn1_w_gate, ffn1_w_up, ffn1_w_down, mix_norm, mix_w_in, forget_bias, conv_w, conv_b, lambda_q1, lambda_k1, lambda_q2, lambda_k2, diff_subln, mix_w_out, cross_norm, mem_norm, cross_w_q, cross_w_kv, cross_w_o, ffn2_norm, ffn2_w_gate, ffn2_w_up, ffn2_w_down, final_norm):
    batch, seq_len, d = x.shape
    mem_len = mem.shape[1]
    depth = ffn1_norm.shape[0]
    n_tok = batch * seq_len
    assert seq_len % TOKEN_TILE == 0 and seq_len % ATT_TILE == 0
    assert ffn1_w_gate.shape[2] % FF_TILE == 0

    bf = lambda a: a.astype(BF16)
    rows3 = lambda a: a.astype(F32).reshape(a.shape[0], 1, a.shape[1])
    ffn1_w = (ffn1_w_gate, ffn1_w_up, ffn1_w_down)
    ffn2_w = (ffn2_w_gate, ffn2_w_up, ffn2_w_down)
    ffn_w = tuple(bf(a[0]) for a in ffn1_w)
    w_in, w_ff = _arrange_in_weights(mix_w_in)
    w_out = bf(mix_w_out)
    wq, wkv, wo = bf(cross_w_q), bf(cross_w_kv), bf(cross_w_o)
    g_ffn1, g_mix, g_cross, g_mem, g_ffn2 = (rows3(a) for a in (ffn1_norm, mix_norm, cross_norm,
                                                                 mem_norm, ffn2_norm))
    g_final = final_norm.astype(F32).reshape(1, 1, d)
    fbias = rows3(jnp.pad(forget_bias, ((0, 0), (0, LANES - FOX_HEADS))))
    cbias = rows3(conv_b)
    subln = rows3(diff_subln)
    lam_vecs = jnp.stack([lambda_q1, lambda_k1, lambda_q2, lambda_k2], axis=1).astype(F32)
    ctab, stab = _rope_tables(positions)

    h = x.reshape(n_tok, d)
    mem2 = mem.reshape(batch * mem_len, d)
    for l in range(depth):
        h, ffn_w = _ffn(h, g_ffn1, ffn_w, g_final, l, ffn2_w + (l,))

        qk, conv, c, c_rows, fox_vt, diff_vt = _inproj(h, g_mix, w_in, w_ff, fbias, ctab, stab,
                                                       conv_w.astype(F32), cbias, l, seq_len=seq_len)
        fox = _fox_attention(qk, fox_vt, c, c_rows, batch=batch, seq_len=seq_len)
        lam_init = 0.8 - 0.6 * math.exp(-0.3 * l)
        diff = _diff_attention(qk, diff_vt, lam_vecs, subln, l, batch=batch, seq_len=seq_len,
                               lam_init=lam_init)
        h = _mixout(h, fox, diff, conv, w_out, l)

        kv = _memkv(mem2, g_mem, wkv, l)
        h = _cross(h, g_cross, wq, kv, wo, l, seq_len=seq_len, mem_len=mem_len)

        last = l == depth - 1
        h, ffn_w = _ffn(h, g_ffn2, ffn_w, g_final, l, None if last else ffn1_w + (l + 1,), final=last)
    return h.reshape(batch, seq_len, d)
```

```python
import functools
import math

import numpy as np
import jax
import jax.numpy as jnp
from jax import lax
from jax.experimental import pallas as pl
from jax.experimental.pallas import tpu as pltpu

F32 = jnp.float32
BF16 = jnp.bfloat16

FOX_HEADS = 6
FOX_HEAD_DIM = 128
DIFF_HEADS = 6
DIFF_QK_DIM = 64
DIFF_V_DIM = 128
CONV_CH = 512
CONV_WIDTH = 3
ROT_DIM = 16
ROPE_THETA = 500000.0
CROSS_HEADS = 4
CROSS_HEAD_DIM = 128
EPS = 1e-6
NEG_INF = -1e30
LOG2E = math.log2(math.e)
FOX_Q_SCALE = FOX_HEAD_DIM ** -0.5 * LOG2E
DIFF_Q_SCALE = DIFF_QK_DIM ** -0.5 * LOG2E

LANES = 128
SUBLANES = 8
BF16_ROWS = 16

TOKEN_TILE = 1024
FF_TILE = 512
IN_TILE = 768
CONV_TILE = 256
ATT_TILE = 512
ONES_ROWS = BF16_ROWS
VMEM_LIMIT = 56 * 1024 * 1024
BIG_VMEM_LIMIT = 60 * 1024 * 1024

_ARB = "arbitrary"


def _params(n_axes, vmem_limit=VMEM_LIMIT):
    return pltpu.CompilerParams(dimension_semantics=(_ARB,) * n_axes,
                                vmem_limit_bytes=vmem_limit)


def _rms(x, g):
    return x * lax.rsqrt(jnp.mean(x * x, axis=-1, keepdims=True) + EPS) * g


def _dot(a, b):
    return jnp.dot(a, b, preferred_element_type=F32)


def _dot_t(a, b):
    return lax.dot_general(a, b, (((1,), (1,)), ((), ())), preferred_element_type=F32)


def _layer_spec(shape, index_map):
    return pl.BlockSpec((None,) + tuple(shape), index_map)


def _ffn_body(*refs, final, convert_next):
    x_ref, g_ref, wg_ref, wu_ref, wd_ref, fg_ref = refs[:6]
    if convert_next:
        src_refs, o_ref, dst_refs, n_ref = refs[6:9], refs[9], refs[10:13], refs[13]
        for src_ref, dst_ref in zip(src_refs, dst_refs):
            dst_ref[...] = src_ref[...].astype(BF16)
    else:
        o_ref, n_ref = refs[6:]
    j = pl.program_id(1)

    @pl.when(j == 0)
    def _():
        x = x_ref[...]
        n_ref[...] = _rms(x, g_ref[...]).astype(BF16)
        o_ref[...] = x

    n = n_ref[...]
    gate = _dot(n, wg_ref[...])
    up = _dot(n, wu_ref[...])
    act = (gate * jax.nn.sigmoid(gate) * (0.5 * up)).astype(BF16)
    o_ref[...] += _dot(act, wd_ref[...])

    if final:
        @pl.when(j == pl.num_programs(1) - 1)
        def _():
            o_ref[...] = _rms(o_ref[...], fg_ref[...])


def _ffn(h, g, weights, final_g, l, next_weights=None, *, final=False):
    n_tok, d = h.shape
    wg, wu, wd = weights
    d_ff = wg.shape[1]
    tm, tf = TOKEN_TILE, FF_TILE
    ni, nj = n_tok // tm, d_ff // tf
    in_specs = [
        pl.BlockSpec((tm, d), lambda i, j: (i, 0)),
        _layer_spec((1, d), lambda i, j: (l, 0, 0)),
        pl.BlockSpec((d, tf), lambda i, j: (0, j)),
        pl.BlockSpec((d, tf), lambda i, j: (0, j)),
        pl.BlockSpec((tf, d), lambda i, j: (j, 0)),
        _layer_spec((1, d), lambda i, j: (0, 0, 0)),
    ]
    out_specs = [pl.BlockSpec((tm, d), lambda i, j: (i, 0))]
    out_shape = [jax.ShapeDtypeStruct((n_tok, d), F32)]
    operands = [h, g, wg, wu, wd, final_g]
    convert_next = next_weights is not None
    if convert_next:
        ng, nu, nd, nl = next_weights
        assert d % ni == 0 and (d // ni) % BF16_ROWS == 0
        dr = d // ni
        in_specs += [_layer_spec((dr, tf), lambda i, j: (nl, i, j)),
                     _layer_spec((dr, tf), lambda i, j: (nl, i, j)),
                     _layer_spec((tf, dr), lambda i, j: (nl, j, i))]
        out_specs += [pl.BlockSpec((dr, tf), lambda i, j: (i, j)),
                      pl.BlockSpec((dr, tf), lambda i, j: (i, j)),
                      pl.BlockSpec((tf, dr), lambda i, j: (j, i))]
        out_shape += [jax.ShapeDtypeStruct(a.shape[1:], BF16) for a in (ng, nu, nd)]
        operands += [ng, nu, nd]
    outs = pl.pallas_call(
        functools.partial(_ffn_body, final=final, convert_next=convert_next),
        grid=(ni, nj),
        in_specs=in_specs,
        out_specs=out_specs,
        out_shape=out_shape,
        scratch_shapes=[pltpu.VMEM((tm, d), BF16)],
        compiler_params=_params(2, BIG_VMEM_LIMIT),
        name="ffn_final" if final else "ffn",
    )(*operands)
    return outs[0], (tuple(outs[1:]) if convert_next else None)


_J_FOX_QK, _J_VALUES, _J_DIFF_QK, _J_CONV = 0, 1, 2, 3
_N_CONV_TILES = CONV_CH // CONV_TILE
_N_IN_PAIRS = 4
assert _N_CONV_TILES == 2


def _inproj_body(x_ref, g_ref, w_ref, wff_ref, fb_ref, ct_ref, st_ref, cw_ref, cb_ref,
                 qk_ref, conv_ref, c_ref, crow_ref, fvt_ref, dvt_ref, n_ref, zbuf_ref, carry_ref,
                 *, tiles_per_seq):
    i = pl.program_id(0)
    j = pl.program_id(1)
    tm = x_ref.shape[0]
    seq_start = (i % tiles_per_seq) == 0

    @pl.when(j == 0)
    def _():
        n_ref[...] = _rms(x_ref[...], g_ref[...]).astype(BF16)

        @pl.when(seq_start)
        def _():
            zbuf_ref[:, 0:SUBLANES, :] = jnp.zeros((_N_CONV_TILES, SUBLANES, CONV_TILE), F32)

    def project(half):
        return _dot_t(n_ref[...], w_ref[half * IN_TILE:(half + 1) * IN_TILE, :])

    @pl.when(j == _J_FOX_QK)
    def _():
        qk_ref[:, 0:IN_TILE] = (project(0) * FOX_Q_SCALE).astype(BF16)
        qk_ref[:, IN_TILE:] = project(1).astype(BF16)
        logf = jax.nn.log_sigmoid(_dot_t(n_ref[...], wff_ref[...]) + fb_ref[...])
        lt = logf.T[0:BF16_ROWS, :]
        hi = lt.astype(BF16)
        r1 = lt - hi.astype(F32)
        mid = r1.astype(BF16)
        lo = (r1 - mid.astype(F32)).astype(BF16)
        src = lax.broadcasted_iota(jnp.int32, (tm, tm), 0)
        dst = lax.broadcasted_iota(jnp.int32, (tm, tm), 1)
        tri = jnp.where(src <= dst, 1.0, 0.0).astype(BF16)
        parts = _dot(jnp.concatenate([hi, mid, lo], axis=0), tri)
        prev = jnp.where(seq_start, 0.0, carry_ref[...])
        ct = (parts[0:BF16_ROWS] + parts[BF16_ROWS:2 * BF16_ROWS] + parts[2 * BF16_ROWS:]) + prev
        carry_ref[...] = ct[:, tm - 1:tm]
        c_ref[...] = jnp.concatenate([ct, jnp.zeros((LANES - BF16_ROWS, tm), F32)], axis=0).T
        for k in range(tm // ATT_TILE):
            crow_ref[k] = ct[:, k * ATT_TILE:(k + 1) * ATT_TILE]

    @pl.when(j == _J_VALUES)
    def _():
        fvt_ref[...] = project(0).T.astype(BF16)
        dvt_ref[...] = project(1).T.astype(BF16)

    @pl.when(j == _J_DIFF_QK)
    def _():
        ct = ct_ref[...]
        st = st_ref[...]
        for half, q_scale in ((0, DIFF_Q_SCALE), (1, None)):
            y = project(half)
            for c0 in range(0, IN_TILE, LANES):
                yc = y[:, c0:c0 + LANES]
                out = yc * ct + pltpu.roll(yc, LANES // 2, axis=1) * st
                if q_scale is not None:
                    out = out * q_scale
                qk_ref[:, half * IN_TILE + c0:half * IN_TILE + c0 + LANES] = out.astype(BF16)

    @pl.when(j == _J_CONV)
    def _():
        cw = cw_ref[...]
        cb = cb_ref[...]
        for t in range(_N_CONV_TILES):
            ch = slice(t * CONV_TILE, (t + 1) * CONV_TILE)
            y = project(t)
            gb = y[:, 0:CONV_TILE]
            z = y[:, CONV_TILE:2 * CONV_TILE] * y[:, 2 * CONV_TILE:3 * CONV_TILE]
            zb = zbuf_ref.at[t]
            zb[SUBLANES:SUBLANES + tm, :] = z
            z1 = zb[SUBLANES - 1:SUBLANES - 1 + tm, :]
            z2 = zb[SUBLANES - 2:SUBLANES - 2 + tm, :]
            conv = z2 * cw[0:1, ch] + z1 * cw[1:2, ch] + z * cw[2:3, ch] + cb[:, ch]
            conv_ref[:, ch] = (gb * conv).astype(BF16)
            zb[0:SUBLANES, :] = z[tm - SUBLANES:tm, :]


def _inproj(h, g, w, wff, fbias, ctab, stab, conv_w, conv_b, l, *, seq_len):
    n_tok, d = h.shape
    tm = TOKEN_TILE
    pair = 2 * IN_TILE
    return pl.pallas_call(
        functools.partial(_inproj_body, tiles_per_seq=seq_len // tm),
        grid=(n_tok // tm, _N_IN_PAIRS),
        in_specs=[
            pl.BlockSpec((tm, d), lambda i, j: (i, 0)),
            _layer_spec((1, d), lambda i, j: (l, 0, 0)),
            _layer_spec((pair, d), lambda i, j: (l, j, 0)),
            _layer_spec((LANES, d), lambda i, j: (l, 0, 0)),
            _layer_spec((1, LANES), lambda i, j: (l, 0, 0)),
            pl.BlockSpec((tm, LANES), lambda i, j: (i, 0)),
            pl.BlockSpec((tm, LANES), lambda i, j: (i, 0)),
            _layer_spec((CONV_WIDTH, CONV_CH), lambda i, j: (l, 0, 0)),
            _layer_spec((1, CONV_CH), lambda i, j: (l, 0, 0)),
        ],
        out_specs=[
            pl.BlockSpec((tm, pair), lambda i, j: (i, j // 2)),
            pl.BlockSpec((tm, CONV_CH), lambda i, j: (i, 0)),
            pl.BlockSpec((tm, LANES), lambda i, j: (i, 0)),
            pl.BlockSpec((tm // ATT_TILE, BF16_ROWS, ATT_TILE), lambda i, j: (i, 0, 0)),
            pl.BlockSpec((IN_TILE, tm), lambda i, j: (0, i)),
            pl.BlockSpec((IN_TILE, tm), lambda i, j: (0, i)),
        ],
        out_shape=[
            jax.ShapeDtypeStruct((n_tok, 2 * pair), BF16),
            jax.ShapeDtypeStruct((n_tok, CONV_CH), BF16),
            jax.ShapeDtypeStruct((n_tok, LANES), F32),
            jax.ShapeDtypeStruct((n_tok // ATT_TILE, BF16_ROWS, ATT_TILE), F32),
            jax.ShapeDtypeStruct((IN_TILE, n_tok), BF16),
            jax.ShapeDtypeStruct((IN_TILE, n_tok), BF16),
        ],
        scratch_shapes=[
            pltpu.VMEM((tm, d), BF16),
            pltpu.VMEM((_N_CONV_TILES, tm + SUBLANES, CONV_TILE), F32),
            pltpu.VMEM((BF16_ROWS, 1), F32),
        ],
        compiler_params=_params(2, BIG_VMEM_LIMIT),
        name="inproj",
    )(h, g, w, wff, fbias, ctab, stab, conv_w, conv_b)


_T_QI, _T_KB, _T_DIAG, _T_STATE = 0, 1, 2, 3
PIPE_LAG = 2
PIPE_SLOTS = 2 * PIPE_LAG


def _item_table(nq):
    items = [(qi, kb) for qi in range(nq) for kb in range(qi + 1)]
    n_loops = -(-len(items) // PIPE_SLOTS)
    n_pos = PIPE_SLOTS * n_loops + 2 * PIPE_LAG
    tab = np.zeros((4, n_pos), np.int32)
    for pos in range(n_pos):
        item = pos - PIPE_LAG
        qi, kb = items[min(max(item, 0), len(items) - 1)]
        real = 0 <= item < len(items)
        tab[:, pos] = (qi, kb, int(real and kb == qi), qi if real else nq)
    return tab, n_loops


def _fill_bias(bias_ref, t):
    key = lax.broadcasted_iota(jnp.int32, (t, t), 0)
    qry = lax.broadcasted_iota(jnp.int32, (t, t), 1)
    bias_ref[0] = jnp.zeros((t, t), F32)
    bias_ref[1] = jnp.where(key <= qry, 0.0, NEG_INF)


def _stage_values(vt_ref, vte_ref, t):
    hd = vt_ref.shape[0]
    for kb in range(vte_ref.shape[0]):
        vte_ref[kb, 0:hd, :] = vt_ref[:, kb * t:(kb + 1) * t]
        vte_ref[kb, hd:, :] = jnp.ones((ONES_ROWS, t), BF16)


def _store_scores(u, u_ref, mx_ref):
    u_ref[...] = u
    mx_ref[...] = jnp.max(u, axis=0, keepdims=True)


def _softmax_stage(u_ref, mx_ref, shift, m_ref, p_ref, al_ref):
    m_old = m_ref[...]
    m_new = jnp.maximum(m_old, mx_ref[...] + shift)
    al_ref[...] = jnp.exp2(m_old - m_new)
    p_ref[...] = jnp.exp2(u_ref[...] - (m_new - shift)).astype(BF16)
    m_ref[...] = m_new


def _value_stage(vt, p_ref, al_ref, acc_ref):
    acc_ref[...] = al_ref[...] * acc_ref[...] + _dot(vt, p_ref[...])


def _fox_body(tab_ref, q_ref, k_ref, vt_ref, c_ref, crow_ref, o_ref,
              ckb_ref, vte_ref, bias_ref, u_ref, mx_ref, p_ref, al_ref, m_ref, acc_ref, *, n_loops):
    b = pl.program_id(0)
    h = pl.program_id(1)
    t = ATT_TILE
    hd = FOX_HEAD_DIM

    @pl.when((b == 0) & (h == 0))
    def _():
        _fill_bias(bias_ref, t)
        p_ref[PIPE_LAG:] = jnp.zeros((PIPE_LAG, t, t), BF16)
        al_ref[PIPE_LAG:] = jnp.zeros((PIPE_LAG, 1, t), F32)
        acc_ref[...] = jnp.zeros_like(acc_ref)

    lane = lax.broadcasted_iota(jnp.int32, c_ref.shape, 1)
    ck = jnp.sum(jnp.where(lane == h, c_ref[...], 0.0), axis=-1, keepdims=True)
    ckb_ref[...] = jnp.broadcast_to(ck * LOG2E, ckb_ref.shape)
    _stage_values(vt_ref, vte_ref, t)
    m_ref[...] = jnp.full_like(m_ref, NEG_INF)

    def rows(idx):
        return pl.ds(pl.multiple_of(idx * t, t), t)

    def score_stage(pos, slot):
        kb = tab_ref[_T_KB, pos]
        u = (_dot_t(k_ref[rows(kb), :], q_ref[rows(tab_ref[_T_QI, pos]), :])
             - jnp.tile(ckb_ref[rows(kb), :], (1, t // LANES)) + bias_ref[tab_ref[_T_DIAG, pos]])
        _store_scores(u, u_ref.at[slot], mx_ref.at[slot])

    def step(tau, slot):
        lagged = (slot + PIPE_LAG) % PIPE_SLOTS
        score_stage(tau + 2 * PIPE_LAG, lagged)
        pos = tau + PIPE_LAG
        cq = crow_ref[tab_ref[_T_QI, pos], pl.ds(h, 1), :] * LOG2E
        _softmax_stage(u_ref.at[slot], mx_ref.at[slot], cq, m_ref.at[tab_ref[_T_STATE, pos]],
                       p_ref.at[slot], al_ref.at[slot])
        value_stage(tau, lagged)

    def value_stage(tau, lagged):
        _value_stage(vte_ref[tab_ref[_T_KB, tau]], p_ref.at[lagged], al_ref.at[lagged],
                     acc_ref.at[tab_ref[_T_STATE, tau]])

    for item in range(PIPE_LAG):
        score_stage(item + PIPE_LAG, item)

    def unrolled(it, carry):
        for slot in range(PIPE_SLOTS):
            step(PIPE_SLOTS * it + slot, slot)
        return carry

    lax.fori_loop(0, n_loops, unrolled, 0)
    for tau in range(PIPE_SLOTS * n_loops, PIPE_SLOTS * n_loops + PIPE_LAG):
        value_stage(tau, (tau + PIPE_LAG) % PIPE_SLOTS)

    for qi in range(acc_ref.shape[0] - 1):
        acc = acc_ref[qi]
        o_ref[qi * t:(qi + 1) * t, :] = (acc[:hd, :] / acc[hd:hd + 1, :]).T.astype(BF16)


def _attention_scratch(t, nq, rows, n_streams):
    shape = lambda *s: ((n_streams,) if n_streams > 1 else ()) + s
    return [
        pltpu.VMEM((PIPE_SLOTS,) + shape(t, t), F32),
        pltpu.VMEM((PIPE_SLOTS,) + shape(1, t), F32),
        pltpu.VMEM((PIPE_SLOTS,) + shape(t, t), BF16),
        pltpu.VMEM((PIPE_SLOTS,) + shape(1, t), F32),
        pltpu.VMEM(shape(nq + 1, 1, t), F32),
        pltpu.VMEM(shape(nq + 1, rows, t), F32),
    ]


def _fox_attention(qk, vt, c, c_rows, *, batch, seq_len):
    t = ATT_TILE
    nq = seq_len // t
    hd = FOX_HEAD_DIM
    rows = hd + ONES_ROWS
    tab, n_loops = _item_table(nq)
    grid_spec = pltpu.PrefetchScalarGridSpec(
        num_scalar_prefetch=1,
        grid=(batch, FOX_HEADS),
        in_specs=[
            pl.BlockSpec((seq_len, hd), lambda b, h, tab: (b, h)),
            pl.BlockSpec((seq_len, hd), lambda b, h, tab: (b, FOX_HEADS + h)),
            pl.BlockSpec((hd, seq_len), lambda b, h, tab: (h, b)),
            pl.BlockSpec((seq_len, LANES), lambda b, h, tab: (b, 0)),
            pl.BlockSpec((nq, BF16_ROWS, t), lambda b, h, tab: (b, 0, 0)),
        ],
        out_specs=pl.BlockSpec((seq_len, hd), lambda b, h, tab: (b, h)),
        scratch_shapes=[pltpu.VMEM((seq_len, LANES), F32),
                        pltpu.VMEM((nq, rows, t), BF16),
                        pltpu.VMEM((2, t, t), F32)] + _attention_scratch(t, nq, rows, 1),
    )
    return pl.pallas_call(
        functools.partial(_fox_body, n_loops=n_loops),
        grid_spec=grid_spec,
        out_shape=jax.ShapeDtypeStruct((batch * seq_len, FOX_HEADS * hd), BF16),
        compiler_params=_params(2),
        name="fox_attention",
    )(jnp.asarray(tab), qk, qk, vt, c, c_rows)


def _diff_body(tab_ref, q_ref, k_ref, vt_ref, lam_ref, sg_ref, o_ref,
               vte_ref, bias_ref, u_ref, mx_ref, p_ref, al_ref, m_ref, acc_ref, *, n_loops, lam_init):
    b = pl.program_id(0)
    h = pl.program_id(1)
    t = ATT_TILE
    hd = DIFF_V_DIM

    @pl.when((b == 0) & (h == 0))
    def _():
        _fill_bias(bias_ref, t)
        p_ref[PIPE_LAG:] = jnp.zeros((PIPE_LAG, 2, t, t), BF16)
        al_ref[PIPE_LAG:] = jnp.zeros((PIPE_LAG, 2, 1, t), F32)
        acc_ref[...] = jnp.zeros_like(acc_ref)

    _stage_values(vt_ref, vte_ref, t)
    m_ref[...] = jnp.full_like(m_ref, NEG_INF)
    no_shift = jnp.zeros((1, t), F32)

    def rows(idx):
        return pl.ds(pl.multiple_of(idx * t, t), t)

    def score_stage(pos, slot):
        q = q_ref[rows(tab_ref[_T_QI, pos]), :]
        k = k_ref[rows(tab_ref[_T_KB, pos]), :]
        bias = bias_ref[tab_ref[_T_DIAG, pos]]
        lane = lax.broadcasted_iota(jnp.int32, q.shape, 1)
        comp0 = (lane < ROT_DIM // 2) | ((lane >= ROT_DIM) & (lane < LANES // 2 + ROT_DIM // 2))
        zero = jnp.zeros_like(q)
        for s, keep in enumerate((comp0, jnp.logical_not(comp0))):
            _store_scores(_dot_t(k, jnp.where(keep, q, zero)) + bias,
                          u_ref.at[slot, s], mx_ref.at[slot, s])

    def step(tau, slot):
        lagged = (slot + PIPE_LAG) % PIPE_SLOTS
        score_stage(tau + 2 * PIPE_LAG, lagged)
        state = tab_ref[_T_STATE, tau + PIPE_LAG]
        for s in range(2):
            _softmax_stage(u_ref.at[slot, s], mx_ref.at[slot, s], no_shift, m_ref.at[s, state],
                           p_ref.at[slot, s], al_ref.at[slot, s])
        value_stage(tau, lagged)

    def value_stage(tau, lagged):
        vt = vte_ref[tab_ref[_T_KB, tau]]
        for s in range(2):
            _value_stage(vt, p_ref.at[lagged, s], al_ref.at[lagged, s],
                         acc_ref.at[s, tab_ref[_T_STATE, tau]])

    for item in range(PIPE_LAG):
        score_stage(item + PIPE_LAG, item)

    def unrolled(it, carry):
        for slot in range(PIPE_SLOTS):
            step(PIPE_SLOTS * it + slot, slot)
        return carry

    lax.fori_loop(0, n_loops, unrolled, 0)
    for tau in range(PIPE_SLOTS * n_loops, PIPE_SLOTS * n_loops + PIPE_LAG):
        value_stage(tau, (tau + PIPE_LAG) % PIPE_SLOTS)

    lv = lam_ref[...]
    lam = (jnp.exp(jnp.sum(lv[0:1, :] * lv[1:2, :], axis=-1, keepdims=True))
           - jnp.exp(jnp.sum(lv[2:3, :] * lv[3:4, :], axis=-1, keepdims=True))
           + lam_init)
    for qi in range(acc_ref.shape[1] - 1):
        a1 = acc_ref[0, qi]
        a2 = acc_ref[1, qi]
        o = a1[:hd, :] / a1[hd:hd + 1, :] - lam * (a2[:hd, :] / a2[hd:hd + 1, :])
        o = o * lax.rsqrt(jnp.mean(o * o, axis=0, keepdims=True) + EPS)
        o_ref[qi * t:(qi + 1) * t, :] = (o.T * sg_ref[...] * (1.0 - lam_init)).astype(BF16)


def _diff_attention(qk, vt, lam_vecs, subln, l, *, batch, seq_len, lam_init):
    t = ATT_TILE
    nq = seq_len // t
    hd = DIFF_V_DIM
    rows = hd + ONES_ROWS
    base = 2 * FOX_HEADS
    tab, n_loops = _item_table(nq)
    grid_spec = pltpu.PrefetchScalarGridSpec(
        num_scalar_prefetch=1,
        grid=(batch, DIFF_HEADS),
        in_specs=[
            pl.BlockSpec((seq_len, hd), lambda b, h, tab: (b, base + h)),
            pl.BlockSpec((seq_len, hd), lambda b, h, tab: (b, base + DIFF_HEADS + h)),
            pl.BlockSpec((hd, seq_len), lambda b, h, tab: (h, b)),
            _layer_spec((4, DIFF_QK_DIM), lambda b, h, tab: (l, 0, 0)),
            _layer_spec((1, hd), lambda b, h, tab: (l, 0, 0)),
        ],
        out_specs=pl.BlockSpec((seq_len, hd), lambda b, h, tab: (b, h)),
        scratch_shapes=[pltpu.VMEM((nq, rows, t), BF16),
                        pltpu.VMEM((2, t, t), F32)] + _attention_scratch(t, nq, rows, 2),
    )
    return pl.pallas_call(
        functools.partial(_diff_body, n_loops=n_loops, lam_init=lam_init),
        grid_spec=grid_spec,
        out_shape=jax.ShapeDtypeStruct((batch * seq_len, DIFF_HEADS * hd), BF16),
        compiler_params=_params(2),
        name="diff_attention",
    )(jnp.asarray(tab), qk, qk, vt, lam_vecs, subln)


def _mixout_body(h_ref, fox_ref, diff_ref, conv_ref, wf_ref, wd_ref, wc_ref, o_ref):
    o_ref[...] = (h_ref[...] + _dot(fox_ref[...], wf_ref[...])
                  + _dot(diff_ref[...], wd_ref[...]) + _dot(conv_ref[...], wc_ref[...]))


def _mixout(h, fox, diff, conv, w_out, l):
    n_tok, d = h.shape
    tm = TOKEN_TILE
    fw, dw, cw = fox.shape[1], diff.shape[1], conv.shape[1]
    assert fw == dw and (fw + dw) % cw == 0
    row = lambda i: (i, 0)
    resident = lambda rows, blk: pl.BlockSpec((None, rows, d), lambda i: (l, blk, 0),
                                              pipeline_mode=pl.Buffered(1))
    return pl.pallas_call(
        _mixout_body,
        grid=(n_tok // tm,),
        in_specs=[
            pl.BlockSpec((tm, d), row),
            pl.BlockSpec((tm, fw), row),
            pl.BlockSpec((tm, dw), row),
            pl.BlockSpec((tm, cw), row),
            resident(fw, 0),
            resident(dw, 1),
            resident(cw, (fw + dw) // cw),
        ],
        out_specs=pl.BlockSpec((tm, d), row),
        out_shape=jax.ShapeDtypeStruct((n_tok, d), F32),
        compiler_params=_params(1),
        name="mixout",
    )(h, fox, diff, conv, w_out, w_out, w_out)


def _memkv_body(m_ref, g_ref, w_ref, o_ref):
    o_ref[...] = _dot(_rms(m_ref[...], g_ref[...]).astype(BF16), w_ref[...]).astype(BF16)


def _memkv(mem, g, w, l):
    n_mem, d = mem.shape
    tm = min(n_mem, TOKEN_TILE)
    width = w.shape[2]
    return pl.pallas_call(
        _memkv_body,
        grid=(n_mem // tm,),
        in_specs=[
            pl.BlockSpec((tm, d), lambda i: (i, 0)),
            _layer_spec((1, d), lambda i: (l, 0, 0)),
            pl.BlockSpec((None, d, width), lambda i: (l, 0, 0), pipeline_mode=pl.Buffered(1)),
        ],
        out_specs=pl.BlockSpec((tm, width), lambda i: (i, 0)),
        out_shape=jax.ShapeDtypeStruct((n_mem, width), BF16),
        compiler_params=_params(1),
        name="memkv",
    )(mem, g, w)


def _cross_body(h_ref, g_ref, wq_ref, kv_ref, wo_ref, o_ref):
    hd = CROSS_HEAD_DIM
    width = CROSS_HEADS * hd
    scale = hd ** -0.5
    x = h_ref[...]
    q = _dot(_rms(x, g_ref[...]).astype(BF16), wq_ref[...]).astype(BF16)
    heads = []
    for hh in range(CROSS_HEADS):
        k = kv_ref[:, hh * hd:(hh + 1) * hd]
        v = kv_ref[:, width + hh * hd:width + (hh + 1) * hd]
        s = _dot_t(q[:, hh * hd:(hh + 1) * hd], k) * scale
        e = jnp.exp(s - jnp.max(s, axis=-1, keepdims=True))
        p = e / jnp.sum(e, axis=-1, keepdims=True)
        heads.append(_dot(p.astype(BF16), v))
    o = jnp.concatenate(heads, axis=-1).astype(BF16)
    o_ref[...] = x + _dot(o, wo_ref[...])


def _cross(h, g, wq, kv, wo, l, *, seq_len, mem_len):
    n_tok, d = h.shape
    tm = TOKEN_TILE
    tiles_per_seq = seq_len // tm
    resident = lambda a: pl.BlockSpec((None,) + a.shape[1:], lambda i: (l, 0, 0),
                                      pipeline_mode=pl.Buffered(1))
    return pl.pallas_call(
        _cross_body,
        grid=(n_tok // tm,),
        in_specs=[
            pl.BlockSpec((tm, d), lambda i: (i, 0)),
            _layer_spec((1, d), lambda i: (l, 0, 0)),
            resident(wq),
            pl.BlockSpec((mem_len, kv.shape[1]), lambda i: (i // tiles_per_seq, 0)),
            resident(wo),
        ],
        out_specs=pl.BlockSpec((tm, d), lambda i: (i, 0)),
        out_shape=jax.ShapeDtypeStruct((n_tok, d), F32),
        compiler_params=_params(1),
        name="cross_attention",
    )(h, g, wq, kv, wo)


def _rope_tables(positions):
    half = ROT_DIM // 2
    inv_freq = ROPE_THETA ** (-jnp.arange(0, ROT_DIM, 2, dtype=F32) / ROT_DIM)
    ang = positions.astype(F32)[..., None] * inv_freq
    cos = jnp.cos(ang).reshape(-1, half)
    sin = jnp.sin(ang).reshape(-1, half)
    lane = np.arange(LANES)
    rotated = (lane % (LANES // 2)) < ROT_DIM
    sign = np.where(lane < LANES // 2, -1.0, 1.0).astype(np.float32)
    reps = LANES // half
    ct = jnp.where(rotated[None, :], jnp.tile(cos, (1, reps)), 1.0)
    st = jnp.where(rotated[None, :], jnp.tile(sin, (1, reps)) * sign[None, :], 0.0)
    return ct, st


def _diff_head_lanes():
    half = ROT_DIM // 2
    src = np.zeros(LANES, np.int64)
    comp0 = np.zeros(LANES, bool)
    for c in range(2):
        for d in range(DIFF_QK_DIM):
            if d < half:
                lane = c * half + d
            elif d < ROT_DIM:
                lane = LANES // 2 + c * half + (d - half)
            else:
                lane = (ROT_DIM if c == 0 else LANES // 2 + ROT_DIM) + (d - ROT_DIM)
            src[lane] = c * DIFF_QK_DIM + d
            comp0[lane] = c == 0
    return src, comp0


def _in_weight_columns():
    fw, dw, cc = FOX_HEADS * FOX_HEAD_DIM, DIFF_HEADS * DIFF_V_DIM, CONV_CH
    ff0 = 3 * fw
    d0 = ff0 + FOX_HEADS
    c0 = d0 + 3 * dw
    src, _ = _diff_head_lanes()
    head_perm = np.concatenate([h * LANES + src for h in range(DIFF_HEADS)])
    cols = [np.arange(2 * fw),
            2 * fw + np.arange(fw), d0 + 2 * dw + np.arange(dw),
            d0 + head_perm, d0 + dw + head_perm]
    for t in range(_N_CONV_TILES):
        cols += [c0 + k * cc + t * CONV_TILE + np.arange(CONV_TILE) for k in range(3)]
    return np.concatenate(cols), ff0


def _runs(cols):
    cuts = np.flatnonzero(np.diff(cols) != 1) + 1
    return [(int(c[0]), int(c[-1]) + 1) for c in np.split(cols, cuts)]


def _arrange_body(x_ref, w_ref, wff_ref, *, runs, ff0):
    n_layers = x_ref.shape[1]
    pad = jnp.zeros((LANES - FOX_HEADS, x_ref.shape[2]), F32)
    for l in range(n_layers):
        w_ref[l] = jnp.concatenate([x_ref[a:b, l, :] for a, b in runs], axis=0).astype(BF16)
        wff_ref[l] = jnp.concatenate([x_ref[ff0:ff0 + FOX_HEADS, l, :], pad], axis=0).astype(BF16)


def _arrange_in_weights(w_in):
    n_layers, d, width = w_in.shape
    cols, ff0 = _in_weight_columns()
    return pl.pallas_call(
        functools.partial(_arrange_body, runs=_runs(cols), ff0=ff0),
        grid=(d // LANES,),
        in_specs=[pl.BlockSpec((width, n_layers, LANES), lambda i: (0, 0, i))],
        out_specs=[pl.BlockSpec((n_layers, cols.size, LANES), lambda i: (0, 0, i)),
                   pl.BlockSpec((n_layers, LANES, LANES), lambda i: (0, 0, i))],
        out_shape=[jax.ShapeDtypeStruct((n_layers, cols.size, d), BF16),
                   jax.ShapeDtypeStruct((n_layers, LANES, d), BF16)],
        compiler_params=_params(1),
        name="arrange_in_weights",
    )(jnp.transpose(w_in, (2, 0, 1)))


def kernel(x, mem, positions, ffn1_norm, ffn1_w_gate, ffn1_w_up, ffn1_w_down, mix_norm, mix_w_in, forget_bias, conv_w, conv_b, lambda_q1, lambda_k1, lambda_q2, lambda_k2, diff_subln, mix_w_out, cross_norm, mem_norm, cross_w_q, cross_w_kv, cross_w_o, ffn2_norm, ffn2_w_gate, ffn2_w_up, ffn2_w_down, final_norm):
    batch, seq_len, d = x.shape
    mem_len = mem.shape[1]
    depth = ffn1_norm.shape[0]
    n_tok = batch * seq_len
    assert seq_len % TOKEN_TILE == 0 and seq_len % ATT_TILE == 0
    assert ffn1_w_gate.shape[2] % FF_TILE == 0

    bf = lambda a: a.astype(BF16)
    rows3 = lambda a: a.astype(F32).reshape(a.shape[0], 1, a.shape[1])
    ffn1_w = (ffn1_w_gate, ffn1_w_up, ffn1_w_down)
    ffn2_w = (ffn2_w_gate, ffn2_w_up, ffn2_w_down)
    ffn_w = tuple(bf(a[0]) for a in ffn1_w)
    w_in, w_ff = _arrange_in_weights(mix_w_in)
    w_out = bf(mix_w_out)
    wq, wkv, wo = bf(cross_w_q), bf(cross_w_kv), bf(cross_w_o)
    g_ffn1, g_mix, g_cross, g_mem, g_ffn2 = (rows3(a) for a in (ffn1_norm, mix_norm, cross_norm,
                                                                 mem_norm, ffn2_norm))
    g_final = final_norm.astype(F32).reshape(1, 1, d)
    fbias = rows3(jnp.pad(forget_bias, ((0, 0), (0, LANES - FOX_HEADS))))
    cbias = rows3(conv_b)
    subln = rows3(diff_subln)
    lam_vecs = jnp.stack([lambda_q1, lambda_k1, lambda_q2, lambda_k2], axis=1).astype(F32)
    ctab, stab = _rope_tables(positions)

    h = x.reshape(n_tok, d)
    mem2 = mem.reshape(batch * mem_len, d)
    for l in range(depth):
        h, ffn_w = _ffn(h, g_ffn1, ffn_w, g_final, l, ffn2_w + (l,))

        qk, conv, c, c_rows, fox_vt, diff_vt = _inproj(h, g_mix, w_in, w_ff, fbias, ctab, stab,
                                                       conv_w.astype(F32), cbias, l, seq_len=seq_len)
        fox = _fox_attention(qk, fox_vt, c, c_rows, batch=batch, seq_len=seq_len)
        lam_init = 0.8 - 0.6 * math.exp(-0.3 * l)
        diff = _diff_attention(qk, diff_vt, lam_vecs, subln, l, batch=batch, seq_len=seq_len,
                               lam_init=lam_init)
        h = _mixout(h, fox, diff, conv, w_out, l)

        kv = _memkv(mem2, g_mem, wkv, l)
        h = _cross(h, g_cross, wq, kv, wo, l, seq_len=seq_len, mem_len=mem_len)

        last = l == depth - 1
        h, ffn_w = _ffn(h, g_ffn2, ffn_w, g_final, l, None if last else ffn1_w + (l + 1,), final=last)
    return h.reshape(batch, seq_len, d)
```

```python
import functools
import math

import numpy as np
import jax
import jax.numpy as jnp
from jax import lax
from jax.experimental import pallas as pl
from jax.experimental.pallas import tpu as pltpu

F32 = jnp.float32
BF16 = jnp.bfloat16

FOX_HEADS = 6
FOX_HEAD_DIM = 128
DIFF_HEADS = 6
DIFF_QK_DIM = 64
DIFF_V_DIM = 128
CONV_CH = 512
CONV_WIDTH = 3
ROT_DIM = 16
ROPE_THETA = 500000.0
CROSS_HEADS = 4
CROSS_HEAD_DIM = 128
EPS = 1e-6
NEG_INF = -1e30
LOG2E = math.log2(math.e)
FOX_Q_SCALE = FOX_HEAD_DIM ** -0.5 * LOG2E
DIFF_Q_SCALE = DIFF_QK_DIM ** -0.5 * LOG2E

LANES = 128
SUBLANES = 8
BF16_ROWS = 16

TOKEN_TILE = 1024
FF_TILE = 512
IN_TILE = 768
CONV_TILE = 256
ATT_TILE = 512
ONES_ROWS = BF16_ROWS
VMEM_LIMIT = 56 * 1024 * 1024
BIG_VMEM_LIMIT = 60 * 1024 * 1024

_ARB = "arbitrary"


def _params(n_axes, vmem_limit=VMEM_LIMIT):
    return pltpu.CompilerParams(dimension_semantics=(_ARB,) * n_axes,
                                vmem_limit_bytes=vmem_limit)


def _rms(x, g):
    return x * lax.rsqrt(jnp.mean(x * x, axis=-1, keepdims=True) + EPS) * g


def _dot(a, b):
    return jnp.dot(a, b, preferred_element_type=F32)


def _dot_t(a, b):
    return lax.dot_general(a, b, (((1,), (1,)), ((), ())), preferred_element_type=F32)


def _layer_spec(shape, index_map):
    return pl.BlockSpec((None,) + tuple(shape), index_map)


def _ffn_body(*refs, final, convert_next):
    x_ref, g_ref, wg_ref, wu_ref, wd_ref, fg_ref = refs[:6]
    if convert_next:
        src_refs, o_ref, dst_refs, n_ref = refs[6:9], refs[9], refs[10:13], refs[13]
        for src_ref, dst_ref in zip(src_refs, dst_refs):
            dst_ref[...] = src_ref[...].astype(BF16)
    else:
        o_ref, n_ref = refs[6:]
    j = pl.program_id(1)

    @pl.when(j == 0)
    def _():
        x = x_ref[...]
        n_ref[...] = _rms(x, g_ref[...]).astype(BF16)
        o_ref[...] = x

    n = n_ref[...]
    gate = _dot(n, wg_ref[...])
    up = _dot(n, wu_ref[...])
    act = (gate * jax.nn.sigmoid(gate) * (0.5 * up)).astype(BF16)
    o_ref[...] += _dot(act, wd_ref[...])

    if final:
        @pl.when(j == pl.num_programs(1) - 1)
        def _():
            o_ref[...] = _rms(o_ref[...], fg_ref[...])


def _ffn(h, g, weights, final_g, l, next_weights=None, *, final=False):
    n_tok, d = h.shape
    wg, wu, wd = weights
    d_ff = wg.shape[1]
    tm, tf = TOKEN_TILE, FF_TILE
    ni, nj = n_tok // tm, d_ff // tf
    in_specs = [
        pl.BlockSpec((tm, d), lambda i, j: (i, 0)),
        _layer_spec((1, d), lambda i, j: (l, 0, 0)),
        pl.BlockSpec((d, tf), lambda i, j: (0, j)),
        pl.BlockSpec((d, tf), lambda i, j: (0, j)),
        pl.BlockSpec((tf, d), lambda i, j: (j, 0)),
        _layer_spec((1, d), lambda i, j: (0, 0, 0)),
    ]
    out_specs = [pl.BlockSpec((tm, d), lambda i, j: (i, 0))]
    out_shape = [jax.ShapeDtypeStruct((n_tok, d), F32)]
    operands = [h, g, wg, wu, wd, final_g]
    convert_next = next_weights is not None
    if convert_next:
        ng, nu, nd, nl = next_weights
        assert d % ni == 0 and (d // ni) % BF16_ROWS == 0
        dr = d // ni
        in_specs += [_layer_spec((dr, tf), lambda i, j: (nl, i, j)),
                     _layer_spec((dr, tf), lambda i, j: (nl, i, j)),
                     _layer_spec((tf, dr), lambda i, j: (nl, j, i))]
        out_specs += [pl.BlockSpec((dr, tf), lambda i, j: (i, j)),
                      pl.BlockSpec((dr, tf), lambda i, j: (i, j)),
                      pl.BlockSpec((tf, dr), lambda i, j: (j, i))]
        out_shape += [jax.ShapeDtypeStruct(a.shape[1:], BF16) for a in (ng, nu, nd)]
        operands += [ng, nu, nd]
    outs = pl.pallas_call(
        functools.partial(_ffn_body, final=final, convert_next=convert_next),
        grid=(ni, nj),
        in_specs=in_specs,
        out_specs=out_specs,
        out_shape=out_shape,
        scratch_shapes=[pltpu.VMEM((tm, d), BF16)],
        compiler_params=_params(2, BIG_VMEM_LIMIT),
        name="ffn_final" if final else "ffn",
    )(*operands)
    return outs[0], (tuple(outs[1:]) if convert_next else None)


_J_FOX_QK, _J_VALUES, _J_DIFF_QK, _J_CONV = 0, 1, 2, 3
_N_CONV_TILES = CONV_CH // CONV_TILE
_N_IN_PAIRS = 4
assert _N_CONV_TILES == 2


def _inproj_body(x_ref, g_ref, w_ref, wff_ref, fb_ref, ct_ref, st_ref, cw_ref, cb_ref,
                 qk_ref, conv_ref, c_ref, crow_ref, fvt_ref, dvt_ref, n_ref, zbuf_ref, carry_ref,
                 *, tiles_per_seq):
    i = pl.program_id(0)
    j = pl.program_id(1)
    tm = x_ref.shape[0]
    seq_start = (i % tiles_per_seq) == 0

    @pl.when(j == 0)
    def _():
        n_ref[...] = _rms(x_ref[...], g_ref[...]).astype(BF16)

        @pl.when(seq_start)
        def _():
            zbuf_ref[:, 0:SUBLANES, :] = jnp.zeros((_N_CONV_TILES, SUBLANES, CONV_TILE), F32)

    def project(half):
        return _dot_t(n_ref[...], w_ref[half * IN_TILE:(half + 1) * IN_TILE, :])

    @pl.when(j == _J_FOX_QK)
    def _():
        qk_ref[:, 0:IN_TILE] = (project(0) * FOX_Q_SCALE).astype(BF16)
        qk_ref[:, IN_TILE:] = project(1).astype(BF16)
        logf = jax.nn.log_sigmoid(_dot_t(n_ref[...], wff_ref[...]) + fb_ref[...])
        lt = logf.T[0:BF16_ROWS, :]
        hi = lt.astype(BF16)
        r1 = lt - hi.astype(F32)
        mid = r1.astype(BF16)
        lo = (r1 - mid.astype(F32)).astype(BF16)
        src = lax.broadcasted_iota(jnp.int32, (tm, tm), 0)
        dst = lax.broadcasted_iota(jnp.int32, (tm, tm), 1)
        tri = jnp.where(src <= dst, 1.0, 0.0).astype(BF16)
        parts = _dot(jnp.concatenate([hi, mid, lo], axis=0), tri)
        prev = jnp.where(seq_start, 0.0, carry_ref[...])
        ct = (parts[0:BF16_ROWS] + parts[BF16_ROWS:2 * BF16_ROWS] + parts[2 * BF16_ROWS:]) + prev
        carry_ref[...] = ct[:, tm - 1:tm]
        c_ref[...] = jnp.concatenate([ct, jnp.zeros((LANES - BF16_ROWS, tm), F32)], axis=0).T
        for k in range(tm // ATT_TILE):
            crow_ref[k] = ct[:, k * ATT_TILE:(k + 1) * ATT_TILE]

    @pl.when(j == _J_VALUES)
    def _():
        fvt_ref[...] = project(0).T.astype(BF16)
        dvt_ref[...] = project(1).T.astype(BF16)

    @pl.when(j == _J_DIFF_QK)
    def _():
        ct = ct_ref[...]
        st = st_ref[...]
        for half, q_scale in ((0, DIFF_Q_SCALE), (1, None)):
            y = project(half)
            for c0 in range(0, IN_TILE, LANES):
                yc = y[:, c0:c0 + LANES]
                out = yc * ct + pltpu.roll(yc, LANES // 2, axis=1) * st
                if q_scale is not None:
                    out = out * q_scale
                qk_ref[:, half * IN_TILE + c0:half * IN_TILE + c0 + LANES] = out.astype(BF16)

    @pl.when(j == _J_CONV)
    def _():
        cw = cw_ref[...]
        cb = cb_ref[...]
        for t in range(_N_CONV_TILES):
            ch = slice(t * CONV_TILE, (t + 1) * CONV_TILE)
            y = project(t)
            gb = y[:, 0:CONV_TILE]
            z = y[:, CONV_TILE:2 * CONV_TILE] * y[:, 2 * CONV_TILE:3 * CONV_TILE]
            zb = zbuf_ref.at[t]
            zb[SUBLANES:SUBLANES + tm, :] = z
            z1 = zb[SUBLANES - 1:SUBLANES - 1 + tm, :]
            z2 = zb[SUBLANES - 2:SUBLANES - 2 + tm, :]
            conv = z2 * cw[0:1, ch] + z1 * cw[1:2, ch] + z * cw[2:3, ch] + cb[:, ch]
            conv_ref[:, ch] = (gb * conv).astype(BF16)
            zb[0:SUBLANES, :] = z[tm - SUBLANES:tm, :]


def _inproj(h, g, w, wff, fbias, ctab, stab, conv_w, conv_b, l, *, seq_len):
    n_tok, d = h.shape
    tm = TOKEN_TILE
    pair = 2 * IN_TILE
    return pl.pallas_call(
        functools.partial(_inproj_body, tiles_per_seq=seq_len // tm),
        grid=(n_tok // tm, _N_IN_PAIRS),
        in_specs=[
            pl.BlockSpec((tm, d), lambda i, j: (i, 0)),
            _layer_spec((1, d), lambda i, j: (l, 0, 0)),
            _layer_spec((pair, d), lambda i, j: (l, j, 0)),
            _layer_spec((LANES, d), lambda i, j: (l, 0, 0)),
            _layer_spec((1, LANES), lambda i, j: (l, 0, 0)),
            pl.BlockSpec((tm, LANES), lambda i, j: (i, 0)),
            pl.BlockSpec((tm, LANES), lambda i, j: (i, 0)),
            _layer_spec((CONV_WIDTH, CONV_CH), lambda i, j: (l, 0, 0)),
            _layer_spec((1, CONV_CH), lambda i, j: (l, 0, 0)),
        ],
        out_specs=[
            pl.BlockSpec((tm, pair), lambda i, j: (i, j // 2)),
            pl.BlockSpec((tm, CONV_CH), lambda i, j: (i, 0)),
            pl.BlockSpec((tm, LANES), lambda i, j: (i, 0)),
            pl.BlockSpec((tm // ATT_TILE, BF16_ROWS, ATT_TILE), lambda i, j: (i, 0, 0)),
            pl.BlockSpec((IN_TILE, tm), lambda i, j: (0, i)),
            pl.BlockSpec((IN_TILE, tm), lambda i, j: (0, i)),
        ],
        out_shape=[
            jax.ShapeDtypeStruct((n_tok, 2 * pair), BF16),
            jax.ShapeDtypeStruct((n_tok, CONV_CH), BF16),
            jax.ShapeDtypeStruct((n_tok, LANES), F32),
            jax.ShapeDtypeStruct((n_tok // ATT_TILE, BF16_ROWS, ATT_TILE), F32),
            jax.ShapeDtypeStruct((IN_TILE, n_tok), BF16),
            jax.ShapeDtypeStruct((IN_TILE, n_tok), BF16),
        ],
        scratch_shapes=[
            pltpu.VMEM((tm, d), BF16),
            pltpu.VMEM((_N_CONV_TILES, tm + SUBLANES, CONV_TILE), F32),
            pltpu.VMEM((BF16_ROWS, 1), F32),
        ],
        compiler_params=_params(2, BIG_VMEM_LIMIT),
        name="inproj",
    )(h, g, w, wff, fbias, ctab, stab, conv_w, conv_b)


_T_QI, _T_KB, _T_DIAG, _T_STATE = 0, 1, 2, 3
PIPE_LAG = 2
PIPE_SLOTS = 2 * PIPE_LAG


def _item_table(nq):
    streams, tabs, base = [], [], 0
    for masked in (False, True):
        items = [(qi, kb) for qi in range(nq) for kb in range(qi + 1) if (kb == qi) == masked]
        if not items:
            continue
        n_loops = -(-len(items) // PIPE_SLOTS)
        n_pos = PIPE_SLOTS * n_loops + 2 * PIPE_LAG
        tab = np.zeros((4, n_pos), np.int32)
        for pos in range(n_pos):
            item = pos - PIPE_LAG
            qi, kb = items[min(max(item, 0), len(items) - 1)]
            real = 0 <= item < len(items)
            tab[:, pos] = (qi, kb, int(real and kb == qi), qi if real else nq)
        streams.append((base, n_loops, masked))
        tabs.append(tab)
        base += n_pos
    return np.concatenate(tabs, axis=1), tuple(streams)


def _fill_bias(bias_ref, t):
    key = lax.broadcasted_iota(jnp.int32, (t, t), 0)
    qry = lax.broadcasted_iota(jnp.int32, (t, t), 1)
    bias_ref[0] = jnp.zeros((t, t), F32)
    bias_ref[1] = jnp.where(key <= qry, 0.0, NEG_INF)


def _stage_values(vt_ref, vte_ref, t):
    hd = vt_ref.shape[0]
    for kb in range(vte_ref.shape[0]):
        vte_ref[kb, 0:hd, :] = vt_ref[:, kb * t:(kb + 1) * t]
        vte_ref[kb, hd:, :] = jnp.ones((ONES_ROWS, t), BF16)


def _store_scores(u, u_ref, mx_ref):
    u_ref[...] = u
    mx_ref[...] = jnp.max(u, axis=0, keepdims=True)


def _softmax_stage(u_ref, mx_ref, shift, m_ref, p_ref, al_ref):
    m_old = m_ref[...]
    m_new = jnp.maximum(m_old, mx_ref[...] + shift)
    al_ref[...] = jnp.exp2(m_old - m_new)
    p_ref[...] = jnp.exp2(u_ref[...] - (m_new - shift)).astype(BF16)
    m_ref[...] = m_new


def _value_stage(vt, p_ref, al_ref, acc_ref):
    acc_ref[...] = al_ref[...] * acc_ref[...] + _dot(vt, p_ref[...])


def _run_streams(streams, score_stage, step, value_stage):
    for base, n_loops, masked in streams:
        for item in range(PIPE_LAG):
            score_stage(base + item + PIPE_LAG, item, masked)

        def unrolled(it, carry, base=base, masked=masked):
            for slot in range(PIPE_SLOTS):
                step(base + PIPE_SLOTS * it + slot, slot, masked)
            return carry

        lax.fori_loop(0, n_loops, unrolled, 0)
        for tau in range(PIPE_SLOTS * n_loops, PIPE_SLOTS * n_loops + PIPE_LAG):
            value_stage(base + tau, (tau + PIPE_LAG) % PIPE_SLOTS)


def _fox_body(tab_ref, q_ref, k_ref, vt_ref, c_ref, crow_ref, o_ref,
              ckb_ref, vte_ref, bias_ref, u_ref, mx_ref, p_ref, al_ref, m_ref, acc_ref, *, streams):
    b = pl.program_id(0)
    h = pl.program_id(1)
    t = ATT_TILE
    hd = FOX_HEAD_DIM

    @pl.when((b == 0) & (h == 0))
    def _():
        _fill_bias(bias_ref, t)
        p_ref[PIPE_LAG:] = jnp.zeros((PIPE_LAG, t, t), BF16)
        al_ref[PIPE_LAG:] = jnp.zeros((PIPE_LAG, 1, t), F32)
        acc_ref[...] = jnp.zeros_like(acc_ref)

    lane = lax.broadcasted_iota(jnp.int32, c_ref.shape, 1)
    ck = jnp.sum(jnp.where(lane == h, c_ref[...], 0.0), axis=-1, keepdims=True)
    ckb_ref[...] = jnp.broadcast_to(ck * LOG2E, ckb_ref.shape)
    _stage_values(vt_ref, vte_ref, t)
    m_ref[...] = jnp.full_like(m_ref, NEG_INF)

    def rows(idx):
        return pl.ds(pl.multiple_of(idx * t, t), t)

    def score_stage(pos, slot, masked):
        kb = tab_ref[_T_KB, pos]
        u = (_dot_t(k_ref[rows(kb), :], q_ref[rows(tab_ref[_T_QI, pos]), :])
             - jnp.tile(ckb_ref[rows(kb), :], (1, t // LANES)))
        if masked:
            u = u + bias_ref[1]
        _store_scores(u, u_ref.at[slot], mx_ref.at[slot])

    def step(tau, slot, masked):
        lagged = (slot + PIPE_LAG) % PIPE_SLOTS
        score_stage(tau + 2 * PIPE_LAG, lagged, masked)
        pos = tau + PIPE_LAG
        cq = crow_ref[tab_ref[_T_QI, pos], pl.ds(h, 1), :] * LOG2E
        _softmax_stage(u_ref.at[slot], mx_ref.at[slot], cq, m_ref.at[tab_ref[_T_STATE, pos]],
                       p_ref.at[slot], al_ref.at[slot])
        value_stage(tau, lagged)

    def value_stage(tau, lagged):
        _value_stage(vte_ref[tab_ref[_T_KB, tau]], p_ref.at[lagged], al_ref.at[lagged],
                     acc_ref.at[tab_ref[_T_STATE, tau]])

    _run_streams(streams, score_stage, step, value_stage)

    for qi in range(acc_ref.shape[0] - 1):
        acc = acc_ref[qi]
        o_ref[qi * t:(qi + 1) * t, :] = (acc[:hd, :] / acc[hd:hd + 1, :]).T.astype(BF16)


def _attention_scratch(t, nq, rows, n_streams):
    shape = lambda *s: ((n_streams,) if n_streams > 1 else ()) + s
    return [
        pltpu.VMEM((PIPE_SLOTS,) + shape(t, t), F32),
        pltpu.VMEM((PIPE_SLOTS,) + shape(1, t), F32),
        pltpu.VMEM((PIPE_SLOTS,) + shape(t, t), BF16),
        pltpu.VMEM((PIPE_SLOTS,) + shape(1, t), F32),
        pltpu.VMEM(shape(nq + 1, 1, t), F32),
        pltpu.VMEM(shape(nq + 1, rows, t), F32),
    ]


def _fox_attention(qk, vt, c, c_rows, *, batch, seq_len):
    t = ATT_TILE
    nq = seq_len // t
    hd = FOX_HEAD_DIM
    rows = hd + ONES_ROWS
    tab, streams = _item_table(nq)
    grid_spec = pltpu.PrefetchScalarGridSpec(
        num_scalar_prefetch=1,
        grid=(batch, FOX_HEADS),
        in_specs=[
            pl.BlockSpec((seq_len, hd), lambda b, h, tab: (b, h)),
            pl.BlockSpec((seq_len, hd), lambda b, h, tab: (b, FOX_HEADS + h)),
            pl.BlockSpec((hd, seq_len), lambda b, h, tab: (h, b)),
            pl.BlockSpec((seq_len, LANES), lambda b, h, tab: (b, 0)),
            pl.BlockSpec((nq, BF16_ROWS, t), lambda b, h, tab: (b, 0, 0)),
        ],
        out_specs=pl.BlockSpec((seq_len, hd), lambda b, h, tab: (b, h)),
        scratch_shapes=[pltpu.VMEM((seq_len, LANES), F32),
                        pltpu.VMEM((nq, rows, t), BF16),
                        pltpu.VMEM((2, t, t), F32)] + _attention_scratch(t, nq, rows, 1),
    )
    return pl.pallas_call(
        functools.partial(_fox_body, streams=streams),
        grid_spec=grid_spec,
        out_shape=jax.ShapeDtypeStruct((batch * seq_len, FOX_HEADS * hd), BF16),
        compiler_params=_params(2),
        name="fox_attention",
    )(jnp.asarray(tab), qk, qk, vt, c, c_rows)


def _diff_body(tab_ref, q_ref, k_ref, vt_ref, lam_ref, sg_ref, o_ref,
               vte_ref, bias_ref, u_ref, mx_ref, p_ref, al_ref, m_ref, acc_ref, *, streams, lam_init):
    b = pl.program_id(0)
    h = pl.program_id(1)
    t = ATT_TILE
    hd = DIFF_V_DIM

    @pl.when((b == 0) & (h == 0))
    def _():
        _fill_bias(bias_ref, t)
        p_ref[PIPE_LAG:] = jnp.zeros((PIPE_LAG, 2, t, t), BF16)
        al_ref[PIPE_LAG:] = jnp.zeros((PIPE_LAG, 2, 1, t), F32)
        acc_ref[...] = jnp.zeros_like(acc_ref)

    _stage_values(vt_ref, vte_ref, t)
    m_ref[...] = jnp.full_like(m_ref, NEG_INF)
    no_shift = jnp.zeros((1, t), F32)

    def rows(idx):
        return pl.ds(pl.multiple_of(idx * t, t), t)

    def score_stage(pos, slot, masked):
        q = q_ref[rows(tab_ref[_T_QI, pos]), :]
        k = k_ref[rows(tab_ref[_T_KB, pos]), :]
        lane = lax.broadcasted_iota(jnp.int32, q.shape, 1)
        comp0 = (lane < ROT_DIM // 2) | ((lane >= ROT_DIM) & (lane < LANES // 2 + ROT_DIM // 2))
        zero = jnp.zeros_like(q)
        for s, keep in enumerate((comp0, jnp.logical_not(comp0))):
            u = _dot_t(k, jnp.where(keep, q, zero))
            if masked:
                u = u + bias_ref[1]
            _store_scores(u, u_ref.at[slot, s], mx_ref.at[slot, s])

    def step(tau, slot, masked):
        lagged = (slot + PIPE_LAG) % PIPE_SLOTS
        score_stage(tau + 2 * PIPE_LAG, lagged, masked)
        state = tab_ref[_T_STATE, tau + PIPE_LAG]
        for s in range(2):
            _softmax_stage(u_ref.at[slot, s], mx_ref.at[slot, s], no_shift, m_ref.at[s, state],
                           p_ref.at[slot, s], al_ref.at[slot, s])
        value_stage(tau, lagged)

    def value_stage(tau, lagged):
        vt = vte_ref[tab_ref[_T_KB, tau]]
        for s in range(2):
            _value_stage(vt, p_ref.at[lagged, s], al_ref.at[lagged, s],
                         acc_ref.at[s, tab_ref[_T_STATE, tau]])

    _run_streams(streams, score_stage, step, value_stage)

    lv = lam_ref[...]
    lam = (jnp.exp(jnp.sum(lv[0:1, :] * lv[1:2, :], axis=-1, keepdims=True))
           - jnp.exp(jnp.sum(lv[2:3, :] * lv[3:4, :], axis=-1, keepdims=True))
           + lam_init)
    for qi in range(acc_ref.shape[1] - 1):
        a1 = acc_ref[0, qi]
        a2 = acc_ref[1, qi]
        o = a1[:hd, :] / a1[hd:hd + 1, :] - lam * (a2[:hd, :] / a2[hd:hd + 1, :])
        o = o * lax.rsqrt(jnp.mean(o * o, axis=0, keepdims=True) + EPS)
        o_ref[qi * t:(qi + 1) * t, :] = (o.T * sg_ref[...] * (1.0 - lam_init)).astype(BF16)


def _diff_attention(qk, vt, lam_vecs, subln, l, *, batch, seq_len, lam_init):
    t = ATT_TILE
    nq = seq_len // t
    hd = DIFF_V_DIM
    rows = hd + ONES_ROWS
    base = 2 * FOX_HEADS
    tab, streams = _item_table(nq)
    grid_spec = pltpu.PrefetchScalarGridSpec(
        num_scalar_prefetch=1,
        grid=(batch, DIFF_HEADS),
        in_specs=[
            pl.BlockSpec((seq_len, hd), lambda b, h, tab: (b, base + h)),
            pl.BlockSpec((seq_len, hd), lambda b, h, tab: (b, base + DIFF_HEADS + h)),
            pl.BlockSpec((hd, seq_len), lambda b, h, tab: (h, b)),
            _layer_spec((4, DIFF_QK_DIM), lambda b, h, tab: (l, 0, 0)),
            _layer_spec((1, hd), lambda b, h, tab: (l, 0, 0)),
        ],
        out_specs=pl.BlockSpec((seq_len, hd), lambda b, h, tab: (b, h)),
        scratch_shapes=[pltpu.VMEM((nq, rows, t), BF16),
                        pltpu.VMEM((2, t, t), F32)] + _attention_scratch(t, nq, rows, 2),
    )
    return pl.pallas_call(
        functools.partial(_diff_body, streams=streams, lam_init=lam_init),
        grid_spec=grid_spec,
        out_shape=jax.ShapeDtypeStruct((batch * seq_len, DIFF_HEADS * hd), BF16),
        compiler_params=_params(2),
        name="diff_attention",
    )(jnp.asarray(tab), qk, qk, vt, lam_vecs, subln)


def _mixout_body(h_ref, fox_ref, diff_ref, conv_ref, wf_ref, wd_ref, wc_ref, o_ref):
    o_ref[...] = (h_ref[...] + _dot(fox_ref[...], wf_ref[...])
                  + _dot(diff_ref[...], wd_ref[...]) + _dot(conv_ref[...], wc_ref[...]))


def _mixout(h, fox, diff, conv, w_out, l):
    n_tok, d = h.shape
    tm = TOKEN_TILE
    fw, dw, cw = fox.shape[1], diff.shape[1], conv.shape[1]
    assert fw == dw and (fw + dw) % cw == 0
    row = lambda i: (i, 0)
    resident = lambda rows, blk: pl.BlockSpec((None, rows, d), lambda i: (l, blk, 0),
                                              pipeline_mode=pl.Buffered(1))
    return pl.pallas_call(
        _mixout_body,
        grid=(n_tok // tm,),
        in_specs=[
            pl.BlockSpec((tm, d), row),
            pl.BlockSpec((tm, fw), row),
            pl.BlockSpec((tm, dw), row),
            pl.BlockSpec((tm, cw), row),
            resident(fw, 0),
            resident(dw, 1),
            resident(cw, (fw + dw) // cw),
        ],
        out_specs=pl.BlockSpec((tm, d), row),
        out_shape=jax.ShapeDtypeStruct((n_tok, d), F32),
        compiler_params=_params(1),
        name="mixout",
    )(h, fox, diff, conv, w_out, w_out, w_out)


def _memkv_body(m_ref, g_ref, w_ref, o_ref):
    o_ref[...] = _dot(_rms(m_ref[...], g_ref[...]).astype(BF16), w_ref[...]).astype(BF16)


def _memkv(mem, g, w, l):
    n_mem, d = mem.shape
    tm = min(n_mem, TOKEN_TILE)
    width = w.shape[2]
    return pl.pallas_call(
        _memkv_body,
        grid=(n_mem // tm,),
        in_specs=[
            pl.BlockSpec((tm, d), lambda i: (i, 0)),
            _layer_spec((1, d), lambda i: (l, 0, 0)),
            pl.BlockSpec((None, d, width), lambda i: (l, 0, 0), pipeline_mode=pl.Buffered(1)),
        ],
        out_specs=pl.BlockSpec((tm, width), lambda i: (i, 0)),
        out_shape=jax.ShapeDtypeStruct((n_mem, width), BF16),
        compiler_params=_params(1),
        name="memkv",
    )(mem, g, w)


def _cross_body(h_ref, g_ref, wq_ref, kv_ref, wo_ref, o_ref):
    hd = CROSS_HEAD_DIM
    width = CROSS_HEADS * hd
    scale = hd ** -0.5
    x = h_ref[...]
    q = _dot(_rms(x, g_ref[...]).astype(BF16), wq_ref[...]).astype(BF16)
    heads = []
    for hh in range(CROSS_HEADS):
        k = kv_ref[:, hh * hd:(hh + 1) * hd]
        v = kv_ref[:, width + hh * hd:width + (hh + 1) * hd]
        s = _dot_t(q[:, hh * hd:(hh + 1) * hd], k) * scale
        e = jnp.exp(s - jnp.max(s, axis=-1, keepdims=True))
        p = e / jnp.sum(e, axis=-1, keepdims=True)
        heads.append(_dot(p.astype(BF16), v))
    o = jnp.concatenate(heads, axis=-1).astype(BF16)
    o_ref[...] = x + _dot(o, wo_ref[...])


def _cross(h, g, wq, kv, wo, l, *, seq_len, mem_len):
    n_tok, d = h.shape
    tm = TOKEN_TILE
    tiles_per_seq = seq_len // tm
    resident = lambda a: pl.BlockSpec((None,) + a.shape[1:], lambda i: (l, 0, 0),
                                      pipeline_mode=pl.Buffered(1))
    return pl.pallas_call(
        _cross_body,
        grid=(n_tok // tm,),
        in_specs=[
            pl.BlockSpec((tm, d), lambda i: (i, 0)),
            _layer_spec((1, d), lambda i: (l, 0, 0)),
            resident(wq),
            pl.BlockSpec((mem_len, kv.shape[1]), lambda i: (i // tiles_per_seq, 0)),
            resident(wo),
        ],
        out_specs=pl.BlockSpec((tm, d), lambda i: (i, 0)),
        out_shape=jax.ShapeDtypeStruct((n_tok, d), F32),
        compiler_params=_params(1),
        name="cross_attention",
    )(h, g, wq, kv, wo)


def _rope_tables(positions):
    half = ROT_DIM // 2
    inv_freq = ROPE_THETA ** (-jnp.arange(0, ROT_DIM, 2, dtype=F32) / ROT_DIM)
    ang = positions.astype(F32)[..., None] * inv_freq
    cos = jnp.cos(ang).reshape(-1, half)
    sin = jnp.sin(ang).reshape(-1, half)
    lane = np.arange(LANES)
    rotated = (lane % (LANES // 2)) < ROT_DIM
    sign = np.where(lane < LANES // 2, -1.0, 1.0).astype(np.float32)
    reps = LANES // half
    ct = jnp.where(rotated[None, :], jnp.tile(cos, (1, reps)), 1.0)
    st = jnp.where(rotated[None, :], jnp.tile(sin, (1, reps)) * sign[None, :], 0.0)
    return ct, st


def _diff_head_lanes():
    half = ROT_DIM // 2
    src = np.zeros(LANES, np.int64)
    comp0 = np.zeros(LANES, bool)
    for c in range(2):
        for d in range(DIFF_QK_DIM):
            if d < half:
                lane = c * half + d
            elif d < ROT_DIM:
                lane = LANES // 2 + c * half + (d - half)
            else:
                lane = (ROT_DIM if c == 0 else LANES // 2 + ROT_DIM) + (d - ROT_DIM)
            src[lane] = c * DIFF_QK_DIM + d
            comp0[lane] = c == 0
    return src, comp0


def _in_weight_columns():
    fw, dw, cc = FOX_HEADS * FOX_HEAD_DIM, DIFF_HEADS * DIFF_V_DIM, CONV_CH
    ff0 = 3 * fw
    d0 = ff0 + FOX_HEADS
    c0 = d0 + 3 * dw
    src, _ = _diff_head_lanes()
    head_perm = np.concatenate([h * LANES + src for h in range(DIFF_HEADS)])
    cols = [np.arange(2 * fw),
            2 * fw + np.arange(fw), d0 + 2 * dw + np.arange(dw),
            d0 + head_perm, d0 + dw + head_perm]
    for t in range(_N_CONV_TILES):
        cols += [c0 + k * cc + t * CONV_TILE + np.arange(CONV_TILE) for k in range(3)]
    return np.concatenate(cols), ff0


def _runs(cols):
    cuts = np.flatnonzero(np.diff(cols) != 1) + 1
    return [(int(c[0]), int(c[-1]) + 1) for c in np.split(cols, cuts)]


def _arrange_body(x_ref, w_ref, wff_ref, *, runs, ff0):
    n_layers = x_ref.shape[1]
    pad = jnp.zeros((LANES - FOX_HEADS, x_ref.shape[2]), F32)
    for l in range(n_layers):
        w_ref[l] = jnp.concatenate([x_ref[a:b, l, :] for a, b in runs], axis=0).astype(BF16)
        wff_ref[l] = jnp.concatenate([x_ref[ff0:ff0 + FOX_HEADS, l, :], pad], axis=0).astype(BF16)


def _arrange_in_weights(w_in):
    n_layers, d, width = w_in.shape
    cols, ff0 = _in_weight_columns()
    return pl.pallas_call(
        functools.partial(_arrange_body, runs=_runs(cols), ff0=ff0),
        grid=(d // LANES,),
        in_specs=[pl.BlockSpec((width, n_layers, LANES), lambda i: (0, 0, i))],
        out_specs=[pl.BlockSpec((n_layers, cols.size, LANES), lambda i: (0, 0, i)),
                   pl.BlockSpec((n_layers, LANES, LANES), lambda i: (0, 0, i))],
        out_shape=[jax.ShapeDtypeStruct((n_layers, cols.size, d), BF16),
                   jax.ShapeDtypeStruct((n_layers, LANES, d), BF16)],
        compiler_params=_params(1),
        name="arrange_in_weights",
    )(jnp.transpose(w_in, (2, 0, 1)))


def kernel(x, mem, positions, ffn1_norm, ffn1_w_gate, ffn1_w_up, ffn1_w_down, mix_norm, mix_w_in, forget_bias, conv_w, conv_b, lambda_q1, lambda_k1, lambda_q2, lambda_k2, diff_subln, mix_w_out, cross_norm, mem_norm, cross_w_q, cross_w_kv, cross_w_o, ffn2_norm, ffn2_w_gate, ffn2_w_up, ffn2_w_down, final_norm):
    batch, seq_len, d = x.shape
    mem_len = mem.shape[1]
    depth = ffn1_norm.shape[0]
    n_tok = batch * seq_len
    assert seq_len % TOKEN_TILE == 0 and seq_len % ATT_TILE == 0
    assert ffn1_w_gate.shape[2] % FF_TILE == 0

    bf = lambda a: a.astype(BF16)
    rows3 = lambda a: a.astype(F32).reshape(a.shape[0], 1, a.shape[1])
    ffn1_w = (ffn1_w_gate, ffn1_w_up, ffn1_w_down)
    ffn2_w = (ffn2_w_gate, ffn2_w_up, ffn2_w_down)
    ffn_w = tuple(bf(a[0]) for a in ffn1_w)
    w_in, w_ff = _arrange_in_weights(mix_w_in)
    w_out = bf(mix_w_out)
    wq, wkv, wo = bf(cross_w_q), bf(cross_w_kv), bf(cross_w_o)
    g_ffn1, g_mix, g_cross, g_mem, g_ffn2 = (rows3(a) for a in (ffn1_norm, mix_norm, cross_norm,
                                                                 mem_norm, ffn2_norm))
    g_final = final_norm.astype(F32).reshape(1, 1, d)
    fbias = rows3(jnp.pad(forget_bias, ((0, 0), (0, LANES - FOX_HEADS))))
    cbias = rows3(conv_b)
    subln = rows3(diff_subln)
    lam_vecs = jnp.stack([lambda_q1, lambda_k1, lambda_q2, lambda_k2], axis=1).astype(F32)
    ctab, stab = _rope_tables(positions)

    h = x.reshape(n_tok, d)
    mem2 = mem.reshape(batch * mem_len, d)
    for l in range(depth):
        h, ffn_w = _ffn(h, g_ffn1, ffn_w, g_final, l, ffn2_w + (l,))

        qk, conv, c, c_rows, fox_vt, diff_vt = _inproj(h, g_mix, w_in, w_ff, fbias, ctab, stab,
                                                       conv_w.astype(F32), cbias, l, seq_len=seq_len)
        fox = _fox_attention(qk, fox_vt, c, c_rows, batch=batch, seq_len=seq_len)
        lam_init = 0.8 - 0.6 * math.exp(-0.3 * l)
        diff = _diff_attention(qk, diff_vt, lam_vecs, subln, l, batch=batch, seq_len=seq_len,
                               lam_init=lam_init)
        h = _mixout(h, fox, diff, conv, w_out, l)

        kv = _memkv(mem2, g_mem, wkv, l)
        h = _cross(h, g_cross, wq, kv, wo, l, seq_len=seq_len, mem_len=mem_len)

        last = l == depth - 1
        h, ffn_w = _ffn(h, g_ffn2, ffn_w, g_final, l, None if last else ffn1_w + (l + 1,), final=last)
    return h.reshape(batch, seq_len, d)
```
